```python
import math
import jax, jax.numpy as jnp
from jax import lax
import numpy as np

D_MODEL = 2048
BATCH = 1
SEQ = 8192
DEPTH = 2

N_MEM = 256
S5_WIDTH = 1024
S5_GROUP = 16
S5_GROUPS = S5_WIDTH // S5_GROUP
S5_STATE = 64
S5_DT_MIN = 1e-3
S5_DT_MAX = 1e-1
MLA_HEADS = 16
MLA_Q_RANK = 448
MLA_KV_RANK = 512
MLA_NOPE = 128
MLA_ROPE = 64
MLA_QK = MLA_NOPE + MLA_ROPE
MLA_V = 128
ROPE_THETA = 10000.0
Q_BLOCK = 128
MEM_HEADS = 4
MEM_HEAD_DIM = 256
MEM_WIDTH = MEM_HEADS * MEM_HEAD_DIM
N_BRANCH = 3
S5_END = S5_WIDTH
CQ_END = S5_END + MLA_Q_RANK
CKV_END = CQ_END + MLA_KV_RANK
KR_END = CKV_END + MLA_ROPE
MQ_END = KR_END + MEM_WIDTH
IN_COLS = MQ_END + N_BRANCH * D_MODEL
D_FF = 7168
N_EXPERTS = 8
TOP_K = 2
D_FF_EXPERT = 7168
N_DENSE = (DEPTH + 1) // 2
N_MOE = DEPTH // 2
EPS = 1e-6
NEG_INF = -1e30

kernel_name = 'hybrid_s5_mla_memory_moe_trunk'


def rms_norm(x, g):
    xf = x.astype(jnp.float32)
    y = xf * lax.rsqrt(jnp.mean(xf * xf, axis=-1, keepdims=True) + EPS)
    return (y * g.astype(jnp.float32)).astype(x.dtype)


def rope_tables(positions):
    inv_freq = ROPE_THETA ** (-jnp.arange(0, MLA_ROPE, 2, dtype=jnp.float32) / MLA_ROPE)
    ang = positions.astype(jnp.float32)[..., None] * inv_freq
    return jnp.cos(ang)[:, :, None, :], jnp.sin(ang)[:, :, None, :]


def rope_tail(x, cos, sin):
    x_n, x_r = jnp.split(x, [MLA_NOPE], axis=-1)
    xr = x_r.astype(jnp.float32)
    x1, x2 = jnp.split(xr, 2, axis=-1)
    rot = jnp.concatenate([x1 * cos - x2 * sin, x2 * cos + x1 * sin], axis=-1)
    return jnp.concatenate([x_n, rot.astype(x.dtype)], axis=-1)


def complex_linear_combine(e1, e2):
    a1r, a1i, b1r, b1i = e1
    a2r, a2i, b2r, b2i = e2
    return (a2r * a1r - a2i * a1i,
            a2r * a1i + a2i * a1r,
            a2r * b1r - a2i * b1i + b2r,
            a2r * b1i + a2i * b1r + b2i)


def s5_branch(u, a_re, a_im, log_dt, b_re, b_im, c_re, c_im, d, w_glu):
    bsz, L, _ = u.shape
    f32 = jnp.float32
    uf = u.astype(f32).reshape(bsz, L, S5_GROUPS, S5_GROUP)
    a_re = a_re.astype(f32)
    a_im = a_im.astype(f32)
    b_re = b_re.astype(f32)
    b_im = b_im.astype(f32)
    dt = jnp.exp(log_dt.astype(f32))[:, None]
    mag = jnp.exp(dt * a_re)
    lam_re = mag * jnp.cos(dt * a_im)
    lam_im = mag * jnp.sin(dt * a_im)
    den = a_re * a_re + a_im * a_im
    n_re = lam_re - 1.0
    n_im = lam_im
    f_re = ((n_re * a_re + n_im * a_im) / den)[..., None]
    f_im = ((n_im * a_re - n_re * a_im) / den)[..., None]
    bb_re = f_re * b_re - f_im * b_im
    bb_im = f_re * b_im + f_im * b_re
    bu_re = jnp.einsum('gph,blgh->blgp', bb_re, uf)
    bu_im = jnp.einsum('gph,blgh->blgp', bb_im, uf)
    lr = jnp.broadcast_to(lam_re, bu_re.shape)
    li = jnp.broadcast_to(lam_im, bu_re.shape)
    _, _, s_re, s_im = lax.associative_scan(complex_linear_combine, (lr, li, bu_re, bu_im), axis=1)
    y = (jnp.einsum('ghp,blgp->blgh', c_re.astype(f32), s_re)
         - jnp.einsum('ghp,blgp->blgh', c_im.astype(f32), s_im)
         + d.astype(f32) * uf)
    y = jax.nn.gelu(y.reshape(bsz, L, S5_WIDTH)).astype(u.dtype)
    ga, gb = jnp.split(y @ w_glu, 2, axis=-1)
    return ga * jax.nn.sigmoid(gb)


def causal_block_attention(q, k, v, positions):
    bsz, L, H, dq = q.shape
    nb = L // Q_BLOCK
    scale = dq ** -0.5
    qb = q.reshape(bsz, nb, Q_BLOCK, H, dq).transpose(1, 0, 2, 3, 4)
    pb = positions.reshape(bsz, nb, Q_BLOCK).transpose(1, 0, 2)

    def one_block(args):
        q_blk, p_blk = args
        s = jnp.einsum('bqhd,bkhd->bhqk', q_blk, k).astype(jnp.float32) * scale
        mask = positions[:, None, None, :] <= p_blk[:, None, :, None]
        s = jnp.where(mask, s, NEG_INF)
        p = jax.nn.softmax(s, axis=-1).astype(v.dtype)
        return jnp.einsum('bhqk,bkhd->bqhd', p, v)

    o = lax.map(one_block, (qb, pb))
    return o.transpose(1, 0, 2, 3, 4).reshape(bsz, L, H, v.shape[-1])


def mla_branch(c_q, c_kv, k_rope, positions, cos, sin, q_a_norm, w_q_b, kv_norm,
               w_kv_b, q_norm, k_norm, w_o):
    bsz, L, _ = c_q.shape
    q = (rms_norm(c_q, q_a_norm) @ w_q_b).reshape(bsz, L, MLA_HEADS, MLA_QK)
    kv = (rms_norm(c_kv, kv_norm) @ w_kv_b).reshape(bsz, L, MLA_HEADS, MLA_NOPE + MLA_V)
    k_nope, v = jnp.split(kv, [MLA_NOPE], axis=-1)
    k_r = jnp.broadcast_to(k_rope[:, :, None, :], (bsz, L, MLA_HEADS, MLA_ROPE))
    k = jnp.concatenate([k_nope, k_r], axis=-1)
    q = rope_tail(rms_norm(q, q_norm), cos, sin)
    k = rope_tail(rms_norm(k, k_norm), cos, sin)
    o = causal_block_attention(q, k, v, positions)
    return o.reshape(bsz, L, MLA_HEADS * MLA_V) @ w_o


def memory_branch(q_mem, mem, mem_norm_g, w_kv, q_norm, k_norm, w_o):
    bsz, L, _ = q_mem.shape
    m_len = mem.shape[1]
    m = rms_norm(mem, mem_norm_g)
    k, v = jnp.split(m @ w_kv, 2, axis=-1)
    k = rms_norm(k.reshape(bsz, m_len, MEM_HEADS, MEM_HEAD_DIM), k_norm)
    v = v.reshape(bsz, m_len, MEM_HEADS, MEM_HEAD_DIM)
    q = rms_norm(q_mem.reshape(bsz, L, MEM_HEADS, MEM_HEAD_DIM), q_norm)
    s = jnp.einsum('blhd,bmhd->bhlm', q, k).astype(jnp.float32) * (MEM_HEAD_DIM ** -0.5)
    p = jax.nn.softmax(s, axis=-1).astype(v.dtype)
    o = jnp.einsum('bhlm,bmhd->blhd', p, v).reshape(bsz, L, MEM_WIDTH)
    return o @ w_o


def swiglu(h, w_gate_up, w_down):
    g, u = jnp.split(h @ w_gate_up, 2, axis=-1)
    return (jax.nn.silu(g) * u) @ w_down


def moe_ffn(h, router, router_b, w_gate_up, w_down):
    bsz, L, D = h.shape
    t = h.reshape(bsz * L, D)
    logits = (t @ router).astype(jnp.float32) + router_b.astype(jnp.float32)
    top_v, top_i = lax.top_k(logits, TOP_K)
    top_w = jax.nn.softmax(top_v, axis=-1)
    out = jnp.zeros_like(t)
    for e in range(N_EXPERTS):
        w_e = jnp.sum(jnp.where(top_i == e, top_w, 0.0), axis=-1).astype(t.dtype)
        out = out + w_e[:, None] * swiglu(t, w_gate_up[e], w_down[e])
    return out.reshape(bsz, L, D)


def setup_inputs(seed: int = 0) -> dict:
    key = jax.random.key(seed)
    ks = iter(jax.random.split(key, 40))
    f32 = jnp.float32

    def nrm(shape, fan_in):
        return jax.random.normal(next(ks), shape, f32) * (fan_in ** -0.5)

    def gain(shape):
        return 1.0 + 0.02 * jax.random.normal(next(ks), shape, f32)

    x = jax.random.normal(next(ks), (BATCH, SEQ, D_MODEL), f32)
    mem = jax.random.normal(next(ks), (BATCH, N_MEM, D_MODEL), f32)
    offset = jax.random.randint(next(ks), (BATCH, 1), 0, 4096, jnp.int32)
    positions = offset + jnp.arange(SEQ, dtype=jnp.int32)[None, :]
    n_idx = jnp.arange(S5_STATE, dtype=f32)
    gp = (DEPTH, S5_GROUPS, S5_STATE)
    return {
        'x': x,
        'mem': mem,
        'positions': positions,
        'norm_mix': gain((DEPTH, D_MODEL)),
        'w_in': nrm((DEPTH, D_MODEL, IN_COLS), D_MODEL),
        's5_a_re': -0.5 + 0.01 * jax.random.normal(next(ks), gp, f32),
        's5_a_im': math.pi * n_idx + 0.01 * jax.random.normal(next(ks), gp, f32),
        's5_log_dt': jax.random.uniform(next(ks), (DEPTH, S5_GROUPS), f32,
                                        math.log(S5_DT_MIN), math.log(S5_DT_MAX)),
        's5_b_re': nrm((DEPTH, S5_GROUPS, S5_STATE, S5_GROUP), 2 * S5_GROUP),
        's5_b_im': nrm((DEPTH, S5_GROUPS, S5_STATE, S5_GROUP), 2 * S5_GROUP),
        's5_c_re': nrm((DEPTH, S5_GROUPS, S5_GROUP, S5_STATE), S5_STATE),
        's5_c_im': nrm((DEPTH, S5_GROUPS, S5_GROUP, S5_STATE), S5_STATE),
        's5_d': jax.random.normal(next(ks), (DEPTH, S5_GROUPS, S5_GROUP), f32),
        's5_w_glu': nrm((DEPTH, S5_WIDTH, 2 * D_MODEL), S5_WIDTH),
        'mla_q_a_norm': gain((DEPTH, MLA_Q_RANK)),
        'mla_w_q_b': nrm((DEPTH, MLA_Q_RANK, MLA_HEADS * MLA_QK), MLA_Q_RANK),
        'mla_kv_norm': gain((DEPTH, MLA_KV_RANK)),
        'mla_w_kv_b': nrm((DEPTH, MLA_KV_RANK, MLA_HEADS * (MLA_NOPE + MLA_V)), MLA_KV_RANK),
        'mla_q_norm': gain((DEPTH, MLA_QK)),
        'mla_k_norm': gain((DEPTH, MLA_QK)),
        'mla_w_o': nrm((DEPTH, MLA_HEADS * MLA_V, D_MODEL), MLA_HEADS * MLA_V),
        'mem_norm': gain((DEPTH, D_MODEL)),
        'mem_w_kv': nrm((DEPTH, D_MODEL, 2 * MEM_WIDTH), D_MODEL),
        'mem_q_norm': gain((DEPTH, MEM_HEAD_DIM)),
        'mem_k_norm': gain((DEPTH, MEM_HEAD_DIM)),
        'mem_w_o': nrm((DEPTH, MEM_WIDTH, D_MODEL), MEM_WIDTH),
        'w_out': nrm((DEPTH, D_MODEL, D_MODEL), D_MODEL),
        'norm_ffn': gain((DEPTH, D_MODEL)),
        'ffn_w_gate_up': nrm((N_DENSE, D_MODEL, 2 * D_FF), D_MODEL),
        'ffn_w_down': nrm((N_DENSE, D_FF, D_MODEL), D_FF),
        'moe_router': nrm((N_MOE, D_MODEL, N_EXPERTS), D_MODEL),
        'moe_router_b': 0.01 * jax.random.normal(next(ks), (N_MOE, N_EXPERTS), f32),
        'moe_w_gate_up': nrm((N_MOE, N_EXPERTS, D_MODEL, 2 * D_FF_EXPERT), D_MODEL),
        'moe_w_down': nrm((N_MOE, N_EXPERTS, D_FF_EXPERT, D_MODEL), D_FF_EXPERT),
    }


def reference(x, mem, positions, norm_mix, w_in, s5_a_re, s5_a_im, s5_log_dt, s5_b_re,
              s5_b_im, s5_c_re, s5_c_im, s5_d, s5_w_glu, mla_q_a_norm, mla_w_q_b,
              mla_kv_norm, mla_w_kv_b, mla_q_norm, mla_k_norm, mla_w_o, mem_norm,
              mem_w_kv, mem_q_norm, mem_k_norm, mem_w_o, w_out, norm_ffn,
              ffn_w_gate_up, ffn_w_down, moe_router, moe_router_b, moe_w_gate_up,
              moe_w_down):
    bsz, L, D = x.shape
    cos, sin = rope_tables(positions)
    for l in range(DEPTH):
        h = rms_norm(x, norm_mix[l])
        proj = h @ w_in[l]
        u, c_q, c_kv, k_rope, q_mem, gate_logits = jnp.split(
            proj, [S5_END, CQ_END, CKV_END, KR_END, MQ_END], axis=-1)
        y_s5 = s5_branch(u, s5_a_re[l], s5_a_im[l], s5_log_dt[l], s5_b_re[l], s5_b_im[l],
                         s5_c_re[l], s5_c_im[l], s5_d[l], s5_w_glu[l])
        y_mla = mla_branch(c_q, c_kv, k_rope, positions, cos, sin, mla_q_a_norm[l],
                           mla_w_q_b[l], mla_kv_norm[l], mla_w_kv_b[l], mla_q_norm[l],
                           mla_k_norm[l], mla_w_o[l])
        y_mem = memory_branch(q_mem, mem, mem_norm[l], mem_w_kv[l], mem_q_norm[l],
                              mem_k_norm[l], mem_w_o[l])
        g = jax.nn.sigmoid(gate_logits.astype(jnp.float32)).astype(x.dtype)
        g = g.reshape(bsz, L, N_BRANCH, D)
        merged = g[:, :, 0, :] * y_s5 + g[:, :, 1, :] * y_mla + g[:, :, 2, :] * y_mem
        x = x + merged @ w_out[l]
        h = rms_norm(x, norm_ffn[l])
        if l % 2 == 0:
            x = x + swiglu(h, ffn_w_gate_up[l // 2], ffn_w_down[l // 2])
        else:
            x = x + moe_ffn(h, moe_router[l // 2], moe_router_b[l // 2],
                            moe_w_gate_up[l // 2], moe_w_down[l // 2])
    return x
```

```python
import functools
import math

import jax
import jax.numpy as jnp
from jax import lax
from jax.experimental import pallas as pl
from jax.experimental.pallas import tpu as pltpu

F32 = jnp.float32
BF16 = jnp.bfloat16
HIGHEST = lax.Precision.HIGHEST

D_MODEL = 2048
N_MEM = 256
S5_WIDTH = 1024
S5_GROUP = 16
S5_GROUPS = S5_WIDTH // S5_GROUP
S5_STATE = 64
S5_CHUNK = 64
S5_CK = S5_CHUNK * S5_GROUP
MLA_HEADS = 16
MLA_Q_RANK = 448
MLA_KV_RANK = 512
MLA_NOPE = 128
MLA_ROPE = 64
MLA_QK = MLA_NOPE + MLA_ROPE
MLA_V = 128
MLA_QK_PAD = 256
ROPE_THETA = 10000.0
MEM_HEADS = 4
MEM_HEAD_DIM = 256
MEM_WIDTH = MEM_HEADS * MEM_HEAD_DIM
N_BRANCH = 3
D_FF = 7168
N_EXPERTS = 8
TOP_K = 2
EPS = 1e-6
NEG_INF = -1e30

PJ_U = 0
PJ_CQKR = 1024
PJ_CKV = 1536
PJ_MQ = 2048
PJ_GATE = 3072
PJ_COLS = PJ_GATE + N_BRANCH * D_MODEL

VMEM_LIMIT = 52 * 1024 * 1024


def _params(sem):
    return pltpu.CompilerParams(dimension_semantics=sem, vmem_limit_bytes=VMEM_LIMIT)


def _rms(x, g):
    r = lax.rsqrt(jnp.mean(x * x, axis=-1, keepdims=True) + EPS)
    return x * r * g


def _norm_mm_kernel(x_ref, g_ref, w_ref, o_ref, h_ref):
    @pl.when(pl.program_id(1) == 0)
    def _():
        h_ref[...] = _rms(x_ref[...], g_ref[...]).astype(BF16)

    o_ref[...] = jnp.dot(h_ref[...], w_ref[...], preferred_element_type=F32).astype(o_ref.dtype)


def norm_matmul(x, g, w, *, tm, tn, out_dtype=BF16):
    m, k = x.shape
    n = w.shape[1]
    return pl.pallas_call(
        _norm_mm_kernel,
        grid=(m // tm, n // tn),
        in_specs=[pl.BlockSpec((tm, k), lambda i, j: (i, 0)),
                  pl.BlockSpec((1, k), lambda i, j: (0, 0)),
                  pl.BlockSpec((k, tn), lambda i, j: (0, j))],
        out_specs=pl.BlockSpec((tm, tn), lambda i, j: (i, j)),
        out_shape=jax.ShapeDtypeStruct((m, n), out_dtype),
        scratch_shapes=[pltpu.VMEM((tm, k), BF16)],
        compiler_params=_params(("parallel", "arbitrary")),
    )(x, g, w)


def _glu_gate_kernel(a_ref, wa_ref, wb_ref, gate_ref, o_ref):
    a = a_ref[...]
    ya = jnp.dot(a, wa_ref[...], preferred_element_type=F32)
    yb = jnp.dot(a, wb_ref[...], preferred_element_type=F32)
    g = jax.nn.sigmoid(gate_ref[...].astype(F32))
    o_ref[...] = (g * (ya * jax.nn.sigmoid(yb))).astype(o_ref.dtype)


def glu_gate(a, w, proj, gate_col, *, tm, tn):
    m, k = a.shape
    n = w.shape[1] // 2
    nb = n // tn
    gb = gate_col // tn
    return pl.pallas_call(
        _glu_gate_kernel,
        grid=(m // tm, nb),
        in_specs=[pl.BlockSpec((tm, k), lambda i, j: (i, 0)),
                  pl.BlockSpec((k, tn), lambda i, j: (0, j)),
                  pl.BlockSpec((k, tn), lambda i, j: (0, j + nb)),
                  pl.BlockSpec((tm, tn), lambda i, j: (i, j + gb))],
        out_specs=pl.BlockSpec((tm, tn), lambda i, j: (i, j)),
        out_shape=jax.ShapeDtypeStruct((m, n), BF16),
        compiler_params=_params(("parallel", "parallel")),
    )(a, w, w, proj)


def _gate_add_kernel(a_ref, w_ref, gate_ref, prev_ref, o_ref):
    y = jnp.dot(a_ref[...], w_ref[...], preferred_element_type=F32)
    g = jax.nn.sigmoid(gate_ref[...].astype(F32))
    o_ref[...] = (prev_ref[...].astype(F32) + g * y).astype(o_ref.dtype)


def gate_add(a, w, proj, gate_col, prev, *, tm, tn):
    m, k = a.shape
    n = w.shape[1]
    gb = gate_col // tn
    return pl.pallas_call(
        _gate_add_kernel,
        grid=(m // tm, n // tn),
        in_specs=[pl.BlockSpec((tm, k), lambda i, j: (i, 0)),
                  pl.BlockSpec((k, tn), lambda i, j: (0, j)),
                  pl.BlockSpec((tm, tn), lambda i, j: (i, j + gb)),
                  pl.BlockSpec((tm, tn), lambda i, j: (i, j))],
        out_specs=pl.BlockSpec((tm, tn), lambda i, j: (i, j)),
        out_shape=jax.ShapeDtypeStruct((m, n), BF16),
        compiler_params=_params(("parallel", "parallel")),
    )(a, w, proj, prev)


def _resid_mm_kernel(a_ref, w_ref, x_ref, o_ref):
    o_ref[...] = x_ref[...] + jnp.dot(a_ref[...], w_ref[...], preferred_element_type=F32)


def resid_matmul(a, w, x, *, tm, tn):
    m, k = a.shape
    n = w.shape[1]
    return pl.pallas_call(
        _resid_mm_kernel,
        grid=(m // tm, n // tn),
        in_specs=[pl.BlockSpec((tm, k), lambda i, j: (i, 0)),
                  pl.BlockSpec((k, tn), lambda i, j: (0, j)),
                  pl.BlockSpec((tm, tn), lambda i, j: (i, j))],
        out_specs=pl.BlockSpec((tm, tn), lambda i, j: (i, j)),
        out_shape=jax.ShapeDtypeStruct((m, n), F32),
        compiler_params=_params(("parallel", "parallel")),
    )(a, w, x)


def _s5_kernel(u_ref, rowp_ref, colp_ref, bt_ref, ct_ref, y_ref, toep_ref):
    t_chunk = S5_CHUNK
    n_chunks = u_ref.shape[1]
    p2 = 2 * S5_STATE

    rowp = rowp_ref[0]
    are_r, aim_r = rowp[0:1], rowp[1:2]
    dt_r = jnp.exp(rowp[2:3])
    colp = colp_ref[0]
    are_c, aim_c = colp[:, 0:1], colp[:, 1:2]
    dt_c = jnp.exp(colp[:, 2:3])
    d_c = colp[0:S5_GROUP, 3:4]

    row_q = lax.broadcasted_iota(jnp.int32, (p2, S5_CK), 0)
    lane_q = lax.broadcasted_iota(jnp.int32, (1, p2), 1)

    tau = lax.broadcasted_iota(jnp.int32, (p2, p2), 1).astype(F32)
    row_pp = lax.broadcasted_iota(jnp.int32, (p2, p2), 0)
    mag = jnp.exp(tau * dt_c * are_c)
    ang = tau * dt_c * aim_c
    pw = mag * jnp.where(row_pp < S5_STATE, jnp.cos(ang), jnp.sin(ang))

    l_t = lax.broadcasted_iota(jnp.int32, (p2, S5_CK), 1) // S5_GROUP
    e_tau = (l_t == row_q).astype(F32)
    e_tau1 = (l_t + 1 == row_q).astype(F32)
    l_i = lax.broadcasted_iota(jnp.int32, (S5_GROUP, S5_CK), 1) % S5_GROUP
    e_ch = (l_i == lax.broadcasted_iota(jnp.int32, (S5_GROUP, S5_CK), 0)).astype(F32)

    def swap_halves(x):
        return jnp.concatenate([x[S5_STATE:], x[:S5_STATE]], axis=0)

    x1 = jnp.dot(pw, e_tau, precision=HIGHEST, preferred_element_type=F32)
    x1s = jnp.dot(pw, e_tau1, precision=HIGHEST, preferred_element_type=F32)
    ct = ct_ref[0]
    cta = jnp.dot(ct[0], e_ch, precision=HIGHEST, preferred_element_type=F32)
    ctb = jnp.dot(ct[1], e_ch, precision=HIGHEST, preferred_element_type=F32)
    ctb = jnp.where(row_q < S5_STATE, -ctb, ctb)
    z = cta * x1 + ctb * swap_halves(x1)
    zs = cta * x1s + ctb * swap_halves(x1s)
    wt = jnp.where(row_q < S5_STATE, zs, -zs)

    lam_mag = jnp.exp(dt_r * are_r)
    lam_re = lam_mag * jnp.cos(dt_r * aim_r)
    lam_im = lam_mag * jnp.sin(dt_r * aim_r)
    den = are_r * are_r + aim_r * aim_r
    n_re = lam_re - 1.0
    f_re = (n_re * are_r + lam_im * aim_r) / den
    f_im = (lam_im * are_r - n_re * aim_r) / den
    bt = bt_ref[0]
    bb_re = f_re * bt[0] - f_im * bt[1]
    bb_im = f_re * bt[1] + f_im * bt[0]
    first = lane_q < S5_STATE

    kt = jnp.dot(jnp.where(first, bb_re, -bb_im), z, precision=HIGHEST, preferred_element_type=F32)
    lane_k = lax.broadcasted_iota(jnp.int32, (S5_GROUP, S5_CK), 1)
    row_k = lax.broadcasted_iota(jnp.int32, (S5_GROUP, S5_CK), 0)
    kt = kt + jnp.where(lane_k == row_k, d_c, 0.0)
    for s in range(t_chunk):
        off = S5_GROUP * s
        blk = kt if s == 0 else jnp.where(lane_k >= off, pltpu.roll(kt, off, 1), 0.0)
        toep_ref[off:off + S5_GROUP, :] = blk.astype(BF16)

    e_s = (t_chunk - 1 - lax.broadcasted_iota(jnp.int32, (t_chunk, p2), 0)).astype(F32)
    pt_mag = jnp.exp(e_s * dt_r * are_r)
    pt_ang = e_s * dt_r * aim_r
    pt_re = pt_mag * jnp.cos(pt_ang)
    pt_im = pt_mag * jnp.sin(pt_ang)
    r_s = lax.broadcasted_iota(jnp.int32, (S5_CK, t_chunk), 0) // S5_GROUP
    e_rows = (r_s == lax.broadcasted_iota(jnp.int32, (S5_CK, t_chunk), 1)).astype(F32)
    r_j = lax.broadcasted_iota(jnp.int32, (S5_CK, S5_GROUP), 0) % S5_GROUP
    e_rowj = (r_j == lax.broadcasted_iota(jnp.int32, (S5_CK, S5_GROUP), 1)).astype(F32)
    hi = functools.partial(jnp.dot, precision=HIGHEST, preferred_element_type=F32)
    gt = (hi(e_rows, pt_re) * hi(e_rowj, jnp.where(first, bb_re, bb_im))
          + hi(e_rows, pt_im) * hi(e_rowj, jnp.where(first, -bb_im, bb_re)))

    u = u_ref[0]
    h = jnp.dot(u, gt.astype(BF16), preferred_element_type=F32)

    k_row = lax.broadcasted_iota(jnp.int32, (8, p2), 0)
    m_pow = (t_chunk * (1 << k_row)).astype(F32)
    m_mag = jnp.exp(m_pow * dt_r * are_r)
    m_ang = m_pow * dt_r * aim_r
    m_re = m_mag * jnp.cos(m_ang)
    m_im = m_mag * jnp.sin(m_ang)
    row_c = lax.broadcasted_iota(jnp.int32, (n_chunks, p2), 0)
    k = 0
    while (1 << k) < n_chunks:
        sh = 1 << k
        s_prev = jnp.where(row_c >= sh, pltpu.roll(h, sh, 0), 0.0)
        s_swap = pltpu.roll(s_prev, S5_STATE, 1)
        mb = jnp.where(first, -m_im[k:k + 1], m_im[k:k + 1])
        h = h + m_re[k:k + 1] * s_prev + mb * s_swap
        k += 1
    h_prev = jnp.where(row_c >= 1, pltpu.roll(h, 1, 0), 0.0)

    y = (jnp.dot(u, toep_ref[...], preferred_element_type=F32)
         + jnp.dot(h_prev.astype(BF16), wt.astype(BF16), preferred_element_type=F32))
    y_ref[0] = jax.nn.gelu(y).astype(y_ref.dtype)


def s5_mix(u, a_re, a_im, log_dt, b_re, b_im, c_re, c_im, d):
    seq = u.shape[0]
    n_chunks = seq // S5_CHUNK
    g = S5_GROUPS
    ug = u.reshape(n_chunks, S5_CHUNK, g, S5_GROUP).transpose(2, 0, 1, 3).reshape(g, n_chunks, S5_CK)

    dup = lambda v: jnp.concatenate([v, v], axis=-1)
    zeros = jnp.zeros((g, 2 * S5_STATE), F32)
    ldt = jnp.broadcast_to(log_dt[:, None], (g, 2 * S5_STATE))
    rowp = jnp.stack([dup(a_re), dup(a_im), ldt] + [zeros] * 5, axis=1)
    d_pad = jnp.pad(d, ((0, 0), (0, 2 * S5_STATE - S5_GROUP)))
    colp = jnp.stack([dup(a_re), dup(a_im), ldt, d_pad] + [zeros] * 4, axis=2)
    bt = jnp.stack([dup(b_re.transpose(0, 2, 1)), dup(b_im.transpose(0, 2, 1))], axis=1)
    ctr, cti = c_re.transpose(0, 2, 1), c_im.transpose(0, 2, 1)
    ct = jnp.stack([jnp.concatenate([ctr, ctr], 1), jnp.concatenate([cti, cti], 1)], axis=1)

    yg = pl.pallas_call(
        _s5_kernel,
        grid=(g,),
        in_specs=[pl.BlockSpec((1, n_chunks, S5_CK), lambda i: (i, 0, 0)),
                  pl.BlockSpec((1, 8, 2 * S5_STATE), lambda i: (i, 0, 0)),
                  pl.BlockSpec((1, 2 * S5_STATE, 8), lambda i: (i, 0, 0)),
                  pl.BlockSpec((1, 2, S5_GROUP, 2 * S5_STATE), lambda i: (i, 0, 0, 0)),
                  pl.BlockSpec((1, 2, 2 * S5_STATE, S5_GROUP), lambda i: (i, 0, 0, 0))],
        out_specs=pl.BlockSpec((1, n_chunks, S5_CK), lambda i: (i, 0, 0)),
        out_shape=jax.ShapeDtypeStruct((g, n_chunks, S5_CK), BF16),
        scratch_shapes=[pltpu.VMEM((S5_CK, S5_CK), BF16)],
        compiler_params=_params(("parallel",)),
    )(ug, rowp, colp, bt, ct)
    return yg.reshape(g, n_chunks, S5_CHUNK, S5_GROUP).transpose(1, 2, 0, 3).reshape(seq, S5_WIDTH)


def _mla_prep_kernel(cqkr_ref, ckv_ref, pos_ref, ga_ref, gkv_ref, gq_ref, gkn_ref, gkr_ref,
                     invf_ref, wq_ref, wkv_ref, q_ref, k_ref, v_ref):
    cq = cqkr_ref[...].astype(F32)
    lane = lax.broadcasted_iota(jnp.int32, cq.shape, 1)
    ssq = jnp.sum(jnp.where(lane < MLA_Q_RANK, cq * cq, 0.0), axis=-1, keepdims=True)
    hq = (cq * lax.rsqrt(ssq * (1.0 / MLA_Q_RANK) + EPS) * ga_ref[...]).astype(BF16)
    qf = jnp.dot(hq, wq_ref[...], preferred_element_type=F32)
    hkv = _rms(ckv_ref[...].astype(F32), gkv_ref[...]).astype(BF16)
    kvf = jnp.dot(hkv, wkv_ref[...], preferred_element_type=F32)

    l128 = lax.broadcasted_iota(jnp.int32, (cq.shape[0], 128), 1)
    half = MLA_ROPE // 2
    ang = pos_ref[...].astype(F32) * invf_ref[...]
    cosv = jnp.cos(ang)
    sinv = jnp.sin(ang)
    sgn_sin = jnp.where(l128 < half, -sinv, jnp.where(l128 < MLA_ROPE, sinv, 0.0))

    def rope(t):
        sw = jnp.where(l128 < half, pltpu.roll(t, 128 - half, 1), pltpu.roll(t, half, 1))
        return t * cosv + sw * sgn_sin

    kr = jnp.where(l128 < MLA_ROPE, pltpu.roll(cq[:, 3 * 128:], MLA_ROPE, 1), 0.0)
    kr_ssq = jnp.sum(kr * kr, axis=-1, keepdims=True)
    kr_rot = rope(kr * gkr_ref[...])

    gq = gq_ref[...]
    gkn = gkn_ref[...]
    scale = MLA_QK ** -0.5
    inv_qk = 1.0 / MLA_QK
    for h in range(MLA_HEADS):
        qh = qf[:, MLA_QK_PAD * h:MLA_QK_PAD * (h + 1)]
        rq = lax.rsqrt(jnp.sum(qh * qh, axis=-1, keepdims=True) * inv_qk + EPS)
        qn = qh * rq * gq
        q_ref[h] = (jnp.concatenate([qn[:, :MLA_NOPE], rope(qn[:, MLA_NOPE:])], axis=1) * scale).astype(BF16)
        kn = kvf[:, MLA_QK_PAD * h:MLA_QK_PAD * h + MLA_NOPE]
        rk = lax.rsqrt((jnp.sum(kn * kn, axis=-1, keepdims=True) + kr_ssq) * inv_qk + EPS)
        k_ref[h] = jnp.concatenate([kn * rk * gkn, kr_rot * rk], axis=1).astype(BF16)
        v_ref[h] = kvf[:, MLA_QK_PAD * h + MLA_NOPE:MLA_QK_PAD * (h + 1)].astype(BF16)


def mla_prep(proj, pos_col, ga, gkv, gq, gkn, gkr, invf, wq, wkv, *, tm):
    seq = proj.shape[0]
    h = MLA_HEADS
    full = lambda shape: pl.BlockSpec(shape, lambda i: (0,) * len(shape))
    return pl.pallas_call(
        _mla_prep_kernel,
        grid=(seq // tm,),
        in_specs=[pl.BlockSpec((tm, 512), lambda i: (i, PJ_CQKR // 512)),
                  pl.BlockSpec((tm, 512), lambda i: (i, PJ_CKV // 512)),
                  pl.BlockSpec((tm, 1), lambda i: (i, 0)),
                  full((1, 512)), full((1, 512)), full((1, MLA_QK_PAD)), full((1, 128)), full((1, 128)),
                  full((1, 128)), full((512, h * MLA_QK_PAD)), full((512, h * MLA_QK_PAD))],
        out_specs=[pl.BlockSpec((h, tm, MLA_QK_PAD), lambda i: (0, i, 0)),
                   pl.BlockSpec((h, tm, MLA_QK_PAD), lambda i: (0, i, 0)),
                   pl.BlockSpec((h, tm, MLA_V), lambda i: (0, i, 0))],
        out_shape=[jax.ShapeDtypeStruct((h, seq, MLA_QK_PAD), BF16),
                   jax.ShapeDtypeStruct((h, seq, MLA_QK_PAD), BF16),
                   jax.ShapeDtypeStruct((h, seq, MLA_V), BF16)],
        compiler_params=_params(("parallel",)),
    )(proj, proj, pos_col, ga, gkv, gq, gkn, gkr, invf, wq, wkv)


def _flash_kernel(q_ref, k_ref, v_ref, pq_ref, pk_ref, o_ref, *, tq, tk):
    qi = pl.program_id(1)
    q = q_ref[0]
    pq = pq_ref[...]

    def step(kb, carry, masked):
        m_prev, l_prev, acc = carry
        start = pl.multiple_of(kb * tk, tk)
        kblk = k_ref[0, pl.ds(start, tk), :]
        vblk = v_ref[0, pl.ds(start, tk), :]
        s = lax.dot_general(q, kblk, (((1,), (1,)), ((), ())), preferred_element_type=F32)
        if masked:
            s = jnp.where(pk_ref[:, pl.ds(start, tk)] <= pq, s, NEG_INF)
        m_new = jnp.maximum(m_prev, jnp.max(s, axis=-1, keepdims=True))
        alpha = jnp.exp(m_prev - m_new)
        p = jnp.exp(s - m_new)
        l_new = alpha * l_prev + jnp.sum(p, axis=-1, keepdims=True)
        acc = alpha * acc + jnp.dot(p.astype(BF16), vblk, preferred_element_type=F32)
        return m_new, l_new, acc

    init = (jnp.full((tq, 1), NEG_INF, F32), jnp.zeros((tq, 1), F32), jnp.zeros((tq, MLA_V), F32))
    ratio = tq // tk
    carry = lax.fori_loop(0, qi * ratio, functools.partial(step, masked=False), init)
    carry = lax.fori_loop(qi * ratio, (qi + 1) * ratio, functools.partial(step, masked=True), carry)
    _, l_fin, acc = carry
    o_ref[...] = (acc / l_fin).astype(o_ref.dtype)


def flash_attention(q, k, v, pos_col, pos_row, *, tq, tk):
    h, seq, _ = q.shape
    return pl.pallas_call(
        functools.partial(_flash_kernel, tq=tq, tk=tk),
        grid=(h, seq // tq),
        in_specs=[pl.BlockSpec((1, tq, MLA_QK_PAD), lambda hh, i: (hh, i, 0)),
                  pl.BlockSpec((1, seq, MLA_QK_PAD), lambda hh, i: (hh, 0, 0)),
                  pl.BlockSpec((1, seq, MLA_V), lambda hh, i: (hh, 0, 0)),
                  pl.BlockSpec((tq, 1), lambda hh, i: (i, 0)),
                  pl.BlockSpec((1, seq), lambda hh, i: (0, 0))],
        out_specs=pl.BlockSpec((tq, MLA_V), lambda hh, i: (i, hh)),
        out_shape=jax.ShapeDtypeStruct((seq, h * MLA_V), BF16),
        compiler_params=_params(("parallel", "arbitrary")),
    )(q, k, v, pos_col, pos_row)


def _mem_kv_kernel(mem_ref, g_ref, w_ref, gk_ref, k_ref, v_ref):
    m = _rms(mem_ref[...], g_ref[...]).astype(BF16)
    kv = jnp.dot(m, w_ref[...], preferred_element_type=F32)
    for h in range(MEM_HEADS):
        kh = kv[:, MEM_HEAD_DIM * h:MEM_HEAD_DIM * (h + 1)]
        k_ref[:, MEM_HEAD_DIM * h:MEM_HEAD_DIM * (h + 1)] = _rms(kh, gk_ref[...]).astype(BF16)
    v_ref[...] = kv[:, MEM_WIDTH:].astype(BF16)


def mem_kv(mem, g, w, gk):
    n_mem = mem.shape[0]
    full = lambda shape: pl.BlockSpec(shape, lambda i: (0,) * len(shape))
    return pl.pallas_call(
        _mem_kv_kernel,
        grid=(1,),
        in_specs=[full(mem.shape), full(g.shape), full(w.shape), full(gk.shape)],
        out_specs=[full((n_mem, MEM_WIDTH)), full((n_mem, MEM_WIDTH))],
        out_shape=[jax.ShapeDtypeStruct((n_mem, MEM_WIDTH), BF16)] * 2,
        compiler_params=_params(("arbitrary",)),
    )(mem, g, w, gk)


def _mem_attn_kernel(q_ref, k_ref, v_ref, gq_ref, o_ref):
    scale = MEM_HEAD_DIM ** -0.5
    for h in range(MEM_HEADS):
        sl = slice(MEM_HEAD_DIM * h, MEM_HEAD_DIM * (h + 1))
        qh = (_rms(q_ref[:, sl].astype(F32), gq_ref[...]) * scale).astype(BF16)
        s = lax.dot_general(qh, k_ref[:, sl], (((1,), (1,)), ((), ())), preferred_element_type=F32)
        p = jnp.exp(s - jnp.max(s, axis=-1, keepdims=True))
        o = jnp.dot(p.astype(BF16), v_ref[:, sl], preferred_element_type=F32)
        o_ref[:, sl] = (o / jnp.sum(p, axis=-1, keepdims=True)).astype(o_ref.dtype)


def mem_attention(proj, k, v, gq, *, tm):
    seq = proj.shape[0]
    n_mem = k.shape[0]
    return pl.pallas_call(
        _mem_attn_kernel,
        grid=(seq // tm,),
        in_specs=[pl.BlockSpec((tm, MEM_WIDTH), lambda i: (i, PJ_MQ // MEM_WIDTH)),
                  pl.BlockSpec((n_mem, MEM_WIDTH), lambda i: (0, 0)),
                  pl.BlockSpec((n_mem, MEM_WIDTH), lambda i: (0, 0)),
                  pl.BlockSpec((1, MEM_HEAD_DIM), lambda i: (0, 0))],
        out_specs=pl.BlockSpec((tm, MEM_WIDTH), lambda i: (i, 0)),
        out_shape=jax.ShapeDtypeStruct((seq, MEM_WIDTH), BF16),
        compiler_params=_params(("parallel",)),
    )(proj, k, v, gq)


def _ffn_kernel(x_ref, g_ref, wg_ref, wu_ref, wd_ref, o_ref, h_ref, acc_ref):
    f = pl.program_id(1)

    @pl.when(f == 0)
    def _():
        h_ref[...] = _rms(x_ref[...], g_ref[...]).astype(BF16)
        acc_ref[...] = jnp.zeros_like(acc_ref)

    h = h_ref[...]
    gate = jnp.dot(h, wg_ref[...], preferred_element_type=F32)
    up = jnp.dot(h, wu_ref[...], preferred_element_type=F32)
    a = (jax.nn.silu(gate) * up).astype(BF16)
    acc_ref[...] += jnp.dot(a, wd_ref[...], preferred_element_type=F32)

    @pl.when(f == pl.num_programs(1) - 1)
    def _():
        o_ref[...] = x_ref[...] + acc_ref[...]


def dense_ffn(x, g, w_gate_up, w_down, *, tm, tf):
    m, d = x.shape
    nf = D_FF // tf
    return pl.pallas_call(
        _ffn_kernel,
        grid=(m // tm, nf),
        in_specs=[pl.BlockSpec((tm, d), lambda i, f: (i, 0)),
                  pl.BlockSpec((1, d), lambda i, f: (0, 0)),
                  pl.BlockSpec((d, tf), lambda i, f: (0, f)),
                  pl.BlockSpec((d, tf), lambda i, f: (0, f + nf)),
                  pl.BlockSpec((tf, d), lambda i, f: (f, 0))],
        out_specs=pl.BlockSpec((tm, d), lambda i, f: (i, 0)),
        out_shape=jax.ShapeDtypeStruct((m, d), F32),
        scratch_shapes=[pltpu.VMEM((tm, d), BF16), pltpu.VMEM((tm, d), F32)],
        compiler_params=_params(("parallel", "arbitrary")),
    )(x, g, w_gate_up, w_gate_up, w_down)


def _router_kernel(x_ref, g_ref, w_ref, b_ref, idx_ref, wgt_ref):
    h = _rms(x_ref[...], g_ref[...])
    logits = jnp.dot(h, w_ref[...], precision=HIGHEST, preferred_element_type=F32) + b_ref[...]
    lane = lax.broadcasted_iota(jnp.int32, logits.shape, 1)
    logits = jnp.where(lane < N_EXPERTS, logits, -jnp.inf)
    v1 = jnp.max(logits, axis=-1, keepdims=True)
    i1 = jnp.min(jnp.where(logits == v1, lane, 128), axis=-1, keepdims=True)
    rest = jnp.where(lane == i1, -jnp.inf, logits)
    v2 = jnp.max(rest, axis=-1, keepdims=True)
    i2 = jnp.min(jnp.where(rest == v2, lane, 128), axis=-1, keepdims=True)
    e2 = jnp.exp(v2 - v1)
    w1 = 1.0 / (1.0 + e2)
    w2 = e2 / (1.0 + e2)
    idx_ref[...] = jnp.where(lane == 0, i1, jnp.where(lane == 1, i2, 0))
    wgt_ref[...] = jnp.where(lane == 0, w1, jnp.where(lane == 1, w2, 0.0))


def moe_router(x, g, w_pad, b_pad, *, tm):
    m, d = x.shape
    return pl.pallas_call(
        _router_kernel,
        grid=(m // tm,),
        in_specs=[pl.BlockSpec((tm, d), lambda i: (i, 0)),
                  pl.BlockSpec((1, d), lambda i: (0, 0)),
                  pl.BlockSpec((d, 128), lambda i: (0, 0)),
                  pl.BlockSpec((1, 128), lambda i: (0, 0))],
        out_specs=[pl.BlockSpec((tm, 128), lambda i: (i, 0)),
                   pl.BlockSpec((tm, 128), lambda i: (i, 0))],
        out_shape=[jax.ShapeDtypeStruct((m, 128), jnp.int32),
                   jax.ShapeDtypeStruct((m, 128), F32)],
        compiler_params=_params(("parallel",)),
    )(x, g, w_pad, b_pad)


def _gather_rows_kernel(src_ref, x_hbm, o_ref, sem, *, tg):
    base = pl.program_id(0) * tg

    def issue(j, c):
        r = src_ref[base + j]
        pltpu.make_async_copy(x_hbm.at[pl.ds(r, 1), :], o_ref.at[pl.ds(j, 1), :], sem).start()
        return c

    lax.fori_loop(0, tg, issue, 0)

    def drain(j, c):
        pltpu.make_async_copy(x_hbm.at[pl.ds(0, 1), :], o_ref.at[pl.ds(j, 1), :], sem).wait()
        return c

    lax.fori_loop(0, tg, drain, 0)


def gather_rows(x, src, *, tg):
    n = src.shape[0]
    d = x.shape[1]
    return pl.pallas_call(
        functools.partial(_gather_rows_kernel, tg=tg),
        grid_spec=pltpu.PrefetchScalarGridSpec(
            num_scalar_prefetch=1,
            grid=(n // tg,),
            in_specs=[pl.BlockSpec(memory_space=pl.ANY)],
            out_specs=pl.BlockSpec((tg, d), lambda i, src: (i, 0)),
            scratch_shapes=[pltpu.SemaphoreType.DMA(())]),
        out_shape=jax.ShapeDtypeStruct((n, d), x.dtype),
        compiler_params=_params(("arbitrary",)),
    )(src, x)


def _moe_ffn_kernel(te_ref, tv_ref, x_ref, g_ref, ws_ref, wg_ref, wu_ref, wd_ref, o_ref, h_ref, acc_ref):
    i = pl.program_id(0)
    f = pl.program_id(1)
    valid = tv_ref[i] > 0

    @pl.when(jnp.logical_and(valid, f == 0))
    def _():
        h_ref[...] = _rms(x_ref[...], g_ref[...]).astype(BF16)
        acc_ref[...] = jnp.zeros_like(acc_ref)

    @pl.when(valid)
    def _():
        h = h_ref[...]
        gate = jnp.dot(h, wg_ref[0], preferred_element_type=F32)
        up = jnp.dot(h, wu_ref[0], preferred_element_type=F32)
        a = (jax.nn.silu(gate) * up).astype(BF16)
        acc_ref[...] += jnp.dot(a, wd_ref[0], preferred_element_type=F32)

    is_last = f == pl.num_programs(1) - 1

    @pl.when(jnp.logical_and(valid, is_last))
    def _():
        o_ref[...] = ws_ref[...] * acc_ref[...]

    @pl.when(jnp.logical_and(jnp.logical_not(valid), is_last))
    def _():
        o_ref[...] = jnp.zeros_like(o_ref)


def moe_ffn(xs, g, ws, tile_expert, tile_valid, w_gate_up, w_down, *, tm, tf):
    p, d = xs.shape
    nf = D_FF // tf
    last = nf - 1

    def fsel(i, f, tv):
        return jnp.where(tv[i] > 0, f, last)

    return pl.pallas_call(
        _moe_ffn_kernel,
        grid_spec=pltpu.PrefetchScalarGridSpec(
            num_scalar_prefetch=2,
            grid=(p // tm, nf),
            in_specs=[pl.BlockSpec((tm, d), lambda i, f, te, tv: (i, 0)),
                      pl.BlockSpec((1, d), lambda i, f, te, tv: (0, 0)),
                      pl.BlockSpec((tm, 1), lambda i, f, te, tv: (i, 0)),
                      pl.BlockSpec((1, d, tf), lambda i, f, te, tv: (te[i], 0, fsel(i, f, tv))),
                      pl.BlockSpec((1, d, tf), lambda i, f, te, tv: (te[i], 0, fsel(i, f, tv) + nf)),
                      pl.BlockSpec((1, tf, d), lambda i, f, te, tv: (te[i], fsel(i, f, tv), 0))],
            out_specs=pl.BlockSpec((tm, d), lambda i, f, te, tv: (i, 0)),
            scratch_shapes=[pltpu.VMEM((tm, d), BF16), pltpu.VMEM((tm, d), F32)]),
        out_shape=jax.ShapeDtypeStruct((p, d), F32),
        compiler_params=_params(("arbitrary", "arbitrary")),
    )(tile_expert, tile_valid, xs, g, ws, w_gate_up, w_gate_up, w_down)


def _combine_kernel(p0_ref, p1_ref, x_ref, ys_hbm, o_ref, buf, sem, *, tt):
    base = pl.program_id(0) * tt

    def issue(j, c):
        r0 = p0_ref[base + j]
        r1 = p1_ref[base + j]
        pltpu.make_async_copy(ys_hbm.at[pl.ds(r0, 1), :], buf.at[0, pl.ds(j, 1), :], sem.at[0]).start()
        pltpu.make_async_copy(ys_hbm.at[pl.ds(r1, 1), :], buf.at[1, pl.ds(j, 1), :], sem.at[1]).start()
        return c

    lax.fori_loop(0, tt, issue, 0)

    def drain(j, c):
        pltpu.make_async_copy(ys_hbm.at[pl.ds(0, 1), :], buf.at[0, pl.ds(j, 1), :], sem.at[0]).wait()
        pltpu.make_async_copy(ys_hbm.at[pl.ds(0, 1), :], buf.at[1, pl.ds(j, 1), :], sem.at[1]).wait()
        return c

    lax.fori_loop(0, tt, drain, 0)
    o_ref[...] = x_ref[...] + (buf[0] + buf[1])


def moe_combine(x, ys, pos0, pos1, *, tt):
    m, d = x.shape
    return pl.pallas_call(
        functools.partial(_combine_kernel, tt=tt),
        grid_spec=pltpu.PrefetchScalarGridSpec(
            num_scalar_prefetch=2,
            grid=(m // tt,),
            in_specs=[pl.BlockSpec((tt, d), lambda i, p0, p1: (i, 0)),
                      pl.BlockSpec(memory_space=pl.ANY)],
            out_specs=pl.BlockSpec((tt, d), lambda i, p0, p1: (i, 0)),
            scratch_shapes=[pltpu.VMEM((2, tt, d), F32), pltpu.SemaphoreType.DMA((2,))]),
        out_shape=jax.ShapeDtypeStruct((m, d), F32),
        compiler_params=_params(("arbitrary",)),
    )(pos0, pos1, x, ys)


def moe_layer(x, g, router, router_b, w_gate_up, w_down, *, tm, tf):
    seq, d = x.shape
    w_pad = jnp.pad(router, ((0, 0), (0, 128 - N_EXPERTS)))
    b_pad = jnp.pad(router_b, (0, 128 - N_EXPERTS))[None, :]
    idx, wgt = moe_router(x, g, w_pad, b_pad, tm=min(512, seq))

    e_flat = idx[:, :TOP_K].reshape(-1)
    onehot = (e_flat[:, None] == jnp.arange(N_EXPERTS, dtype=jnp.int32)[None, :]).astype(jnp.int32)
    rank = jnp.cumsum(onehot, axis=0) - onehot
    counts = jnp.sum(onehot, axis=0)
    padded = ((counts + tm - 1) // tm) * tm
    ends = jnp.cumsum(padded)
    starts = ends - padded
    pos = jnp.sum(onehot * (starts[None, :] + rank), axis=1).astype(jnp.int32)
    p_total = TOP_K * seq + N_EXPERTS * tm
    tok = jnp.arange(TOP_K * seq, dtype=jnp.int32) // TOP_K
    src = jnp.zeros((p_total,), jnp.int32).at[pos].set(tok)
    ws = jnp.zeros((p_total,), F32).at[pos].set(wgt[:, :TOP_K].reshape(-1))[:, None]
    tile_start = jnp.arange(p_total // tm, dtype=jnp.int32) * tm
    tile_expert = jnp.minimum(jnp.sum((tile_start[:, None] >= ends[None, :]).astype(jnp.int32), axis=1),
                              N_EXPERTS - 1).astype(jnp.int32)
    tile_valid = (tile_start < ends[-1]).astype(jnp.int32)
    tile_expert = jnp.where(tile_valid > 0, tile_expert, jnp.max(jnp.where(tile_valid > 0, tile_expert, 0)))

    xs = gather_rows(x, src, tg=min(256, seq))
    ys = moe_ffn(xs, g, ws, tile_expert, tile_valid, w_gate_up, w_down, tm=tm, tf=tf)
    pos2 = pos.reshape(seq, TOP_K)
    return moe_combine(x, ys, pos2[:, 0], pos2[:, 1], tt=min(256, seq))


def _repack_w_in(w):
    return jnp.concatenate([w[:, 0:1024], w[:, 1024:1472], w[:, 1984:2048], w[:, 1472:1984],
                            w[:, 2048:]], axis=1).astype(BF16)


def _repack_w_q_b(w):
    w = w.reshape(MLA_Q_RANK, MLA_HEADS, MLA_QK)
    w = jnp.pad(w, ((0, 512 - MLA_Q_RANK), (0, 0), (0, MLA_QK_PAD - MLA_QK)))
    return w.reshape(512, MLA_HEADS * MLA_QK_PAD).astype(BF16)


def mixer_layer(x, mem, pos_col, pos_row, invf, p):
    seq = x.shape[0]
    tm = min(1024, seq)
    row = lambda v: v[None, :]
    proj = norm_matmul(x, row(p['norm_mix']), _repack_w_in(p['w_in']), tm=tm, tn=1024)

    y = s5_mix(proj[:, PJ_U:PJ_U + S5_WIDTH], p['s5_a_re'], p['s5_a_im'], p['s5_log_dt'], p['s5_b_re'],
               p['s5_b_im'], p['s5_c_re'], p['s5_c_im'], p['s5_d'])
    merged = glu_gate(y, p['s5_w_glu'].astype(BF16), proj, PJ_GATE, tm=tm, tn=1024)

    ga = row(jnp.pad(p['mla_q_a_norm'], (0, 512 - MLA_Q_RANK)))
    gq = row(jnp.pad(p['mla_q_norm'], (0, MLA_QK_PAD - MLA_QK)))
    gkn = row(p['mla_k_norm'][:MLA_NOPE])
    gkr = row(jnp.pad(p['mla_k_norm'][MLA_NOPE:], (0, 128 - MLA_ROPE)))
    q, k, v = mla_prep(proj, pos_col, ga, row(p['mla_kv_norm']), gq, gkn, gkr, invf,
                       _repack_w_q_b(p['mla_w_q_b']), p['mla_w_kv_b'].astype(BF16), tm=min(256, seq))
    o = flash_attention(q, k, v, pos_col, pos_row, tq=min(512, seq), tk=min(512, seq))
    merged = gate_add(o, p['mla_w_o'].astype(BF16), proj, PJ_GATE + D_MODEL, merged, tm=tm, tn=1024)

    km, vm = mem_kv(mem, row(p['mem_norm']), p['mem_w_kv'].astype(BF16), row(p['mem_k_norm']))
    om = mem_attention(proj, km, vm, row(p['mem_q_norm']), tm=min(512, seq))
    merged = gate_add(om, p['mem_w_o'].astype(BF16), proj, PJ_GATE + 2 * D_MODEL, merged, tm=tm, tn=1024)

    return resid_matmul(merged, p['w_out'].astype(BF16), x, tm=tm, tn=1024)


_LAYER_KEYS = ('norm_mix', 'w_in', 's5_a_re', 's5_a_im', 's5_log_dt', 's5_b_re', 's5_b_im', 's5_c_re',
               's5_c_im', 's5_d', 's5_w_glu', 'mla_q_a_norm', 'mla_w_q_b', 'mla_kv_norm', 'mla_w_kv_b',
               'mla_q_norm', 'mla_k_norm', 'mla_w_o', 'mem_norm', 'mem_w_kv', 'mem_q_norm', 'mem_k_norm',
               'mem_w_o', 'w_out')


def kernel(x, mem, positions, norm_mix, w_in, s5_a_re, s5_a_im, s5_log_dt, s5_b_re, s5_b_im, s5_c_re, s5_c_im, s5_d, s5_w_glu, mla_q_a_norm, mla_w_q_b, mla_kv_norm, mla_w_kv_b, mla_q_norm, mla_k_norm, mla_w_o, mem_norm, mem_w_kv, mem_q_norm, mem_k_norm, mem_w_o, w_out, norm_ffn, ffn_w_gate_up, ffn_w_down, moe_router, moe_router_b, moe_w_gate_up, moe_w_down):
    stacked = dict(zip(_LAYER_KEYS, (norm_mix, w_in, s5_a_re, s5_a_im, s5_log_dt, s5_b_re, s5_b_im, s5_c_re,
                                     s5_c_im, s5_d, s5_w_glu, mla_q_a_norm, mla_w_q_b, mla_kv_norm,
                                     mla_w_kv_b, mla_q_norm, mla_k_norm, mla_w_o, mem_norm, mem_w_kv,
                                     mem_q_norm, mem_k_norm, mem_w_o, w_out)))
    bsz, seq, d = x.shape
    depth = norm_mix.shape[0]
    half = MLA_ROPE // 2
    inv_freq = ROPE_THETA ** (-jnp.arange(0, MLA_ROPE, 2, dtype=F32) / MLA_ROPE)
    invf = jnp.concatenate([inv_freq, inv_freq, jnp.zeros((128 - 2 * half,), F32)])[None, :]

    outs = []
    for b in range(bsz):
        xb = x[b]
        pos_col = positions[b][:, None]
        pos_row = positions[b][None, :]
        for l in range(depth):
            p = {key: val[l] for key, val in stacked.items()}
            xb = mixer_layer(xb, mem[b], pos_col, pos_row, invf, p)
            gf = norm_ffn[l][None, :]
            if l % 2 == 0:
                xb = dense_ffn(xb, gf, ffn_w_gate_up[l // 2].astype(BF16), ffn_w_down[l // 2].astype(BF16),
                               tm=min(512, seq), tf=512)
            else:
                xb = moe_layer(xb, gf, moe_router[l // 2], moe_router_b[l // 2],
                               moe_w_gate_up[l // 2].astype(BF16), moe_w_down[l // 2].astype(BF16),
                               tm=min(512, seq), tf=512)
        outs.append(xb)
    return jnp.stack(outs, axis=0)
```

```python
import functools
import math

import jax
import jax.numpy as jnp
from jax import lax
from jax.experimental import pallas as pl
from jax.experimental.pallas import tpu as pltpu

F32 = jnp.float32
BF16 = jnp.bfloat16
HIGHEST = lax.Precision.HIGHEST

D_MODEL = 2048
N_MEM = 256
S5_WIDTH = 1024
S5_GROUP = 16
S5_GROUPS = S5_WIDTH // S5_GROUP
S5_STATE = 64
S5_CHUNK = 64
S5_CK = S5_CHUNK * S5_GROUP
MLA_HEADS = 16
MLA_Q_RANK = 448
MLA_KV_RANK = 512
MLA_NOPE = 128
MLA_ROPE = 64
MLA_QK = MLA_NOPE + MLA_ROPE
MLA_V = 128
MLA_QK_PAD = 256
MLA_VP = 256
ROPE_THETA = 10000.0
MEM_HEADS = 4
MEM_HEAD_DIM = 256
MEM_WIDTH = MEM_HEADS * MEM_HEAD_DIM
N_BRANCH = 3
D_FF = 7168
N_EXPERTS = 8
TOP_K = 2
EPS = 1e-6
NEG_INF = -1e30

PJ_U = 0
PJ_MLA = 1024
PJ_MQ = 2048
PJ_GATE = 3072
MLA_W = MLA_Q_RANK + MLA_KV_RANK + MLA_ROPE
CQ_W = 512
CKV_OFF = 384
CKV_W = MLA_W - CKV_OFF
CKV_LO = MLA_Q_RANK - CKV_OFF
KR_OFF = MLA_W - 128

VMEM_LIMIT = 52 * 1024 * 1024


def _params(sem):
    return pltpu.CompilerParams(dimension_semantics=sem, vmem_limit_bytes=VMEM_LIMIT)


def _rms(x, g):
    r = lax.rsqrt(jnp.mean(x * x, axis=-1, keepdims=True) + EPS)
    return x * r * g


def _onehot_dot(a, b):
    if a.dtype == BF16:
        hi = b.astype(BF16)
        lo = (b - hi.astype(F32)).astype(BF16)
        return (jnp.dot(a, hi, preferred_element_type=F32) + jnp.dot(a, lo, preferred_element_type=F32))
    hi = a.astype(BF16)
    lo = (a - hi.astype(F32)).astype(BF16)
    return (jnp.dot(hi, b, preferred_element_type=F32) + jnp.dot(lo, b, preferred_element_type=F32))


def _norm_mm_kernel(x_ref, g_ref, w_ref, o_ref, h_ref):
    @pl.when(pl.program_id(1) == 0)
    def _():
        h_ref[...] = _rms(x_ref[...], g_ref[...]).astype(BF16)

    o_ref[...] = jnp.dot(h_ref[...], w_ref[...], preferred_element_type=F32).astype(o_ref.dtype)


def norm_matmul(x, g, w, *, tm, tn, out_dtype=BF16):
    m, k = x.shape
    n = w.shape[1]
    return pl.pallas_call(
        _norm_mm_kernel,
        grid=(m // tm, n // tn),
        in_specs=[pl.BlockSpec((tm, k), lambda i, j: (i, 0)),
                  pl.BlockSpec((1, k), lambda i, j: (0, 0)),
                  pl.BlockSpec((k, tn), lambda i, j: (0, j))],
        out_specs=pl.BlockSpec((tm, tn), lambda i, j: (i, j)),
        out_shape=jax.ShapeDtypeStruct((m, n), out_dtype),
        scratch_shapes=[pltpu.VMEM((tm, k), BF16)],
        compiler_params=_params(("parallel", "arbitrary")),
    )(x, g, w)


def _glu_gate_kernel(a_ref, wa_ref, wb_ref, gate_ref, o_ref):
    a = a_ref[...]
    ya = jnp.dot(a, wa_ref[...], preferred_element_type=F32)
    yb = jnp.dot(a, wb_ref[...], preferred_element_type=F32)
    g = jax.nn.sigmoid(gate_ref[...].astype(F32))
    o_ref[...] = (g * (ya * jax.nn.sigmoid(yb))).astype(o_ref.dtype)


def glu_gate(a, w, proj, gate_col, *, tm, tn):
    m, k = a.shape
    n = w.shape[1] // 2
    nb = n // tn
    gb = gate_col // tn
    return pl.pallas_call(
        _glu_gate_kernel,
        grid=(m // tm, nb),
        in_specs=[pl.BlockSpec((tm, k), lambda i, j: (i, 0)),
                  pl.BlockSpec((k, tn), lambda i, j: (0, j)),
                  pl.BlockSpec((k, tn), lambda i, j: (0, j + nb)),
                  pl.BlockSpec((tm, tn), lambda i, j: (i, j + gb))],
        out_specs=pl.BlockSpec((tm, tn), lambda i, j: (i, j)),
        out_shape=jax.ShapeDtypeStruct((m, n), BF16),
        compiler_params=_params(("parallel", "parallel")),
    )(a, w, w, proj)


def _gate_add_kernel(a_ref, w_ref, gate_ref, prev_ref, o_ref):
    y = jnp.dot(a_ref[...], w_ref[...], preferred_element_type=F32)
    g = jax.nn.sigmoid(gate_ref[...].astype(F32))
    o_ref[...] = (prev_ref[...].astype(F32) + g * y).astype(o_ref.dtype)


def gate_add(a, w, proj, gate_col, prev, *, tm, tn):
    m, k = a.shape
    n = w.shape[1]
    gb = gate_col // tn
    return pl.pallas_call(
        _gate_add_kernel,
        grid=(m // tm, n // tn),
        in_specs=[pl.BlockSpec((tm, k), lambda i, j: (i, 0)),
                  pl.BlockSpec((k, tn), lambda i, j: (0, j)),
                  pl.BlockSpec((tm, tn), lambda i, j: (i, j + gb)),
                  pl.BlockSpec((tm, tn), lambda i, j: (i, j))],
        out_specs=pl.BlockSpec((tm, tn), lambda i, j: (i, j)),
        out_shape=jax.ShapeDtypeStruct((m, n), BF16),
        compiler_params=_params(("parallel", "parallel")),
    )(a, w, proj, prev)


def _resid_mm_kernel(a_ref, w_ref, x_ref, o_ref):
    o_ref[...] = x_ref[...] + jnp.dot(a_ref[...], w_ref[...], preferred_element_type=F32)


def resid_matmul(a, w, x, *, tm, tn):
    m, k = a.shape
    n = w.shape[1]
    return pl.pallas_call(
        _resid_mm_kernel,
        grid=(m // tm, n // tn),
        in_specs=[pl.BlockSpec((tm, k), lambda i, j: (i, 0)),
                  pl.BlockSpec((k, tn), lambda i, j: (0, j)),
                  pl.BlockSpec((tm, tn), lambda i, j: (i, j))],
        out_specs=pl.BlockSpec((tm, tn), lambda i, j: (i, j)),
        out_shape=jax.ShapeDtypeStruct((m, n), F32),
        compiler_params=_params(("parallel", "parallel")),
    )(a, w, x)


def _s5_kernel(u_ref, rowp_ref, colp_ref, bt_ref, ct_ref, y_ref, toep_ref):
    t_chunk = S5_CHUNK
    n_chunks = u_ref.shape[1]
    p2 = 2 * S5_STATE

    rowp = rowp_ref[0]
    are_r, aim_r = rowp[0:1], rowp[1:2]
    dt_r = jnp.exp(rowp[2:3])
    colp = colp_ref[0]
    are_c, aim_c = colp[:, 0:1], colp[:, 1:2]
    dt_c = jnp.exp(colp[:, 2:3])
    d_c = colp[0:S5_GROUP, 3:4]

    row_q = lax.broadcasted_iota(jnp.int32, (p2, S5_CK), 0)
    lane_q = lax.broadcasted_iota(jnp.int32, (1, p2), 1)

    tau = lax.broadcasted_iota(jnp.int32, (p2, p2), 1).astype(F32)
    row_pp = lax.broadcasted_iota(jnp.int32, (p2, p2), 0)
    mag = jnp.exp(tau * dt_c * are_c)
    ang = tau * dt_c * aim_c
    pw = mag * jnp.where(row_pp < S5_STATE, jnp.cos(ang), jnp.sin(ang))

    l_t = lax.broadcasted_iota(jnp.int32, (p2, S5_CK), 1) // S5_GROUP
    e_tau = (l_t == row_q).astype(BF16)
    e_tau1 = (l_t + 1 == row_q).astype(BF16)
    l_i = lax.broadcasted_iota(jnp.int32, (S5_GROUP, S5_CK), 1) % S5_GROUP
    e_ch = (l_i == lax.broadcasted_iota(jnp.int32, (S5_GROUP, S5_CK), 0)).astype(BF16)

    def swap_halves(x):
        return jnp.concatenate([x[S5_STATE:], x[:S5_STATE]], axis=0)

    x1 = _onehot_dot(pw, e_tau)
    x1s = _onehot_dot(pw, e_tau1)
    ct = ct_ref[0]
    cta = _onehot_dot(ct[0], e_ch)
    ctb = _onehot_dot(ct[1], e_ch)
    ctb = jnp.where(row_q < S5_STATE, -ctb, ctb)
    z = cta * x1 + ctb * swap_halves(x1)
    zs = cta * x1s + ctb * swap_halves(x1s)
    wt = jnp.where(row_q < S5_STATE, zs, -zs)

    lam_mag = jnp.exp(dt_r * are_r)
    lam_re = lam_mag * jnp.cos(dt_r * aim_r)
    lam_im = lam_mag * jnp.sin(dt_r * aim_r)
    den = are_r * are_r + aim_r * aim_r
    n_re = lam_re - 1.0
    f_re = (n_re * are_r + lam_im * aim_r) / den
    f_im = (lam_im * are_r - n_re * aim_r) / den
    bt = bt_ref[0]
    bb_re = f_re * bt[0] - f_im * bt[1]
    bb_im = f_re * bt[1] + f_im * bt[0]
    first = lane_q < S5_STATE

    kt = jnp.dot(jnp.where(first, bb_re, -bb_im), z, precision=HIGHEST, preferred_element_type=F32)
    lane_k = lax.broadcasted_iota(jnp.int32, (S5_GROUP, S5_CK), 1)
    row_k = lax.broadcasted_iota(jnp.int32, (S5_GROUP, S5_CK), 0)
    kt = kt + jnp.where(lane_k == row_k, d_c, 0.0)
    for s in range(t_chunk):
        off = S5_GROUP * s
        blk = kt if s == 0 else jnp.where(lane_k >= off, pltpu.roll(kt, off, 1), 0.0)
        toep_ref[off:off + S5_GROUP, :] = blk.astype(BF16)

    e_s = (t_chunk - 1 - lax.broadcasted_iota(jnp.int32, (t_chunk, p2), 0)).astype(F32)
    pt_mag = jnp.exp(e_s * dt_r * are_r)
    pt_ang = e_s * dt_r * aim_r
    pt_re = pt_mag * jnp.cos(pt_ang)
    pt_im = pt_mag * jnp.sin(pt_ang)
    r_s = lax.broadcasted_iota(jnp.int32, (S5_CK, t_chunk), 0) // S5_GROUP
    e_rows = (r_s == lax.broadcasted_iota(jnp.int32, (S5_CK, t_chunk), 1)).astype(BF16)
    r_j = lax.broadcasted_iota(jnp.int32, (S5_CK, S5_GROUP), 0) % S5_GROUP
    e_rowj = (r_j == lax.broadcasted_iota(jnp.int32, (S5_CK, S5_GROUP), 1)).astype(BF16)
    gt = (_onehot_dot(e_rows, pt_re) * _onehot_dot(e_rowj, jnp.where(first, bb_re, bb_im))
          + _onehot_dot(e_rows, pt_im) * _onehot_dot(e_rowj, jnp.where(first, -bb_im, bb_re)))

    u = u_ref[0]
    h = jnp.dot(u, gt.astype(BF16), preferred_element_type=F32)

    k_row = lax.broadcasted_iota(jnp.int32, (8, p2), 0)
    m_pow = (t_chunk * (1 << k_row)).astype(F32)
    m_mag = jnp.exp(m_pow * dt_r * are_r)
    m_ang = m_pow * dt_r * aim_r
    m_re = m_mag * jnp.cos(m_ang)
    m_im = m_mag * jnp.sin(m_ang)
    row_c = lax.broadcasted_iota(jnp.int32, (n_chunks, p2), 0)
    k = 0
    while (1 << k) < n_chunks:
        sh = 1 << k
        s_prev = jnp.where(row_c >= sh, pltpu.roll(h, sh, 0), 0.0)
        s_swap = pltpu.roll(s_prev, S5_STATE, 1)
        mb = jnp.where(first, -m_im[k:k + 1], m_im[k:k + 1])
        h = h + m_re[k:k + 1] * s_prev + mb * s_swap
        k += 1
    h_prev = jnp.where(row_c >= 1, pltpu.roll(h, 1, 0), 0.0)

    y = (jnp.dot(u, toep_ref[...], preferred_element_type=F32)
         + jnp.dot(h_prev.astype(BF16), wt.astype(BF16), preferred_element_type=F32))
    y_ref[0] = jax.nn.gelu(y).astype(y_ref.dtype)


def s5_mix(u, a_re, a_im, log_dt, b_re, b_im, c_re, c_im, d):
    seq = u.shape[0]
    n_chunks = seq // S5_CHUNK
    g = S5_GROUPS
    ug = u.reshape(n_chunks, S5_CHUNK, g, S5_GROUP).transpose(2, 0, 1, 3).reshape(g, n_chunks, S5_CK)

    dup = lambda v: jnp.concatenate([v, v], axis=-1)
    zeros = jnp.zeros((g, 2 * S5_STATE), F32)
    ldt = jnp.broadcast_to(log_dt[:, None], (g, 2 * S5_STATE))
    rowp = jnp.stack([dup(a_re), dup(a_im), ldt] + [zeros] * 5, axis=1)
    d_pad = jnp.pad(d, ((0, 0), (0, 2 * S5_STATE - S5_GROUP)))
    colp = jnp.stack([dup(a_re), dup(a_im), ldt, d_pad] + [zeros] * 4, axis=2)
    bt = jnp.stack([dup(b_re.transpose(0, 2, 1)), dup(b_im.transpose(0, 2, 1))], axis=1)
    ctr, cti = c_re.transpose(0, 2, 1), c_im.transpose(0, 2, 1)
    ct = jnp.stack([jnp.concatenate([ctr, ctr], 1), jnp.concatenate([cti, cti], 1)], axis=1)

    yg = pl.pallas_call(
        _s5_kernel,
        grid=(g,),
        in_specs=[pl.BlockSpec((1, n_chunks, S5_CK), lambda i: (i, 0, 0)),
                  pl.BlockSpec((1, 8, 2 * S5_STATE), lambda i: (i, 0, 0)),
                  pl.BlockSpec((1, 2 * S5_STATE, 8), lambda i: (i, 0, 0)),
                  pl.BlockSpec((1, 2, S5_GROUP, 2 * S5_STATE), lambda i: (i, 0, 0, 0)),
                  pl.BlockSpec((1, 2, 2 * S5_STATE, S5_GROUP), lambda i: (i, 0, 0, 0))],
        out_specs=pl.BlockSpec((1, n_chunks, S5_CK), lambda i: (i, 0, 0)),
        out_shape=jax.ShapeDtypeStruct((g, n_chunks, S5_CK), BF16),
        scratch_shapes=[pltpu.VMEM((S5_CK, S5_CK), BF16)],
        compiler_params=_params(("parallel",)),
    )(ug, rowp, colp, bt, ct)
    return yg.reshape(g, n_chunks, S5_CHUNK, S5_GROUP).transpose(1, 2, 0, 3).reshape(seq, S5_WIDTH)


def _mla_prep_kernel(mla_ref, pos_ref, ga_ref, gkv_ref, gq_ref, gkn_ref, gkr_ref,
                     invf_ref, wq_ref, wkv_ref, q_ref, k_ref, v_ref):
    cq = mla_ref[:, 0:CQ_W].astype(F32)
    lane = lax.broadcasted_iota(jnp.int32, cq.shape, 1)
    ssq = jnp.sum(jnp.where(lane < MLA_Q_RANK, cq * cq, 0.0), axis=-1, keepdims=True)
    hq = (cq * lax.rsqrt(ssq * (1.0 / MLA_Q_RANK) + EPS) * ga_ref[...]).astype(BF16)
    qf = jnp.dot(hq, wq_ref[...], preferred_element_type=F32)
    ckv = mla_ref[:, CKV_OFF:].astype(F32)
    lane_kv = lax.broadcasted_iota(jnp.int32, ckv.shape, 1)
    in_kv = jnp.logical_and(lane_kv >= CKV_LO, lane_kv < CKV_LO + MLA_KV_RANK)
    ssq_kv = jnp.sum(jnp.where(in_kv, ckv * ckv, 0.0), axis=-1, keepdims=True)
    hkv = (ckv * lax.rsqrt(ssq_kv * (1.0 / MLA_KV_RANK) + EPS) * gkv_ref[...]).astype(BF16)
    kvf = jnp.dot(hkv, wkv_ref[...], preferred_element_type=F32)

    l128 = lax.broadcasted_iota(jnp.int32, (cq.shape[0], 128), 1)
    half = MLA_ROPE // 2
    ang = pos_ref[...].astype(F32) * invf_ref[...]
    cosv = jnp.cos(ang)
    sinv = jnp.sin(ang)
    sgn_sin = jnp.where(l128 < half, -sinv, jnp.where(l128 < MLA_ROPE, sinv, 0.0))

    def rope(t):
        sw = jnp.where(l128 < half, pltpu.roll(t, 128 - half, 1), pltpu.roll(t, half, 1))
        return t * cosv + sw * sgn_sin

    kr = jnp.where(l128 < MLA_ROPE, pltpu.roll(mla_ref[:, KR_OFF:].astype(F32), MLA_ROPE, 1), 0.0)
    kr_ssq = jnp.sum(kr * kr, axis=-1, keepdims=True)
    kr_rot = rope(kr * gkr_ref[...])

    gq = gq_ref[...]
    gkn = gkn_ref[...]
    scale = MLA_QK ** -0.5
    inv_qk = 1.0 / MLA_QK
    ones_col = jnp.where(l128 == 0, 1.0, 0.0).astype(BF16)
    for h in range(MLA_HEADS):
        qh = qf[:, MLA_QK_PAD * h:MLA_QK_PAD * (h + 1)]
        rq = lax.rsqrt(jnp.sum(qh * qh, axis=-1, keepdims=True) * inv_qk + EPS)
        qn = qh * rq * gq
        q_ref[h] = (jnp.concatenate([qn[:, :MLA_NOPE], rope(qn[:, MLA_NOPE:])], axis=1) * scale).astype(BF16)
        kn = kvf[:, MLA_QK_PAD * h:MLA_QK_PAD * h + MLA_NOPE]
        rk = lax.rsqrt((jnp.sum(kn * kn, axis=-1, keepdims=True) + kr_ssq) * inv_qk + EPS)
        k_ref[h] = jnp.concatenate([kn * rk * gkn, kr_rot * rk], axis=1).astype(BF16)
        vh = kvf[:, MLA_QK_PAD * h + MLA_NOPE:MLA_QK_PAD * (h + 1)].astype(BF16)
        v_ref[h] = jnp.concatenate([vh, ones_col], axis=1)


def mla_prep(proj, pos_col, ga, gkv, gq, gkn, gkr, invf, wq, wkv, *, tm):
    seq = proj.shape[0]
    h = MLA_HEADS
    full = lambda shape: pl.BlockSpec(shape, lambda i: (0,) * len(shape))
    return pl.pallas_call(
        _mla_prep_kernel,
        grid=(seq // tm,),
        in_specs=[pl.BlockSpec((tm, MLA_W), lambda i: (i, PJ_MLA // MLA_W)),
                  pl.BlockSpec((tm, 1), lambda i: (i, 0)),
                  full((1, CQ_W)), full((1, CKV_W)), full((1, MLA_QK_PAD)), full((1, 128)), full((1, 128)),
                  full((1, 128)), full((CQ_W, h * MLA_QK_PAD)), full((CKV_W, h * MLA_QK_PAD))],
        out_specs=[pl.BlockSpec((h, tm, MLA_QK_PAD), lambda i: (0, i, 0)),
                   pl.BlockSpec((h, tm, MLA_QK_PAD), lambda i: (0, i, 0)),
                   pl.BlockSpec((h, tm, MLA_VP), lambda i: (0, i, 0))],
        out_shape=[jax.ShapeDtypeStruct((h, seq, MLA_QK_PAD), BF16),
                   jax.ShapeDtypeStruct((h, seq, MLA_QK_PAD), BF16),
                   jax.ShapeDtypeStruct((h, seq, MLA_VP), BF16)],
        compiler_params=_params(("parallel",)),
    )(proj, pos_col, ga, gkv, gq, gkn, gkr, invf, wq, wkv)


def _flash_kernel(q_ref, k_ref, v_ref, pq_ref, pk_ref, o_ref, sa, sb, pa, pb, ala, alb, m_ref, acc_ref, *, t):
    qi = pl.program_id(1)
    nt = (((1,), (1,)), ((), ()))

    def blk(j):
        return pl.ds(pl.multiple_of(j * t, t), t)

    def stage_a(j, s_out):
        s_out[...] = lax.dot_general(q_ref[0], k_ref[0, blk(j), :], nt, preferred_element_type=F32)

    def stage_b(j, s_in, p_out, al_out, masked):
        s = s_in[...]
        if masked:
            s = jnp.where(pk_ref[:, blk(j)] <= pq_ref[...], s, NEG_INF)
        m_prev = m_ref[...]
        m_new = jnp.maximum(m_prev, jnp.max(s, axis=-1, keepdims=True))
        al_out[...] = jnp.exp(m_prev - m_new)
        p_out[...] = jnp.exp((s - jnp.concatenate([m_new] * (t // 128), axis=1)).astype(BF16))
        m_ref[...] = m_new

    def stage_c(j, p_in, al_in):
        al = al_in[...]
        acc_ref[...] = (jnp.concatenate([al] * (MLA_VP // 128), axis=1) * acc_ref[...]
                        + jnp.dot(p_in[...], v_ref[0, blk(j), :], preferred_element_type=F32))

    m_ref[...] = jnp.full(m_ref.shape, NEG_INF, F32)
    acc_ref[...] = jnp.zeros(acc_ref.shape, F32)
    pb[...] = jnp.zeros(pb.shape, BF16)
    alb[...] = jnp.ones(alb.shape, F32)
    stage_a(0, sa)

    def pair_body(j):
        stage_a(j + 1, sb)
        stage_b(j, sa, pa, ala, False)
        stage_c(jnp.maximum(j - 1, 0), pb, alb)
        stage_a(j + 2, sa)
        stage_b(j + 1, sb, pb, alb, False)
        stage_c(j, pa, ala)

    def quad(i, c):
        pair_body(4 * i)
        pair_body(4 * i + 2)
        return c

    lax.fori_loop(0, qi // 4, quad, 0)

    @pl.when((qi // 2) % 2 == 1)
    def _():
        pair_body(4 * (qi // 4))

    @pl.when(qi % 2 == 0)
    def _():
        stage_b(qi, sa, pa, ala, True)
        stage_c(jnp.maximum(qi - 1, 0), pb, alb)
        stage_c(qi, pa, ala)

    @pl.when(qi % 2 == 1)
    def _():
        stage_a(qi, sb)
        stage_b(qi - 1, sa, pa, ala, False)
        stage_c(jnp.maximum(qi - 2, 0), pb, alb)
        stage_b(qi, sb, pb, alb, True)
        stage_c(qi - 1, pa, ala)
        stage_c(qi, pb, alb)

    acc = acc_ref[...]
    o_ref[...] = (acc[:, :MLA_V] / acc[:, MLA_V:MLA_V + 1]).astype(o_ref.dtype)


def flash_attention(q, k, v, pos_col, pos_row, *, t):
    h, seq, _ = q.shape
    return pl.pallas_call(
        functools.partial(_flash_kernel, t=t),
        grid=(h, seq // t),
        in_specs=[pl.BlockSpec((1, t, MLA_QK_PAD), lambda hh, i: (hh, i, 0)),
                  pl.BlockSpec((1, seq, MLA_QK_PAD), lambda hh, i: (hh, 0, 0)),
                  pl.BlockSpec((1, seq, MLA_VP), lambda hh, i: (hh, 0, 0)),
                  pl.BlockSpec((t, 1), lambda hh, i: (i, 0)),
                  pl.BlockSpec((1, seq), lambda hh, i: (0, 0))],
        out_specs=pl.BlockSpec((t, MLA_V), lambda hh, i: (i, hh)),
        out_shape=jax.ShapeDtypeStruct((seq, h * MLA_V), BF16),
        scratch_shapes=[pltpu.VMEM((t, t), F32), pltpu.VMEM((t, t), F32),
                        pltpu.VMEM((t, t), BF16), pltpu.VMEM((t, t), BF16),
                        pltpu.VMEM((t, 128), F32), pltpu.VMEM((t, 128), F32),
                        pltpu.VMEM((t, 128), F32), pltpu.VMEM((t, MLA_VP), F32)],
        compiler_params=_params(("parallel", "arbitrary")),
    )(q, k, v, pos_col, pos_row)


def _mem_kv_kernel(mem_ref, g_ref, w_ref, gk_ref, k_ref, v_ref):
    m = _rms(mem_ref[...], g_ref[...]).astype(BF16)
    kv = jnp.dot(m, w_ref[...], preferred_element_type=F32)
    for h in range(MEM_HEADS):
        kh = kv[:, MEM_HEAD_DIM * h:MEM_HEAD_DIM * (h + 1)]
        k_ref[:, MEM_HEAD_DIM * h:MEM_HEAD_DIM * (h + 1)] = _rms(kh, gk_ref[...]).astype(BF16)
    v_ref[...] = kv[:, MEM_WIDTH:].astype(BF16)


def mem_kv(mem, g, w, gk):
    n_mem = mem.shape[0]
    full = lambda shape: pl.BlockSpec(shape, lambda i: (0,) * len(shape))
    return pl.pallas_call(
        _mem_kv_kernel,
        grid=(1,),
        in_specs=[full(mem.shape), full(g.shape), full(w.shape), full(gk.shape)],
        out_specs=[full((n_mem, MEM_WIDTH)), full((n_mem, MEM_WIDTH))],
        out_shape=[jax.ShapeDtypeStruct((n_mem, MEM_WIDTH), BF16)] * 2,
        compiler_params=_params(("arbitrary",)),
    )(mem, g, w, gk)


def _mem_attn_kernel(q_ref, k_ref, v_ref, gq_ref, o_ref):
    scale = MEM_HEAD_DIM ** -0.5
    for h in range(MEM_HEADS):
        sl = slice(MEM_HEAD_DIM * h, MEM_HEAD_DIM * (h + 1))
        qh = (_rms(q_ref[:, sl].astype(F32), gq_ref[...]) * scale).astype(BF16)
        s = lax.dot_general(qh, k_ref[:, sl], (((1,), (1,)), ((), ())), preferred_element_type=F32)
        p = jnp.exp(s - jnp.max(s, axis=-1, keepdims=True))
        o = jnp.dot(p.astype(BF16), v_ref[:, sl], preferred_element_type=F32)
        o_ref[:, sl] = (o / jnp.sum(p, axis=-1, keepdims=True)).astype(o_ref.dtype)


def mem_attention(proj, k, v, gq, *, tm):
    seq = proj.shape[0]
    n_mem = k.shape[0]
    return pl.pallas_call(
        _mem_attn_kernel,
        grid=(seq // tm,),
        in_specs=[pl.BlockSpec((tm, MEM_WIDTH), lambda i: (i, PJ_MQ // MEM_WIDTH)),
                  pl.BlockSpec((n_mem, MEM_WIDTH), lambda i: (0, 0)),
                  pl.BlockSpec((n_mem, MEM_WIDTH), lambda i: (0, 0)),
                  pl.BlockSpec((1, MEM_HEAD_DIM), lambda i: (0, 0))],
        out_specs=pl.BlockSpec((tm, MEM_WIDTH), lambda i: (i, 0)),
        out_shape=jax.ShapeDtypeStruct((seq, MEM_WIDTH), BF16),
        compiler_params=_params(("parallel",)),
    )(proj, k, v, gq)


def _ffn_kernel(x_ref, g_ref, wg_ref, wu_ref, wd_ref, o_ref, h_ref, acc_ref):
    f = pl.program_id(1)

    @pl.when(f == 0)
    def _():
        h_ref[...] = _rms(x_ref[...], g_ref[...]).astype(BF16)
        acc_ref[...] = jnp.zeros_like(acc_ref)

    h = h_ref[...]
    gate = jnp.dot(h, wg_ref[...], preferred_element_type=F32)
    up = jnp.dot(h, wu_ref[...], preferred_element_type=F32)
    a = (jax.nn.silu(gate) * up).astype(BF16)
    acc_ref[...] += jnp.dot(a, wd_ref[...], preferred_element_type=F32)

    @pl.when(f == pl.num_programs(1) - 1)
    def _():
        o_ref[...] = x_ref[...] + acc_ref[...]


def dense_ffn(x, g, w_gate_up, w_down, *, tm, tf):
    m, d = x.shape
    nf = D_FF // tf
    return pl.pallas_call(
        _ffn_kernel,
        grid=(m // tm, nf),
        in_specs=[pl.BlockSpec((tm, d), lambda i, f: (i, 0)),
                  pl.BlockSpec((1, d), lambda i, f: (0, 0)),
                  pl.BlockSpec((d, tf), lambda i, f: (0, f)),
                  pl.BlockSpec((d, tf), lambda i, f: (0, f + nf)),
                  pl.BlockSpec((tf, d), lambda i, f: (f, 0))],
        out_specs=pl.BlockSpec((tm, d), lambda i, f: (i, 0)),
        out_shape=jax.ShapeDtypeStruct((m, d), F32),
        scratch_shapes=[pltpu.VMEM((tm, d), BF16), pltpu.VMEM((tm, d), F32)],
        compiler_params=_params(("parallel", "arbitrary")),
    )(x, g, w_gate_up, w_gate_up, w_down)


def _router_kernel(x_ref, g_ref, w_ref, b_ref, idx_ref, wgt_ref, h_ref):
    h = _rms(x_ref[...], g_ref[...])
    h_ref[...] = h.astype(BF16)
    logits = jnp.dot(h, w_ref[...], precision=HIGHEST, preferred_element_type=F32) + b_ref[...]
    lane = lax.broadcasted_iota(jnp.int32, logits.shape, 1)
    logits = jnp.where(lane < N_EXPERTS, logits, -jnp.inf)
    v1 = jnp.max(logits, axis=-1, keepdims=True)
    i1 = jnp.min(jnp.where(logits == v1, lane, 128), axis=-1, keepdims=True)
    rest = jnp.where(lane == i1, -jnp.inf, logits)
    v2 = jnp.max(rest, axis=-1, keepdims=True)
    i2 = jnp.min(jnp.where(rest == v2, lane, 128), axis=-1, keepdims=True)
    e2 = jnp.exp(v2 - v1)
    w1 = 1.0 / (1.0 + e2)
    w2 = e2 / (1.0 + e2)
    idx_ref[...] = jnp.where(lane == 0, i1, jnp.where(lane == 1, i2, 0))
    wgt_ref[...] = jnp.where(lane == 0, w1, jnp.where(lane == 1, w2, 0.0))


def moe_router(x, g, w_pad, b_pad, *, tm):
    m, d = x.shape
    return pl.pallas_call(
        _router_kernel,
        grid=(m // tm,),
        in_specs=[pl.BlockSpec((tm, d), lambda i: (i, 0)),
                  pl.BlockSpec((1, d), lambda i: (0, 0)),
                  pl.BlockSpec((d, 128), lambda i: (0, 0)),
                  pl.BlockSpec((1, 128), lambda i: (0, 0))],
        out_specs=[pl.BlockSpec((tm, 128), lambda i: (i, 0)),
                   pl.BlockSpec((tm, 128), lambda i: (i, 0)),
                   pl.BlockSpec((tm, d), lambda i: (i, 0))],
        out_shape=[jax.ShapeDtypeStruct((m, 128), jnp.int32),
                   jax.ShapeDtypeStruct((m, 128), F32),
                   jax.ShapeDtypeStruct((m, d), BF16)],
        compiler_params=_params(("parallel",)),
    )(x, g, w_pad, b_pad)


def _gather_rows_kernel(src_ref, used_ref, x_hbm, o_ref, sem, *, tg):
    i = pl.program_id(0)
    base = i * tg

    @pl.when(used_ref[i] > 0)
    def _():
        def issue(j, c):
            pltpu.make_async_copy(x_hbm.at[src_ref[base + j]], o_ref.at[j], sem).start()
            return c

        lax.fori_loop(0, tg, issue, 0, unroll=8)

        def drain(j, c):
            pltpu.make_async_copy(x_hbm.at[0], o_ref.at[j], sem).wait()
            return c

        lax.fori_loop(0, tg, drain, 0, unroll=8)

    @pl.when(used_ref[i] == 0)
    def _():
        o_ref[...] = jnp.zeros_like(o_ref)


def gather_rows(x3, src, used, *, tg):
    n = src.shape[0]
    tile = x3.shape[1:]
    return pl.pallas_call(
        functools.partial(_gather_rows_kernel, tg=tg),
        grid_spec=pltpu.PrefetchScalarGridSpec(
            num_scalar_prefetch=2,
            grid=(n // tg,),
            in_specs=[pl.BlockSpec(memory_space=pl.ANY)],
            out_specs=pl.BlockSpec((tg,) + tile, lambda i, src, used: (i, 0, 0)),
            scratch_shapes=[pltpu.SemaphoreType.DMA(())]),
        out_shape=jax.ShapeDtypeStruct((n,) + tile, x3.dtype),
        compiler_params=_params(("arbitrary",)),
    )(src, used, x3)


def _moe_ffn_kernel(te_ref, tr_ref, h_ref, wg_ref, wu_ref, wd_ref, o_ref):
    i = pl.program_id(0)
    f = pl.program_id(1)
    rows = tr_ref[i]
    tm = h_ref.shape[0]
    half = tm // 2

    @pl.when(f == 0)
    def _():
        o_ref[...] = jnp.zeros_like(o_ref)

    def run(n):
        h = h_ref[0:n, :]
        gate = jnp.dot(h, wg_ref[0].astype(BF16), preferred_element_type=F32)
        up = jnp.dot(h, wu_ref[0].astype(BF16), preferred_element_type=F32)
        a = (jax.nn.silu(gate) * up).astype(BF16)
        o_ref[0:n, :] += jnp.dot(a, wd_ref[0].astype(BF16), preferred_element_type=F32)

    @pl.when(rows > half)
    def _():
        run(tm)

    @pl.when(jnp.logical_and(rows > 0, rows <= half))
    def _():
        run(half)


def moe_ffn(hs, tile_expert, tile_rows, w_gate_up, w_down, *, tm, tf):
    p, d = hs.shape
    nf = D_FF // tf
    last = nf - 1

    def fsel(i, f, tr):
        return jnp.where(tr[i] > 0, f, last)

    return pl.pallas_call(
        _moe_ffn_kernel,
        grid_spec=pltpu.PrefetchScalarGridSpec(
            num_scalar_prefetch=2,
            grid=(p // tm, nf),
            in_specs=[pl.BlockSpec((tm, d), lambda i, f, te, tr: (i, 0)),
                      pl.BlockSpec((1, d, tf), lambda i, f, te, tr: (te[i], 0, fsel(i, f, tr))),
                      pl.BlockSpec((1, d, tf), lambda i, f, te, tr: (te[i], 0, fsel(i, f, tr) + nf)),
                      pl.BlockSpec((1, tf, d), lambda i, f, te, tr: (te[i], fsel(i, f, tr), 0))],
            out_specs=pl.BlockSpec((tm, d), lambda i, f, te, tr: (i, 0))),
        out_shape=jax.ShapeDtypeStruct((p, d), F32),
        compiler_params=_params(("arbitrary", "arbitrary")),
    )(tile_expert, tile_rows, hs, w_gate_up, w_gate_up, w_down)


def _combine_kernel(p0_ref, p1_ref, x_ref, w_ref, ys_hbm, o_ref, buf, sem, *, tt):
    base = pl.program_id(0) * tt

    def issue(j, c):
        r0 = p0_ref[base + j]
        r1 = p1_ref[base + j]
        pltpu.make_async_copy(ys_hbm.at[pl.ds(r0, 1), :], buf.at[0, pl.ds(j, 1), :], sem.at[0]).start()
        pltpu.make_async_copy(ys_hbm.at[pl.ds(r1, 1), :], buf.at[1, pl.ds(j, 1), :], sem.at[1]).start()
        return c

    lax.fori_loop(0, tt, issue, 0)

    def drain(j, c):
        pltpu.make_async_copy(ys_hbm.at[pl.ds(0, 1), :], buf.at[0, pl.ds(j, 1), :], sem.at[0]).wait()
        pltpu.make_async_copy(ys_hbm.at[pl.ds(0, 1), :], buf.at[1, pl.ds(j, 1), :], sem.at[1]).wait()
        return c

    lax.fori_loop(0, tt, drain, 0)
    w = w_ref[...]
    o_ref[...] = x_ref[...] + (w[:, 0:1] * buf[0] + w[:, 1:2] * buf[1])


def moe_combine(x, wgt, ys, pos0, pos1, *, tt):
    m, d = x.shape
    return pl.pallas_call(
        functools.partial(_combine_kernel, tt=tt),
        grid_spec=pltpu.PrefetchScalarGridSpec(
            num_scalar_prefetch=2,
            grid=(m // tt,),
            in_specs=[pl.BlockSpec((tt, d), lambda i, p0, p1: (i, 0)),
                      pl.BlockSpec((tt, 128), lambda i, p0, p1: (i, 0)),
                      pl.BlockSpec(memory_space=pl.ANY)],
            out_specs=pl.BlockSpec((tt, d), lambda i, p0, p1: (i, 0)),
            scratch_shapes=[pltpu.VMEM((2, tt, d), F32), pltpu.SemaphoreType.DMA((2,))]),
        out_shape=jax.ShapeDtypeStruct((m, d), F32),
        compiler_params=_params(("arbitrary",)),
    )(pos0, pos1, x, wgt, ys)


def moe_layer(x, g, router, router_b, w_gate_up, w_down, expert_base, *, tm, tf):
    seq, d = x.shape
    half = tm // 2
    w_pad = jnp.pad(router, ((0, 0), (0, 128 - N_EXPERTS)))
    b_pad = jnp.pad(router_b, (0, 128 - N_EXPERTS))[None, :]
    idx, wgt, h = moe_router(x, g, w_pad, b_pad, tm=min(512, seq))

    e_flat = idx[:, :TOP_K].reshape(-1)
    onehot = (e_flat[:, None] == jnp.arange(N_EXPERTS, dtype=jnp.int32)[None, :]).astype(jnp.int32)
    rank = jnp.cumsum(onehot, axis=0) - onehot
    counts = jnp.sum(onehot, axis=0)
    tiles_per = (counts + tm - 1) // tm
    tile_ends = jnp.cumsum(tiles_per)
    tile_starts = tile_ends - tiles_per
    pos = jnp.sum(onehot * (tile_starts[None, :] * tm + rank), axis=1).astype(jnp.int32)
    n_tiles = TOP_K * seq // tm + N_EXPERTS
    p_total = n_tiles * tm
    tok = jnp.arange(TOP_K * seq, dtype=jnp.int32) // TOP_K
    src = jnp.zeros((p_total,), jnp.int32).at[pos].set(tok)
    tile_idx = jnp.arange(n_tiles, dtype=jnp.int32)
    tile_expert = jnp.minimum(jnp.sum((tile_idx[:, None] >= tile_ends[None, :]).astype(jnp.int32), axis=1),
                              N_EXPERTS - 1).astype(jnp.int32)
    valid = tile_idx < tile_ends[-1]
    remaining = counts[tile_expert] - (tile_idx - tile_starts[tile_expert]) * tm
    tile_rows = jnp.where(valid, jnp.minimum(remaining, tm), 0).astype(jnp.int32)
    tile_expert = jnp.where(valid, tile_expert, jnp.max(jnp.where(valid, tile_expert, 0))).astype(jnp.int32)
    half_used = (jnp.repeat(tile_rows, 2) > jnp.tile(jnp.array([0, half], jnp.int32), n_tiles)).astype(jnp.int32)

    hs = gather_rows(h.reshape(seq, d // 128, 128), src, half_used, tg=half).reshape(p_total, d)
    ys = moe_ffn(hs, tile_expert + expert_base, tile_rows, w_gate_up, w_down, tm=tm, tf=tf)
    pos2 = pos.reshape(seq, TOP_K)
    return moe_combine(x, wgt, ys, pos2[:, 0], pos2[:, 1], tt=min(256, seq))


def _repack_w_q_b(w):
    w = w.reshape(MLA_Q_RANK, MLA_HEADS, MLA_QK)
    w = jnp.pad(w, ((0, CQ_W - MLA_Q_RANK), (0, 0), (0, MLA_QK_PAD - MLA_QK)))
    return w.reshape(CQ_W, MLA_HEADS * MLA_QK_PAD).astype(BF16)


def mixer_layer(x, mem, pos_col, pos_row, invf, p):
    seq = x.shape[0]
    tm = min(1024, seq)
    row = lambda v: v[None, :]
    proj = norm_matmul(x, row(p['norm_mix']), p['w_in'].astype(BF16), tm=tm, tn=1024)

    y = s5_mix(proj[:, PJ_U:PJ_U + S5_WIDTH], p['s5_a_re'], p['s5_a_im'], p['s5_log_dt'], p['s5_b_re'],
               p['s5_b_im'], p['s5_c_re'], p['s5_c_im'], p['s5_d'])
    merged = glu_gate(y, p['s5_w_glu'].astype(BF16), proj, PJ_GATE, tm=tm, tn=1024)

    ga = row(jnp.pad(p['mla_q_a_norm'], (0, CQ_W - MLA_Q_RANK)))
    kv_pad = (CKV_LO, CKV_W - CKV_LO - MLA_KV_RANK)
    gkv = row(jnp.pad(p['mla_kv_norm'], kv_pad))
    gq = row(jnp.pad(p['mla_q_norm'], (0, MLA_QK_PAD - MLA_QK)))
    gkn = row(p['mla_k_norm'][:MLA_NOPE])
    gkr = row(jnp.pad(p['mla_k_norm'][MLA_NOPE:], (0, 128 - MLA_ROPE)))
    wkv = jnp.pad(p['mla_w_kv_b'], (kv_pad, (0, 0))).astype(BF16)
    q, k, v = mla_prep(proj, pos_col, ga, gkv, gq, gkn, gkr, invf,
                       _repack_w_q_b(p['mla_w_q_b']), wkv, tm=min(256, seq))
    o = flash_attention(q, k, v, pos_col, pos_row, t=min(512, seq))
    merged = gate_add(o, p['mla_w_o'].astype(BF16), proj, PJ_GATE + D_MODEL, merged, tm=tm, tn=1024)

    km, vm = mem_kv(mem, row(p['mem_norm']), p['mem_w_kv'].astype(BF16), row(p['mem_k_norm']))
    om = mem_attention(proj, km, vm, row(p['mem_q_norm']), tm=min(512, seq))
    merged = gate_add(om, p['mem_w_o'].astype(BF16), proj, PJ_GATE + 2 * D_MODEL, merged, tm=tm, tn=1024)

    return resid_matmul(merged, p['w_out'].astype(BF16), x, tm=tm, tn=1024)


_LAYER_KEYS = ('norm_mix', 'w_in', 's5_a_re', 's5_a_im', 's5_log_dt', 's5_b_re', 's5_b_im', 's5_c_re',
               's5_c_im', 's5_d', 's5_w_glu', 'mla_q_a_norm', 'mla_w_q_b', 'mla_kv_norm', 'mla_w_kv_b',
               'mla_q_norm', 'mla_k_norm', 'mla_w_o', 'mem_norm', 'mem_w_kv', 'mem_q_norm', 'mem_k_norm',
               'mem_w_o', 'w_out')


def kernel(x, mem, positions, norm_mix, w_in, s5_a_re, s5_a_im, s5_log_dt, s5_b_re, s5_b_im, s5_c_re, s5_c_im, s5_d, s5_w_glu, mla_q_a_norm, mla_w_q_b, mla_kv_norm, mla_w_kv_b, mla_q_norm, mla_k_norm, mla_w_o, mem_norm, mem_w_kv, mem_q_norm, mem_k_norm, mem_w_o, w_out, norm_ffn, ffn_w_gate_up, ffn_w_down, moe_router, moe_router_b, moe_w_gate_up, moe_w_down):
    stacked = dict(zip(_LAYER_KEYS, (norm_mix, w_in, s5_a_re, s5_a_im, s5_log_dt, s5_b_re, s5_b_im, s5_c_re,
                                     s5_c_im, s5_d, s5_w_glu, mla_q_a_norm, mla_w_q_b, mla_kv_norm,
                                     mla_w_kv_b, mla_q_norm, mla_k_norm, mla_w_o, mem_norm, mem_w_kv,
                                     mem_q_norm, mem_k_norm, mem_w_o, w_out)))
    bsz, seq, d = x.shape
    depth = norm_mix.shape[0]
    half = MLA_ROPE // 2
    inv_freq = ROPE_THETA ** (-jnp.arange(0, MLA_ROPE, 2, dtype=F32) / MLA_ROPE)
    invf = jnp.concatenate([inv_freq, inv_freq, jnp.zeros((128 - 2 * half,), F32)])[None, :]
    moe_gu = moe_w_gate_up.reshape((-1,) + moe_w_gate_up.shape[2:])
    moe_dn = moe_w_down.reshape((-1,) + moe_w_down.shape[2:])

    outs = []
    for b in range(bsz):
        xb = x[b]
        pos_col = positions[b][:, None]
        pos_row = positions[b][None, :]
        for l in range(depth):
            p = {key: val[l] for key, val in stacked.items()}
            xb = mixer_layer(xb, mem[b], pos_col, pos_row, invf, p)
            gf = norm_ffn[l][None, :]
            if l % 2 == 0:
                xb = dense_ffn(xb, gf, ffn_w_gate_up[l // 2].astype(BF16), ffn_w_down[l // 2].astype(BF16),
                               tm=min(512, seq), tf=512)
            else:
                xb = moe_layer(xb, gf, moe_router[l // 2], moe_router_b[l // 2], moe_gu, moe_dn,
                               (l // 2) * N_EXPERTS, tm=min(1024, TOP_K * seq), tf=256)
        outs.append(xb)
    return jnp.stack(outs, axis=0)
```

```python
import functools
import math

import jax
import jax.numpy as jnp
from jax import lax
from jax.experimental import pallas as pl
from jax.experimental.pallas import tpu as pltpu

F32 = jnp.float32
BF16 = jnp.bfloat16
HIGHEST = lax.Precision.HIGHEST

D_MODEL = 2048
N_MEM = 256
S5_WIDTH = 1024
S5_GROUP = 16
S5_GROUPS = S5_WIDTH // S5_GROUP
S5_STATE = 64
S5_CHUNK = 64
S5_CK = S5_CHUNK * S5_GROUP
MLA_HEADS = 16
MLA_Q_RANK = 448
MLA_KV_RANK = 512
MLA_NOPE = 128
MLA_ROPE = 64
MLA_QK = MLA_NOPE + MLA_ROPE
MLA_V = 128
MLA_QK_PAD = 256
MLA_VP = 256
ROPE_THETA = 10000.0
MEM_HEADS = 4
MEM_HEAD_DIM = 256
MEM_WIDTH = MEM_HEADS * MEM_HEAD_DIM
N_BRANCH = 3
D_FF = 7168
N_EXPERTS = 8
TOP_K = 2
EPS = 1e-6
NEG_INF = -1e30

PJ_U = 0
PJ_MLA = 1024
PJ_MQ = 2048
PJ_GATE = 3072
MLA_W = MLA_Q_RANK + MLA_KV_RANK + MLA_ROPE
CQ_W = 512
CKV_OFF = 384
CKV_W = MLA_W - CKV_OFF
CKV_LO = MLA_Q_RANK - CKV_OFF
KR_OFF = MLA_W - 128

VMEM_LIMIT = 52 * 1024 * 1024


def _params(sem):
    return pltpu.CompilerParams(dimension_semantics=sem, vmem_limit_bytes=VMEM_LIMIT)


def _rms(x, g):
    r = lax.rsqrt(jnp.mean(x * x, axis=-1, keepdims=True) + EPS)
    return x * r * g


def _onehot_dot(a, b):
    if a.dtype == BF16:
        hi = b.astype(BF16)
        lo = (b - hi.astype(F32)).astype(BF16)
        return (jnp.dot(a, hi, preferred_element_type=F32) + jnp.dot(a, lo, preferred_element_type=F32))
    hi = a.astype(BF16)
    lo = (a - hi.astype(F32)).astype(BF16)
    return (jnp.dot(hi, b, preferred_element_type=F32) + jnp.dot(lo, b, preferred_element_type=F32))


def _norm_mm_kernel(x_ref, g_ref, w_ref, o_ref, h_ref):
    @pl.when(pl.program_id(1) == 0)
    def _():
        h_ref[...] = _rms(x_ref[...], g_ref[...]).astype(BF16)

    o_ref[...] = jnp.dot(h_ref[...], w_ref[...], preferred_element_type=F32).astype(o_ref.dtype)


def norm_matmul(x, g, w, *, tm, tn, out_dtype=BF16):
    m, k = x.shape
    n = w.shape[1]
    return pl.pallas_call(
        _norm_mm_kernel,
        grid=(m // tm, n // tn),
        in_specs=[pl.BlockSpec((tm, k), lambda i, j: (i, 0)),
                  pl.BlockSpec((1, k), lambda i, j: (0, 0)),
                  pl.BlockSpec((k, tn), lambda i, j: (0, j))],
        out_specs=pl.BlockSpec((tm, tn), lambda i, j: (i, j)),
        out_shape=jax.ShapeDtypeStruct((m, n), out_dtype),
        scratch_shapes=[pltpu.VMEM((tm, k), BF16)],
        compiler_params=_params(("parallel", "arbitrary")),
    )(x, g, w)


def _glu_gate_kernel(a_ref, wa_ref, wb_ref, gate_ref, o_ref):
    a = a_ref[...]
    ya = jnp.dot(a, wa_ref[...], preferred_element_type=F32)
    yb = jnp.dot(a, wb_ref[...], preferred_element_type=F32)
    g = jax.nn.sigmoid(gate_ref[...].astype(F32))
    o_ref[...] = (g * (ya * jax.nn.sigmoid(yb))).astype(o_ref.dtype)


def glu_gate(a, w, proj, gate_col, *, tm, tn):
    m, k = a.shape
    n = w.shape[1] // 2
    nb = n // tn
    gb = gate_col // tn
    return pl.pallas_call(
        _glu_gate_kernel,
        grid=(m // tm, nb),
        in_specs=[pl.BlockSpec((tm, k), lambda i, j: (i, 0)),
                  pl.BlockSpec((k, tn), lambda i, j: (0, j)),
                  pl.BlockSpec((k, tn), lambda i, j: (0, j + nb)),
                  pl.BlockSpec((tm, tn), lambda i, j: (i, j + gb))],
        out_specs=pl.BlockSpec((tm, tn), lambda i, j: (i, j)),
        out_shape=jax.ShapeDtypeStruct((m, n), BF16),
        compiler_params=_params(("parallel", "parallel")),
    )(a, w, w, proj)


def _gate_add_kernel(a_ref, w_ref, gate_ref, prev_ref, o_ref):
    y = jnp.dot(a_ref[...], w_ref[...], preferred_element_type=F32)
    g = jax.nn.sigmoid(gate_ref[...].astype(F32))
    o_ref[...] = (prev_ref[...].astype(F32) + g * y).astype(o_ref.dtype)


def gate_add(a, w, proj, gate_col, prev, *, tm, tn):
    m, k = a.shape
    n = w.shape[1]
    gb = gate_col // tn
    return pl.pallas_call(
        _gate_add_kernel,
        grid=(m // tm, n // tn),
        in_specs=[pl.BlockSpec((tm, k), lambda i, j: (i, 0)),
                  pl.BlockSpec((k, tn), lambda i, j: (0, j)),
                  pl.BlockSpec((tm, tn), lambda i, j: (i, j + gb)),
                  pl.BlockSpec((tm, tn), lambda i, j: (i, j))],
        out_specs=pl.BlockSpec((tm, tn), lambda i, j: (i, j)),
        out_shape=jax.ShapeDtypeStruct((m, n), BF16),
        compiler_params=_params(("parallel", "parallel")),
    )(a, w, proj, prev)


def _resid_mm_kernel(a_ref, w_ref, x_ref, o_ref):
    o_ref[...] = x_ref[...] + jnp.dot(a_ref[...], w_ref[...], preferred_element_type=F32)


def resid_matmul(a, w, x, *, tm, tn):
    m, k = a.shape
    n = w.shape[1]
    return pl.pallas_call(
        _resid_mm_kernel,
        grid=(m // tm, n // tn),
        in_specs=[pl.BlockSpec((tm, k), lambda i, j: (i, 0)),
                  pl.BlockSpec((k, tn), lambda i, j: (0, j)),
                  pl.BlockSpec((tm, tn), lambda i, j: (i, j))],
        out_specs=pl.BlockSpec((tm, tn), lambda i, j: (i, j)),
        out_shape=jax.ShapeDtypeStruct((m, n), F32),
        compiler_params=_params(("parallel", "parallel")),
    )(a, w, x)


def _s5_kernel(u_ref, rowp_ref, colp_ref, bt_ref, ct_ref, y_ref, toep_ref):
    t_chunk = S5_CHUNK
    n_chunks = u_ref.shape[1]
    p2 = 2 * S5_STATE

    rowp = rowp_ref[0]
    are_r, aim_r = rowp[0:1], rowp[1:2]
    dt_r = jnp.exp(rowp[2:3])
    colp = colp_ref[0]
    are_c, aim_c = colp[:, 0:1], colp[:, 1:2]
    dt_c = jnp.exp(colp[:, 2:3])
    d_c = colp[0:S5_GROUP, 3:4]

    row_q = lax.broadcasted_iota(jnp.int32, (p2, S5_CK), 0)
    lane_q = lax.broadcasted_iota(jnp.int32, (1, p2), 1)

    tau = lax.broadcasted_iota(jnp.int32, (p2, p2), 1).astype(F32)
    row_pp = lax.broadcasted_iota(jnp.int32, (p2, p2), 0)
    mag = jnp.exp(tau * dt_c * are_c)
    ang = tau * dt_c * aim_c
    pw = mag * jnp.where(row_pp < S5_STATE, jnp.cos(ang), jnp.sin(ang))

    l_t = lax.broadcasted_iota(jnp.int32, (p2, S5_CK), 1) // S5_GROUP
    e_tau = (l_t == row_q).astype(BF16)
    e_tau1 = (l_t + 1 == row_q).astype(BF16)
    l_i = lax.broadcasted_iota(jnp.int32, (S5_GROUP, S5_CK), 1) % S5_GROUP
    e_ch = (l_i == lax.broadcasted_iota(jnp.int32, (S5_GROUP, S5_CK), 0)).astype(BF16)

    def swap_halves(x):
        return jnp.concatenate([x[S5_STATE:], x[:S5_STATE]], axis=0)

    x1 = _onehot_dot(pw, e_tau)
    x1s = _onehot_dot(pw, e_tau1)
    ct = ct_ref[0]
    cta = _onehot_dot(ct[0], e_ch)
    ctb = _onehot_dot(ct[1], e_ch)
    ctb = jnp.where(row_q < S5_STATE, -ctb, ctb)
    z = cta * x1 + ctb * swap_halves(x1)
    zs = cta * x1s + ctb * swap_halves(x1s)
    wt = jnp.where(row_q < S5_STATE, zs, -zs)

    lam_mag = jnp.exp(dt_r * are_r)
    lam_re = lam_mag * jnp.cos(dt_r * aim_r)
    lam_im = lam_mag * jnp.sin(dt_r * aim_r)
    den = are_r * are_r + aim_r * aim_r
    n_re = lam_re - 1.0
    f_re = (n_re * are_r + lam_im * aim_r) / den
    f_im = (lam_im * are_r - n_re * aim_r) / den
    bt = bt_ref[0]
    bb_re = f_re * bt[0] - f_im * bt[1]
    bb_im = f_re * bt[1] + f_im * bt[0]
    first = lane_q < S5_STATE

    kt = jnp.dot(jnp.where(first, bb_re, -bb_im), z, precision=HIGHEST, preferred_element_type=F32)
    lane_k = lax.broadcasted_iota(jnp.int32, (S5_GROUP, S5_CK), 1)
    row_k = lax.broadcasted_iota(jnp.int32, (S5_GROUP, S5_CK), 0)
    kt = kt + jnp.where(lane_k == row_k, d_c, 0.0)
    for s in range(t_chunk):
        off = S5_GROUP * s
        blk = kt if s == 0 else jnp.where(lane_k >= off, pltpu.roll(kt, off, 1), 0.0)
        toep_ref[off:off + S5_GROUP, :] = blk.astype(BF16)

    e_s = (t_chunk - 1 - lax.broadcasted_iota(jnp.int32, (t_chunk, p2), 0)).astype(F32)
    pt_mag = jnp.exp(e_s * dt_r * are_r)
    pt_ang = e_s * dt_r * aim_r
    pt_re = pt_mag * jnp.cos(pt_ang)
    pt_im = pt_mag * jnp.sin(pt_ang)
    r_s = lax.broadcasted_iota(jnp.int32, (S5_CK, t_chunk), 0) // S5_GROUP
    e_rows = (r_s == lax.broadcasted_iota(jnp.int32, (S5_CK, t_chunk), 1)).astype(BF16)
    r_j = lax.broadcasted_iota(jnp.int32, (S5_CK, S5_GROUP), 0) % S5_GROUP
    e_rowj = (r_j == lax.broadcasted_iota(jnp.int32, (S5_CK, S5_GROUP), 1)).astype(BF16)
    gt = (_onehot_dot(e_rows, pt_re) * _onehot_dot(e_rowj, jnp.where(first, bb_re, bb_im))
          + _onehot_dot(e_rows, pt_im) * _onehot_dot(e_rowj, jnp.where(first, -bb_im, bb_re)))

    u = u_ref[0]
    h = jnp.dot(u, gt.astype(BF16), preferred_element_type=F32)

    k_row = lax.broadcasted_iota(jnp.int32, (8, p2), 0)
    m_pow = (t_chunk * (1 << k_row)).astype(F32)
    m_mag = jnp.exp(m_pow * dt_r * are_r)
    m_ang = m_pow * dt_r * aim_r
    m_re = m_mag * jnp.cos(m_ang)
    m_im = m_mag * jnp.sin(m_ang)
    row_c = lax.broadcasted_iota(jnp.int32, (n_chunks, p2), 0)
    k = 0
    while (1 << k) < n_chunks:
        sh = 1 << k
        s_prev = jnp.where(row_c >= sh, pltpu.roll(h, sh, 0), 0.0)
        s_swap = pltpu.roll(s_prev, S5_STATE, 1)
        mb = jnp.where(first, -m_im[k:k + 1], m_im[k:k + 1])
        h = h + m_re[k:k + 1] * s_prev + mb * s_swap
        k += 1
    h_prev = jnp.where(row_c >= 1, pltpu.roll(h, 1, 0), 0.0)

    y = (jnp.dot(u, toep_ref[...], preferred_element_type=F32)
         + jnp.dot(h_prev.astype(BF16), wt.astype(BF16), preferred_element_type=F32))
    y_ref[0] = jax.nn.gelu(y).astype(y_ref.dtype)


def s5_mix(u, a_re, a_im, log_dt, b_re, b_im, c_re, c_im, d):
    seq = u.shape[0]
    n_chunks = seq // S5_CHUNK
    g = S5_GROUPS
    ug = u.reshape(n_chunks, S5_CHUNK, g, S5_GROUP).transpose(2, 0, 1, 3).reshape(g, n_chunks, S5_CK)

    dup = lambda v: jnp.concatenate([v, v], axis=-1)
    zeros = jnp.zeros((g, 2 * S5_STATE), F32)
    ldt = jnp.broadcast_to(log_dt[:, None], (g, 2 * S5_STATE))
    rowp = jnp.stack([dup(a_re), dup(a_im), ldt] + [zeros] * 5, axis=1)
    d_pad = jnp.pad(d, ((0, 0), (0, 2 * S5_STATE - S5_GROUP)))
    colp = jnp.stack([dup(a_re), dup(a_im), ldt, d_pad] + [zeros] * 4, axis=2)
    bt = jnp.stack([dup(b_re.transpose(0, 2, 1)), dup(b_im.transpose(0, 2, 1))], axis=1)
    ctr, cti = c_re.transpose(0, 2, 1), c_im.transpose(0, 2, 1)
    ct = jnp.stack([jnp.concatenate([ctr, ctr], 1), jnp.concatenate([cti, cti], 1)], axis=1)

    yg = pl.pallas_call(
        _s5_kernel,
        grid=(g,),
        in_specs=[pl.BlockSpec((1, n_chunks, S5_CK), lambda i: (i, 0, 0)),
                  pl.BlockSpec((1, 8, 2 * S5_STATE), lambda i: (i, 0, 0)),
                  pl.BlockSpec((1, 2 * S5_STATE, 8), lambda i: (i, 0, 0)),
                  pl.BlockSpec((1, 2, S5_GROUP, 2 * S5_STATE), lambda i: (i, 0, 0, 0)),
                  pl.BlockSpec((1, 2, 2 * S5_STATE, S5_GROUP), lambda i: (i, 0, 0, 0))],
        out_specs=pl.BlockSpec((1, n_chunks, S5_CK), lambda i: (i, 0, 0)),
        out_shape=jax.ShapeDtypeStruct((g, n_chunks, S5_CK), BF16),
        scratch_shapes=[pltpu.VMEM((S5_CK, S5_CK), BF16)],
        compiler_params=_params(("parallel",)),
    )(ug, rowp, colp, bt, ct)
    return yg.reshape(g, n_chunks, S5_CHUNK, S5_GROUP).transpose(1, 2, 0, 3).reshape(seq, S5_WIDTH)


def _mla_prep_kernel(mla_ref, pos_ref, ga_ref, gkv_ref, gq_ref, gkn_ref, gkr_ref,
                     invf_ref, wq_ref, wkv_ref, q_ref, k_ref, v_ref):
    cq = mla_ref[:, 0:CQ_W].astype(F32)
    lane = lax.broadcasted_iota(jnp.int32, cq.shape, 1)
    ssq = jnp.sum(jnp.where(lane < MLA_Q_RANK, cq * cq, 0.0), axis=-1, keepdims=True)
    hq = (cq * lax.rsqrt(ssq * (1.0 / MLA_Q_RANK) + EPS) * ga_ref[...]).astype(BF16)
    qf = jnp.dot(hq, wq_ref[...], preferred_element_type=F32)
    ckv = mla_ref[:, CKV_OFF:].astype(F32)
    lane_kv = lax.broadcasted_iota(jnp.int32, ckv.shape, 1)
    in_kv = jnp.logical_and(lane_kv >= CKV_LO, lane_kv < CKV_LO + MLA_KV_RANK)
    ssq_kv = jnp.sum(jnp.where(in_kv, ckv * ckv, 0.0), axis=-1, keepdims=True)
    hkv = (ckv * lax.rsqrt(ssq_kv * (1.0 / MLA_KV_RANK) + EPS) * gkv_ref[...]).astype(BF16)
    kvf = jnp.dot(hkv, wkv_ref[...], preferred_element_type=F32)

    l128 = lax.broadcasted_iota(jnp.int32, (cq.shape[0], 128), 1)
    half = MLA_ROPE // 2
    ang = pos_ref[...].astype(F32) * invf_ref[...]
    cosv = jnp.cos(ang)
    sinv = jnp.sin(ang)
    sgn_sin = jnp.where(l128 < half, -sinv, jnp.where(l128 < MLA_ROPE, sinv, 0.0))

    def rope(t):
        sw = jnp.where(l128 < half, pltpu.roll(t, 128 - half, 1), pltpu.roll(t, half, 1))
        return t * cosv + sw * sgn_sin

    kr = jnp.where(l128 < MLA_ROPE, pltpu.roll(mla_ref[:, KR_OFF:].astype(F32), MLA_ROPE, 1), 0.0)
    kr_ssq = jnp.sum(kr * kr, axis=-1, keepdims=True)
    kr_rot = rope(kr * gkr_ref[...])

    gq = gq_ref[...]
    gkn = gkn_ref[...]
    scale = MLA_QK ** -0.5
    inv_qk = 1.0 / MLA_QK
    ones_col = jnp.where(l128 == 0, 1.0, 0.0).astype(BF16)
    for h in range(MLA_HEADS):
        qh = qf[:, MLA_QK_PAD * h:MLA_QK_PAD * (h + 1)]
        rq = lax.rsqrt(jnp.sum(qh * qh, axis=-1, keepdims=True) * inv_qk + EPS)
        qn = qh * rq * gq
        q_ref[h] = (jnp.concatenate([qn[:, :MLA_NOPE], rope(qn[:, MLA_NOPE:])], axis=1) * scale).astype(BF16)
        kn = kvf[:, MLA_QK_PAD * h:MLA_QK_PAD * h + MLA_NOPE]
        rk = lax.rsqrt((jnp.sum(kn * kn, axis=-1, keepdims=True) + kr_ssq) * inv_qk + EPS)
        k_ref[h] = jnp.concatenate([kn * rk * gkn, kr_rot * rk], axis=1).astype(BF16)
        vh = kvf[:, MLA_QK_PAD * h + MLA_NOPE:MLA_QK_PAD * (h + 1)].astype(BF16)
        v_ref[h] = jnp.concatenate([vh, ones_col], axis=1)


def mla_prep(proj, pos_col, ga, gkv, gq, gkn, gkr, invf, wq, wkv, *, tm):
    seq = proj.shape[0]
    h = MLA_HEADS
    full = lambda shape: pl.BlockSpec(shape, lambda i: (0,) * len(shape))
    return pl.pallas_call(
        _mla_prep_kernel,
        grid=(seq // tm,),
        in_specs=[pl.BlockSpec((tm, MLA_W), lambda i: (i, PJ_MLA // MLA_W)),
                  pl.BlockSpec((tm, 1), lambda i: (i, 0)),
                  full((1, CQ_W)), full((1, CKV_W)), full((1, MLA_QK_PAD)), full((1, 128)), full((1, 128)),
                  full((1, 128)), full((CQ_W, h * MLA_QK_PAD)), full((CKV_W, h * MLA_QK_PAD))],
        out_specs=[pl.BlockSpec((h, tm, MLA_QK_PAD), lambda i: (0, i, 0)),
                   pl.BlockSpec((h, tm, MLA_QK_PAD), lambda i: (0, i, 0)),
                   pl.BlockSpec((h, tm, MLA_VP), lambda i: (0, i, 0))],
        out_shape=[jax.ShapeDtypeStruct((h, seq, MLA_QK_PAD), BF16),
                   jax.ShapeDtypeStruct((h, seq, MLA_QK_PAD), BF16),
                   jax.ShapeDtypeStruct((h, seq, MLA_VP), BF16)],
        compiler_params=_params(("parallel",)),
    )(proj, pos_col, ga, gkv, gq, gkn, gkr, invf, wq, wkv)


def _flash_kernel(q_ref, k_ref, v_ref, pq_ref, pk_ref, o_ref, sa, sb, pa, pb, ala, alb, m_ref, acc_ref, *, t):
    qi = pl.program_id(1)
    nt = (((1,), (1,)), ((), ()))

    def blk(j):
        return pl.ds(pl.multiple_of(j * t, t), t)

    def stage_a(j, s_out):
        s_out[...] = lax.dot_general(q_ref[0], k_ref[0, blk(j), :], nt, preferred_element_type=F32)

    def stage_b(j, s_in, p_out, al_out, masked):
        s = s_in[...]
        if masked:
            s = jnp.where(pk_ref[:, blk(j)] <= pq_ref[...], s, NEG_INF)
        m_prev = m_ref[...]
        m_new = jnp.maximum(m_prev, jnp.max(s, axis=-1, keepdims=True))
        al_out[...] = jnp.exp(m_prev - m_new)
        p_out[...] = jnp.exp((s - jnp.concatenate([m_new] * (t // 128), axis=1)).astype(BF16))
        m_ref[...] = m_new

    def stage_c(j, p_in, al_in):
        al = al_in[...]
        acc_ref[...] = (jnp.concatenate([al] * (MLA_VP // 128), axis=1) * acc_ref[...]
                        + jnp.dot(p_in[...], v_ref[0, blk(j), :], preferred_element_type=F32))

    m_ref[...] = jnp.full(m_ref.shape, NEG_INF, F32)
    acc_ref[...] = jnp.zeros(acc_ref.shape, F32)
    pb[...] = jnp.zeros(pb.shape, BF16)
    alb[...] = jnp.ones(alb.shape, F32)
    stage_a(0, sa)

    def pair_body(j):
        stage_a(j + 1, sb)
        stage_b(j, sa, pa, ala, False)
        stage_c(jnp.maximum(j - 1, 0), pb, alb)
        stage_a(j + 2, sa)
        stage_b(j + 1, sb, pb, alb, False)
        stage_c(j, pa, ala)

    def quad(i, c):
        pair_body(4 * i)
        pair_body(4 * i + 2)
        return c

    lax.fori_loop(0, qi // 4, quad, 0)

    @pl.when((qi // 2) % 2 == 1)
    def _():
        pair_body(4 * (qi // 4))

    @pl.when(qi % 2 == 0)
    def _():
        stage_b(qi, sa, pa, ala, True)
        stage_c(jnp.maximum(qi - 1, 0), pb, alb)
        stage_c(qi, pa, ala)

    @pl.when(qi % 2 == 1)
    def _():
        stage_a(qi, sb)
        stage_b(qi - 1, sa, pa, ala, False)
        stage_c(jnp.maximum(qi - 2, 0), pb, alb)
        stage_b(qi, sb, pb, alb, True)
        stage_c(qi - 1, pa, ala)
        stage_c(qi, pb, alb)

    acc = acc_ref[...]
    o_ref[...] = (acc[:, :MLA_V] / acc[:, MLA_V:MLA_V + 1]).astype(o_ref.dtype)


def flash_attention(q, k, v, pos_col, pos_row, *, t):
    h, seq, _ = q.shape
    return pl.pallas_call(
        functools.partial(_flash_kernel, t=t),
        grid=(h, seq // t),
        in_specs=[pl.BlockSpec((1, t, MLA_QK_PAD), lambda hh, i: (hh, i, 0)),
                  pl.BlockSpec((1, seq, MLA_QK_PAD), lambda hh, i: (hh, 0, 0)),
                  pl.BlockSpec((1, seq, MLA_VP), lambda hh, i: (hh, 0, 0)),
                  pl.BlockSpec((t, 1), lambda hh, i: (i, 0)),
                  pl.BlockSpec((1, seq), lambda hh, i: (0, 0))],
        out_specs=pl.BlockSpec((t, MLA_V), lambda hh, i: (i, hh)),
        out_shape=jax.ShapeDtypeStruct((seq, h * MLA_V), BF16),
        scratch_shapes=[pltpu.VMEM((t, t), F32), pltpu.VMEM((t, t), F32),
                        pltpu.VMEM((t, t), BF16), pltpu.VMEM((t, t), BF16),
                        pltpu.VMEM((t, 128), F32), pltpu.VMEM((t, 128), F32),
                        pltpu.VMEM((t, 128), F32), pltpu.VMEM((t, MLA_VP), F32)],
        compiler_params=_params(("parallel", "arbitrary")),
    )(q, k, v, pos_col, pos_row)


def _mem_kv_kernel(mem_ref, g_ref, w_ref, gk_ref, k_ref, v_ref):
    m = _rms(mem_ref[...], g_ref[...]).astype(BF16)
    kv = jnp.dot(m, w_ref[...], preferred_element_type=F32)
    for h in range(MEM_HEADS):
        kh = kv[:, MEM_HEAD_DIM * h:MEM_HEAD_DIM * (h + 1)]
        k_ref[:, MEM_HEAD_DIM * h:MEM_HEAD_DIM * (h + 1)] = _rms(kh, gk_ref[...]).astype(BF16)
    v_ref[...] = kv[:, MEM_WIDTH:].astype(BF16)


def mem_kv(mem, g, w, gk):
    n_mem = mem.shape[0]
    full = lambda shape: pl.BlockSpec(shape, lambda i: (0,) * len(shape))
    return pl.pallas_call(
        _mem_kv_kernel,
        grid=(1,),
        in_specs=[full(mem.shape), full(g.shape), full(w.shape), full(gk.shape)],
        out_specs=[full((n_mem, MEM_WIDTH)), full((n_mem, MEM_WIDTH))],
        out_shape=[jax.ShapeDtypeStruct((n_mem, MEM_WIDTH), BF16)] * 2,
        compiler_params=_params(("arbitrary",)),
    )(mem, g, w, gk)


def _mem_attn_kernel(q_ref, k_ref, v_ref, gq_ref, o_ref):
    scale = MEM_HEAD_DIM ** -0.5
    for h in range(MEM_HEADS):
        sl = slice(MEM_HEAD_DIM * h, MEM_HEAD_DIM * (h + 1))
        qh = (_rms(q_ref[:, sl].astype(F32), gq_ref[...]) * scale).astype(BF16)
        s = lax.dot_general(qh, k_ref[:, sl], (((1,), (1,)), ((), ())), preferred_element_type=F32)
        p = jnp.exp(s - jnp.max(s, axis=-1, keepdims=True))
        o = jnp.dot(p.astype(BF16), v_ref[:, sl], preferred_element_type=F32)
        o_ref[:, sl] = (o / jnp.sum(p, axis=-1, keepdims=True)).astype(o_ref.dtype)


def mem_attention(proj, k, v, gq, *, tm):
    seq = proj.shape[0]
    n_mem = k.shape[0]
    return pl.pallas_call(
        _mem_attn_kernel,
        grid=(seq // tm,),
        in_specs=[pl.BlockSpec((tm, MEM_WIDTH), lambda i: (i, PJ_MQ // MEM_WIDTH)),
                  pl.BlockSpec((n_mem, MEM_WIDTH), lambda i: (0, 0)),
                  pl.BlockSpec((n_mem, MEM_WIDTH), lambda i: (0, 0)),
                  pl.BlockSpec((1, MEM_HEAD_DIM), lambda i: (0, 0))],
        out_specs=pl.BlockSpec((tm, MEM_WIDTH), lambda i: (i, 0)),
        out_shape=jax.ShapeDtypeStruct((seq, MEM_WIDTH), BF16),
        compiler_params=_params(("parallel",)),
    )(proj, k, v, gq)


def _ffn_kernel(x_ref, g_ref, wg_ref, wu_ref, wd_ref, o_ref, h_ref, acc_ref):
    f = pl.program_id(1)

    @pl.when(f == 0)
    def _():
        h_ref[...] = _rms(x_ref[...], g_ref[...]).astype(BF16)
        acc_ref[...] = jnp.zeros_like(acc_ref)

    h = h_ref[...]
    gate = jnp.dot(h, wg_ref[...], preferred_element_type=F32)
    up = jnp.dot(h, wu_ref[...], preferred_element_type=F32)
    a = (jax.nn.silu(gate) * up).astype(BF16)
    acc_ref[...] += jnp.dot(a, wd_ref[...], preferred_element_type=F32)

    @pl.when(f == pl.num_programs(1) - 1)
    def _():
        o_ref[...] = x_ref[...] + acc_ref[...]


def dense_ffn(x, g, w_gate_up, w_down, *, tm, tf):
    m, d = x.shape
    nf = D_FF // tf
    return pl.pallas_call(
        _ffn_kernel,
        grid=(m // tm, nf),
        in_specs=[pl.BlockSpec((tm, d), lambda i, f: (i, 0)),
                  pl.BlockSpec((1, d), lambda i, f: (0, 0)),
                  pl.BlockSpec((d, tf), lambda i, f: (0, f)),
                  pl.BlockSpec((d, tf), lambda i, f: (0, f + nf)),
                  pl.BlockSpec((tf, d), lambda i, f: (f, 0))],
        out_specs=pl.BlockSpec((tm, d), lambda i, f: (i, 0)),
        out_shape=jax.ShapeDtypeStruct((m, d), F32),
        scratch_shapes=[pltpu.VMEM((tm, d), BF16), pltpu.VMEM((tm, d), F32)],
        compiler_params=_params(("parallel", "arbitrary")),
    )(x, g, w_gate_up, w_gate_up, w_down)


def _router_kernel(x_ref, g_ref, w_ref, b_ref, idx_ref, wgt_ref):
    h = _rms(x_ref[...], g_ref[...])
    logits = jnp.dot(h, w_ref[...], precision=HIGHEST, preferred_element_type=F32) + b_ref[...]
    lane = lax.broadcasted_iota(jnp.int32, logits.shape, 1)
    logits = jnp.where(lane < N_EXPERTS, logits, -jnp.inf)
    v1 = jnp.max(logits, axis=-1, keepdims=True)
    i1 = jnp.min(jnp.where(logits == v1, lane, 128), axis=-1, keepdims=True)
    rest = jnp.where(lane == i1, -jnp.inf, logits)
    v2 = jnp.max(rest, axis=-1, keepdims=True)
    i2 = jnp.min(jnp.where(rest == v2, lane, 128), axis=-1, keepdims=True)
    e2 = jnp.exp(v2 - v1)
    w1 = 1.0 / (1.0 + e2)
    w2 = e2 / (1.0 + e2)
    idx_ref[...] = jnp.where(lane == 0, i1, jnp.where(lane == 1, i2, 0))
    wgt_ref[...] = jnp.where(lane == 0, w1, jnp.where(lane == 1, w2, 0.0))


def moe_router(x, g, w_pad, b_pad, *, tm):
    m, d = x.shape
    return pl.pallas_call(
        _router_kernel,
        grid=(m // tm,),
        in_specs=[pl.BlockSpec((tm, d), lambda i: (i, 0)),
                  pl.BlockSpec((1, d), lambda i: (0, 0)),
                  pl.BlockSpec((d, 128), lambda i: (0, 0)),
                  pl.BlockSpec((1, 128), lambda i: (0, 0))],
        out_specs=[pl.BlockSpec((tm, 128), lambda i: (i, 0)),
                   pl.BlockSpec((tm, 128), lambda i: (i, 0))],
        out_shape=[jax.ShapeDtypeStruct((m, 128), jnp.int32),
                   jax.ShapeDtypeStruct((m, 128), F32)],
        compiler_params=_params(("parallel",)),
    )(x, g, w_pad, b_pad)


MOE_ROW_STEPS = 4


def _moe_ffn_kernel(te_ref, tr_ref, src_ref, x_hbm, g_ref, wg_ref, wu_ref, wd_ref, o_ref, xbuf, h_ref, sem):
    i = pl.program_id(0)
    f = pl.program_id(1)
    n_tiles = pl.num_programs(0)
    rows = tr_ref[i]
    tm = h_ref.shape[0]
    per_step = xbuf.shape[0] // pl.num_programs(1)

    def row_copy(idx, r):
        return pltpu.make_async_copy(x_hbm.at[pl.ds(idx, 1), :], xbuf.at[pl.ds(r, 1), :], sem)

    def start_rows(tile, step):
        for k in range(per_step):
            r = step * per_step + k
            row_copy(src_ref[tile * tm + jnp.minimum(r, tm - 1)], r).start()

    def wait_all():
        def wait_row(r, c):
            row_copy(0, r).wait()
            return c

        lax.fori_loop(0, xbuf.shape[0], wait_row, 0, unroll=4)

    is_first = i == 0
    is_last = i == n_tiles - 1
    nxt = jnp.minimum(i + 1, n_tiles - 1)
    prev_rows = tr_ref[jnp.maximum(i - 1, 0)]

    @pl.when(f == 0)
    def _():
        o_ref[...] = jnp.zeros_like(o_ref)

        @pl.when(jnp.logical_and(is_first, rows > 0))
        def _():
            def first(step, c):
                start_rows(0, step)
                return c

            lax.fori_loop(0, pl.num_programs(1), first, 0)

        @pl.when(jnp.where(is_first, rows, prev_rows) > 0)
        def _():
            wait_all()

        @pl.when(rows > 0)
        def _():
            h_ref[...] = _rms(xbuf[0:tm, :], g_ref[...]).astype(BF16)

    def run(n):
        start_rows(nxt, f)
        h = h_ref[0:n, :]
        gate = jnp.dot(h, wg_ref[0].astype(BF16), preferred_element_type=F32)
        up = jnp.dot(h, wu_ref[0].astype(BF16), preferred_element_type=F32)
        a = (jax.nn.silu(gate) * up).astype(BF16)
        o_ref[0:n, :] += jnp.dot(a, wd_ref[0].astype(BF16), preferred_element_type=F32)

    step = tm // MOE_ROW_STEPS
    for k in range(1, MOE_ROW_STEPS + 1):
        @pl.when(jnp.logical_and(rows > (k - 1) * step, rows <= k * step))
        def _(n=k * step):
            run(n)

    @pl.when(jnp.logical_and(jnp.logical_and(is_last, f == pl.num_programs(1) - 1), rows > 0))
    def _():
        wait_all()


def moe_ffn(x, g, src, tile_expert, tile_rows, w_gate_up, w_down, *, tm, tf):
    d = x.shape[1]
    p = src.shape[0]
    nf = D_FF // tf
    last = nf - 1
    per_step = -(-tm // nf)
    while (nf * per_step) % 8:
        per_step += 1
    gather_rows = nf * per_step

    def fsel(i, f, tr):
        return jnp.where(tr[i] > 0, f, last)

    return pl.pallas_call(
        _moe_ffn_kernel,
        grid_spec=pltpu.PrefetchScalarGridSpec(
            num_scalar_prefetch=3,
            grid=(p // tm, nf),
            in_specs=[pl.BlockSpec(memory_space=pl.ANY),
                      pl.BlockSpec((1, d), lambda i, f, te, tr, src: (0, 0)),
                      pl.BlockSpec((1, d, tf), lambda i, f, te, tr, src: (te[i], 0, fsel(i, f, tr))),
                      pl.BlockSpec((1, d, tf), lambda i, f, te, tr, src: (te[i], 0, fsel(i, f, tr) + nf)),
                      pl.BlockSpec((1, tf, d), lambda i, f, te, tr, src: (te[i], fsel(i, f, tr), 0))],
            out_specs=pl.BlockSpec((tm, d), lambda i, f, te, tr, src: (i, 0)),
            scratch_shapes=[pltpu.VMEM((gather_rows, d), F32), pltpu.VMEM((tm, d), BF16),
                            pltpu.SemaphoreType.DMA(())]),
        out_shape=jax.ShapeDtypeStruct((p, d), F32),
        compiler_params=_params(("arbitrary", "arbitrary")),
    )(tile_expert, tile_rows, src, x, g, w_gate_up, w_gate_up, w_down)


def _combine_kernel(p0_ref, p1_ref, x_ref, w_ref, ys_hbm, o_ref, buf, sem, *, tt):
    base = pl.program_id(0) * tt

    def issue(j, c):
        r0 = p0_ref[base + j]
        r1 = p1_ref[base + j]
        pltpu.make_async_copy(ys_hbm.at[pl.ds(r0, 1), :], buf.at[0, pl.ds(j, 1), :], sem.at[0]).start()
        pltpu.make_async_copy(ys_hbm.at[pl.ds(r1, 1), :], buf.at[1, pl.ds(j, 1), :], sem.at[1]).start()
        return c

    lax.fori_loop(0, tt, issue, 0, unroll=8)

    def drain(j, c):
        pltpu.make_async_copy(ys_hbm.at[pl.ds(0, 1), :], buf.at[0, pl.ds(j, 1), :], sem.at[0]).wait()
        pltpu.make_async_copy(ys_hbm.at[pl.ds(0, 1), :], buf.at[1, pl.ds(j, 1), :], sem.at[1]).wait()
        return c

    lax.fori_loop(0, tt, drain, 0, unroll=8)
    w = w_ref[...]
    o_ref[...] = x_ref[...] + (w[:, 0:1] * buf[0] + w[:, 1:2] * buf[1])


def moe_combine(x, wgt, ys, pos0, pos1, *, tt):
    m, d = x.shape
    return pl.pallas_call(
        functools.partial(_combine_kernel, tt=tt),
        grid_spec=pltpu.PrefetchScalarGridSpec(
            num_scalar_prefetch=2,
            grid=(m // tt,),
            in_specs=[pl.BlockSpec((tt, d), lambda i, p0, p1: (i, 0)),
                      pl.BlockSpec((tt, 128), lambda i, p0, p1: (i, 0)),
                      pl.BlockSpec(memory_space=pl.ANY)],
            out_specs=pl.BlockSpec((tt, d), lambda i, p0, p1: (i, 0)),
            scratch_shapes=[pltpu.VMEM((2, tt, d), F32), pltpu.SemaphoreType.DMA((2,))]),
        out_shape=jax.ShapeDtypeStruct((m, d), F32),
        compiler_params=_params(("arbitrary",)),
    )(pos0, pos1, x, wgt, ys)


def moe_layer(x, g, router, router_b, w_gate_up, w_down, expert_base, *, tm, tf):
    seq, d = x.shape
    w_pad = jnp.pad(router, ((0, 0), (0, 128 - N_EXPERTS)))
    b_pad = jnp.pad(router_b, (0, 128 - N_EXPERTS))[None, :]
    idx, wgt = moe_router(x, g, w_pad, b_pad, tm=min(512, seq))

    e_flat = idx[:, :TOP_K].reshape(-1)
    onehot = (e_flat[:, None] == jnp.arange(N_EXPERTS, dtype=jnp.int32)[None, :]).astype(jnp.int32)
    rank = jnp.cumsum(onehot, axis=0) - onehot
    counts = jnp.sum(onehot, axis=0)
    tiles_per = (counts + tm - 1) // tm
    tile_ends = jnp.cumsum(tiles_per)
    tile_starts = tile_ends - tiles_per
    pos = jnp.sum(onehot * (tile_starts[None, :] * tm + rank), axis=1).astype(jnp.int32)
    n_tiles = TOP_K * seq // tm + N_EXPERTS
    p_total = n_tiles * tm
    tok = jnp.arange(TOP_K * seq, dtype=jnp.int32) // TOP_K
    src = jnp.zeros((p_total,), jnp.int32).at[pos].set(tok)
    tile_idx = jnp.arange(n_tiles, dtype=jnp.int32)
    tile_expert = jnp.minimum(jnp.sum((tile_idx[:, None] >= tile_ends[None, :]).astype(jnp.int32), axis=1),
                              N_EXPERTS - 1).astype(jnp.int32)
    valid = tile_idx < tile_ends[-1]
    remaining = counts[tile_expert] - (tile_idx - tile_starts[tile_expert]) * tm
    tile_rows = jnp.where(valid, jnp.minimum(remaining, tm), 0).astype(jnp.int32)
    tile_expert = jnp.where(valid, tile_expert, jnp.max(jnp.where(valid, tile_expert, 0))).astype(jnp.int32)

    ys = moe_ffn(x, g, src, tile_expert + expert_base, tile_rows, w_gate_up, w_down, tm=tm, tf=tf)
    pos2 = pos.reshape(seq, TOP_K)
    return moe_combine(x, wgt, ys, pos2[:, 0], pos2[:, 1], tt=min(256, seq))


def _repack_w_q_b(w):
    w = w.reshape(MLA_Q_RANK, MLA_HEADS, MLA_QK)
    w = jnp.pad(w, ((0, CQ_W - MLA_Q_RANK), (0, 0), (0, MLA_QK_PAD - MLA_QK)))
    return w.reshape(CQ_W, MLA_HEADS * MLA_QK_PAD).astype(BF16)


def mixer_layer(x, mem, pos_col, pos_row, invf, p):
    seq = x.shape[0]
    tm = min(1024, seq)
    row = lambda v: v[None, :]
    proj = norm_matmul(x, row(p['norm_mix']), p['w_in'].astype(BF16), tm=tm, tn=1024)

    y = s5_mix(proj[:, PJ_U:PJ_U + S5_WIDTH], p['s5_a_re'], p['s5_a_im'], p['s5_log_dt'], p['s5_b_re'],
               p['s5_b_im'], p['s5_c_re'], p['s5_c_im'], p['s5_d'])
    merged = glu_gate(y, p['s5_w_glu'].astype(BF16), proj, PJ_GATE, tm=tm, tn=1024)

    ga = row(jnp.pad(p['mla_q_a_norm'], (0, CQ_W - MLA_Q_RANK)))
    kv_pad = (CKV_LO, CKV_W - CKV_LO - MLA_KV_RANK)
    gkv = row(jnp.pad(p['mla_kv_norm'], kv_pad))
    gq = row(jnp.pad(p['mla_q_norm'], (0, MLA_QK_PAD - MLA_QK)))
    gkn = row(p['mla_k_norm'][:MLA_NOPE])
    gkr = row(jnp.pad(p['mla_k_norm'][MLA_NOPE:], (0, 128 - MLA_ROPE)))
    wkv = jnp.pad(p['mla_w_kv_b'], (kv_pad, (0, 0))).astype(BF16)
    q, k, v = mla_prep(proj, pos_col, ga, gkv, gq, gkn, gkr, invf,
                       _repack_w_q_b(p['mla_w_q_b']), wkv, tm=min(256, seq))
    o = flash_attention(q, k, v, pos_col, pos_row, t=min(512, seq))
    merged = gate_add(o, p['mla_w_o'].astype(BF16), proj, PJ_GATE + D_MODEL, merged, tm=tm, tn=1024)

    km, vm = mem_kv(mem, row(p['mem_norm']), p['mem_w_kv'].astype(BF16), row(p['mem_k_norm']))
    om = mem_attention(proj, km, vm, row(p['mem_q_norm']), tm=min(512, seq))
    merged = gate_add(om, p['mem_w_o'].astype(BF16), proj, PJ_GATE + 2 * D_MODEL, merged, tm=tm, tn=1024)

    return resid_matmul(merged, p['w_out'].astype(BF16), x, tm=tm, tn=1024)


_LAYER_KEYS = ('norm_mix', 'w_in', 's5_a_re', 's5_a_im', 's5_log_dt', 's5_b_re', 's5_b_im', 's5_c_re',
               's5_c_im', 's5_d', 's5_w_glu', 'mla_q_a_norm', 'mla_w_q_b', 'mla_kv_norm', 'mla_w_kv_b',
               'mla_q_norm', 'mla_k_norm', 'mla_w_o', 'mem_norm', 'mem_w_kv', 'mem_q_norm', 'mem_k_norm',
               'mem_w_o', 'w_out')


def kernel(x, mem, positions, norm_mix, w_in, s5_a_re, s5_a_im, s5_log_dt, s5_b_re, s5_b_im, s5_c_re, s5_c_im, s5_d, s5_w_glu, mla_q_a_norm, mla_w_q_b, mla_kv_norm, mla_w_kv_b, mla_q_norm, mla_k_norm, mla_w_o, mem_norm, mem_w_kv, mem_q_norm, mem_k_norm, mem_w_o, w_out, norm_ffn, ffn_w_gate_up, ffn_w_down, moe_router, moe_router_b, moe_w_gate_up, moe_w_down):
    stacked = dict(zip(_LAYER_KEYS, (norm_mix, w_in, s5_a_re, s5_a_im, s5_log_dt, s5_b_re, s5_b_im, s5_c_re,
                                     s5_c_im, s5_d, s5_w_glu, mla_q_a_norm, mla_w_q_b, mla_kv_norm,
                                     mla_w_kv_b, mla_q_norm, mla_k_norm, mla_w_o, mem_norm, mem_w_kv,
                                     mem_q_norm, mem_k_norm, mem_w_o, w_out)))
    bsz, seq, d = x.shape
    depth = norm_mix.shape[0]
    half = MLA_ROPE // 2
    inv_freq = ROPE_THETA ** (-jnp.arange(0, MLA_ROPE, 2, dtype=F32) / MLA_ROPE)
    invf = jnp.concatenate([inv_freq, inv_freq, jnp.zeros((128 - 2 * half,), F32)])[None, :]
    moe_gu = moe_w_gate_up.reshape((-1,) + moe_w_gate_up.shape[2:])
    moe_dn = moe_w_down.reshape((-1,) + moe_w_down.shape[2:])

    outs = []
    for b in range(bsz):
        xb = x[b]
        pos_col = positions[b][:, None]
        pos_row = positions[b][None, :]
        for l in range(depth):
            p = {key: val[l] for key, val in stacked.items()}
            xb = mixer_layer(xb, mem[b], pos_col, pos_row, invf, p)
            gf = norm_ffn[l][None, :]
            if l % 2 == 0:
                xb = dense_ffn(xb, gf, ffn_w_gate_up[l // 2].astype(BF16), ffn_w_down[l // 2].astype(BF16),
                               tm=min(512, seq), tf=512)
            else:
                xb = moe_layer(xb, gf, moe_router[l // 2], moe_router_b[l // 2], moe_gu, moe_dn,
                               (l // 2) * N_EXPERTS, tm=min(1024, TOP_K * seq), tf=256)
        outs.append(xb)
    return jnp.stack(outs, axis=0)
```

```python
import functools
import math

import jax
import jax.numpy as jnp
from jax import lax
from jax.experimental import pallas as pl
from jax.experimental.pallas import tpu as pltpu

F32 = jnp.float32
BF16 = jnp.bfloat16
HIGHEST = lax.Precision.HIGHEST

D_MODEL = 2048
N_MEM = 256
S5_WIDTH = 1024
S5_GROUP = 16
S5_GROUPS = S5_WIDTH // S5_GROUP
S5_STATE = 64
S5_CHUNK = 64
S5_CK = S5_CHUNK * S5_GROUP
MLA_HEADS = 16
MLA_Q_RANK = 448
MLA_KV_RANK = 512
MLA_NOPE = 128
MLA_ROPE = 64
MLA_QK = MLA_NOPE + MLA_ROPE
MLA_V = 128
MLA_QK_PAD = 256
MLA_VP = 256
ROPE_THETA = 10000.0
MEM_HEADS = 4
MEM_HEAD_DIM = 256
MEM_WIDTH = MEM_HEADS * MEM_HEAD_DIM
N_BRANCH = 3
D_FF = 7168
N_EXPERTS = 8
TOP_K = 2
EPS = 1e-6
NEG_INF = -1e30

PJ_U = 0
PJ_MLA = 1024
PJ_MQ = 2048
PJ_GATE = 3072
MLA_W = MLA_Q_RANK + MLA_KV_RANK + MLA_ROPE
CQ_W = 512
CKV_OFF = 384
CKV_W = MLA_W - CKV_OFF
CKV_LO = MLA_Q_RANK - CKV_OFF
KR_OFF = MLA_W - 128

VMEM_LIMIT = 52 * 1024 * 1024


def _params(sem):
    return pltpu.CompilerParams(dimension_semantics=sem, vmem_limit_bytes=VMEM_LIMIT)


def _rms(x, g):
    r = lax.rsqrt(jnp.mean(x * x, axis=-1, keepdims=True) + EPS)
    return x * r * g


def _onehot_dot(a, b):
    if a.dtype == BF16:
        hi = b.astype(BF16)
        lo = (b - hi.astype(F32)).astype(BF16)
        return (jnp.dot(a, hi, preferred_element_type=F32) + jnp.dot(a, lo, preferred_element_type=F32))
    hi = a.astype(BF16)
    lo = (a - hi.astype(F32)).astype(BF16)
    return (jnp.dot(hi, b, preferred_element_type=F32) + jnp.dot(lo, b, preferred_element_type=F32))


def _norm_mm_kernel(x_ref, g_ref, w_ref, o_ref, h_ref):
    @pl.when(pl.program_id(1) == 0)
    def _():
        h_ref[...] = _rms(x_ref[...], g_ref[...]).astype(BF16)

    o_ref[...] = jnp.dot(h_ref[...], w_ref[...], preferred_element_type=F32).astype(o_ref.dtype)


def norm_matmul(x, g, w, *, tm, tn, out_dtype=BF16):
    m, k = x.shape
    n = w.shape[1]
    return pl.pallas_call(
        _norm_mm_kernel,
        grid=(m // tm, n // tn),
        in_specs=[pl.BlockSpec((tm, k), lambda i, j: (i, 0)),
                  pl.BlockSpec((1, k), lambda i, j: (0, 0)),
                  pl.BlockSpec((k, tn), lambda i, j: (0, j))],
        out_specs=pl.BlockSpec((tm, tn), lambda i, j: (i, j)),
        out_shape=jax.ShapeDtypeStruct((m, n), out_dtype),
        scratch_shapes=[pltpu.VMEM((tm, k), BF16)],
        compiler_params=_params(("parallel", "arbitrary")),
    )(x, g, w)


def _merge_kernel(y_ref, o_ref, om_ref, wa_ref, wb_ref, wo_ref, wm_ref, g0_ref, g1_ref, g2_ref, out_ref):
    y = y_ref[...]
    s5 = (jnp.dot(y, wa_ref[...], preferred_element_type=F32)
          * jax.nn.sigmoid(jnp.dot(y, wb_ref[...], preferred_element_type=F32)))
    mla = jnp.dot(o_ref[...], wo_ref[...], preferred_element_type=F32)
    mem = jnp.dot(om_ref[...], wm_ref[...], preferred_element_type=F32)
    gate = lambda r: jax.nn.sigmoid(r[...].astype(F32))
    out_ref[...] = (gate(g0_ref) * s5 + gate(g1_ref) * mla + gate(g2_ref) * mem).astype(out_ref.dtype)


def merge_branches(y, o, om, w_glu, w_o, w_mo, proj, *, tm, tn):
    m = y.shape[0]
    n = w_o.shape[1]
    nb = n // tn
    gb = PJ_GATE // tn
    lhs = lambda a: pl.BlockSpec((tm, a.shape[1]), lambda i, j: (i, 0))
    col = lambda w, off: pl.BlockSpec((w.shape[0], tn), lambda i, j: (0, j + off))
    gate = lambda b: pl.BlockSpec((tm, tn), lambda i, j: (i, gb + b * nb + j))
    return pl.pallas_call(
        _merge_kernel,
        grid=(m // tm, nb),
        in_specs=[lhs(y), lhs(o), lhs(om), col(w_glu, 0), col(w_glu, nb), col(w_o, 0), col(w_mo, 0),
                  gate(0), gate(1), gate(2)],
        out_specs=pl.BlockSpec((tm, tn), lambda i, j: (i, j)),
        out_shape=jax.ShapeDtypeStruct((m, n), BF16),
        compiler_params=_params(("parallel", "parallel")),
    )(y, o, om, w_glu, w_glu, w_o, w_mo, proj, proj, proj)


def _resid_mm_kernel(a_ref, w_ref, x_ref, o_ref):
    o_ref[...] = x_ref[...] + jnp.dot(a_ref[...], w_ref[...], preferred_element_type=F32)


def resid_matmul(a, w, x, *, tm, tn):
    m, k = a.shape
    n = w.shape[1]
    return pl.pallas_call(
        _resid_mm_kernel,
        grid=(m // tm, n // tn),
        in_specs=[pl.BlockSpec((tm, k), lambda i, j: (i, 0)),
                  pl.BlockSpec((k, tn), lambda i, j: (0, j)),
                  pl.BlockSpec((tm, tn), lambda i, j: (i, j))],
        out_specs=pl.BlockSpec((tm, tn), lambda i, j: (i, j)),
        out_shape=jax.ShapeDtypeStruct((m, n), F32),
        compiler_params=_params(("parallel", "parallel")),
    )(a, w, x)


def _s5_kernel(u_ref, rowp_ref, colp_ref, bt_ref, ct_ref, y_ref, toep_ref):
    t_chunk = S5_CHUNK
    n_chunks = u_ref.shape[1]
    p2 = 2 * S5_STATE

    rowp = rowp_ref[0]
    are_r, aim_r = rowp[0:1], rowp[1:2]
    dt_r = jnp.exp(rowp[2:3])
    colp = colp_ref[0]
    are_c, aim_c = colp[:, 0:1], colp[:, 1:2]
    dt_c = jnp.exp(colp[:, 2:3])
    d_c = colp[0:S5_GROUP, 3:4]

    row_q = lax.broadcasted_iota(jnp.int32, (p2, S5_CK), 0)
    lane_q = lax.broadcasted_iota(jnp.int32, (1, p2), 1)

    tau = lax.broadcasted_iota(jnp.int32, (p2, p2), 1).astype(F32)
    row_pp = lax.broadcasted_iota(jnp.int32, (p2, p2), 0)
    mag = jnp.exp(tau * dt_c * are_c)
    ang = tau * dt_c * aim_c
    pw = mag * jnp.where(row_pp < S5_STATE, jnp.cos(ang), jnp.sin(ang))

    l_t = lax.broadcasted_iota(jnp.int32, (p2, S5_CK), 1) // S5_GROUP
    e_tau = (l_t == row_q).astype(BF16)
    e_tau1 = (l_t + 1 == row_q).astype(BF16)
    l_i = lax.broadcasted_iota(jnp.int32, (S5_GROUP, S5_CK), 1) % S5_GROUP
    e_ch = (l_i == lax.broadcasted_iota(jnp.int32, (S5_GROUP, S5_CK), 0)).astype(BF16)

    def swap_halves(x):
        return jnp.concatenate([x[S5_STATE:], x[:S5_STATE]], axis=0)

    x1 = _onehot_dot(pw, e_tau)
    x1s = _onehot_dot(pw, e_tau1)
    ct = ct_ref[0]
    cta = _onehot_dot(ct[0], e_ch)
    ctb = _onehot_dot(ct[1], e_ch)
    ctb = jnp.where(row_q < S5_STATE, -ctb, ctb)
    z = cta * x1 + ctb * swap_halves(x1)
    zs = cta * x1s + ctb * swap_halves(x1s)
    wt = jnp.where(row_q < S5_STATE, zs, -zs)

    lam_mag = jnp.exp(dt_r * are_r)
    lam_re = lam_mag * jnp.cos(dt_r * aim_r)
    lam_im = lam_mag * jnp.sin(dt_r * aim_r)
    den = are_r * are_r + aim_r * aim_r
    n_re = lam_re - 1.0
    f_re = (n_re * are_r + lam_im * aim_r) / den
    f_im = (lam_im * are_r - n_re * aim_r) / den
    bt = bt_ref[0]
    bb_re = f_re * bt[0] - f_im * bt[1]
    bb_im = f_re * bt[1] + f_im * bt[0]
    first = lane_q < S5_STATE

    kt = jnp.dot(jnp.where(first, bb_re, -bb_im), z, precision=HIGHEST, preferred_element_type=F32)
    lane_k = lax.broadcasted_iota(jnp.int32, (S5_GROUP, S5_CK), 1)
    row_k = lax.broadcasted_iota(jnp.int32, (S5_GROUP, S5_CK), 0)
    kt = kt + jnp.where(lane_k == row_k, d_c, 0.0)
    for s in range(t_chunk):
        off = S5_GROUP * s
        blk = kt if s == 0 else jnp.where(lane_k >= off, pltpu.roll(kt, off, 1), 0.0)
        toep_ref[off:off + S5_GROUP, :] = blk.astype(BF16)

    e_s = (t_chunk - 1 - lax.broadcasted_iota(jnp.int32, (t_chunk, p2), 0)).astype(F32)
    pt_mag = jnp.exp(e_s * dt_r * are_r)
    pt_ang = e_s * dt_r * aim_r
    pt_re = pt_mag * jnp.cos(pt_ang)
    pt_im = pt_mag * jnp.sin(pt_ang)
    r_s = lax.broadcasted_iota(jnp.int32, (S5_CK, t_chunk), 0) // S5_GROUP
    e_rows = (r_s == lax.broadcasted_iota(jnp.int32, (S5_CK, t_chunk), 1)).astype(BF16)
    r_j = lax.broadcasted_iota(jnp.int32, (S5_CK, S5_GROUP), 0) % S5_GROUP
    e_rowj = (r_j == lax.broadcasted_iota(jnp.int32, (S5_CK, S5_GROUP), 1)).astype(BF16)
    gt = (_onehot_dot(e_rows, pt_re) * _onehot_dot(e_rowj, jnp.where(first, bb_re, bb_im))
          + _onehot_dot(e_rows, pt_im) * _onehot_dot(e_rowj, jnp.where(first, -bb_im, bb_re)))

    u = u_ref[0]
    h = jnp.dot(u, gt.astype(BF16), preferred_element_type=F32)

    k_row = lax.broadcasted_iota(jnp.int32, (8, p2), 0)
    m_pow = (t_chunk * (1 << k_row)).astype(F32)
    m_mag = jnp.exp(m_pow * dt_r * are_r)
    m_ang = m_pow * dt_r * aim_r
    m_re = m_mag * jnp.cos(m_ang)
    m_im = m_mag * jnp.sin(m_ang)
    row_c = lax.broadcasted_iota(jnp.int32, (n_chunks, p2), 0)
    k = 0
    while (1 << k) < n_chunks:
        sh = 1 << k
        s_prev = jnp.where(row_c >= sh, pltpu.roll(h, sh, 0), 0.0)
        s_swap = pltpu.roll(s_prev, S5_STATE, 1)
        mb = jnp.where(first, -m_im[k:k + 1], m_im[k:k + 1])
        h = h + m_re[k:k + 1] * s_prev + mb * s_swap
        k += 1
    h_prev = jnp.where(row_c >= 1, pltpu.roll(h, 1, 0), 0.0)

    y = (jnp.dot(u, toep_ref[...], preferred_element_type=F32)
         + jnp.dot(h_prev.astype(BF16), wt.astype(BF16), preferred_element_type=F32))
    y_ref[0] = jax.nn.gelu(y).astype(y_ref.dtype)


def s5_mix(u, a_re, a_im, log_dt, b_re, b_im, c_re, c_im, d):
    seq = u.shape[0]
    n_chunks = seq // S5_CHUNK
    g = S5_GROUPS
    ug = u.reshape(n_chunks, S5_CHUNK, g, S5_GROUP).transpose(2, 0, 1, 3).reshape(g, n_chunks, S5_CK)

    dup = lambda v: jnp.concatenate([v, v], axis=-1)
    zeros = jnp.zeros((g, 2 * S5_STATE), F32)
    ldt = jnp.broadcast_to(log_dt[:, None], (g, 2 * S5_STATE))
    rowp = jnp.stack([dup(a_re), dup(a_im), ldt] + [zeros] * 5, axis=1)
    d_pad = jnp.pad(d, ((0, 0), (0, 2 * S5_STATE - S5_GROUP)))
    colp = jnp.stack([dup(a_re), dup(a_im), ldt, d_pad] + [zeros] * 4, axis=2)
    bt = jnp.stack([dup(b_re.transpose(0, 2, 1)), dup(b_im.transpose(0, 2, 1))], axis=1)
    ctr, cti = c_re.transpose(0, 2, 1), c_im.transpose(0, 2, 1)
    ct = jnp.stack([jnp.concatenate([ctr, ctr], 1), jnp.concatenate([cti, cti], 1)], axis=1)

    yg = pl.pallas_call(
        _s5_kernel,
        grid=(g,),
        in_specs=[pl.BlockSpec((1, n_chunks, S5_CK), lambda i: (i, 0, 0)),
                  pl.BlockSpec((1, 8, 2 * S5_STATE), lambda i: (i, 0, 0)),
                  pl.BlockSpec((1, 2 * S5_STATE, 8), lambda i: (i, 0, 0)),
                  pl.BlockSpec((1, 2, S5_GROUP, 2 * S5_STATE), lambda i: (i, 0, 0, 0)),
                  pl.BlockSpec((1, 2, 2 * S5_STATE, S5_GROUP), lambda i: (i, 0, 0, 0))],
        out_specs=pl.BlockSpec((1, n_chunks, S5_CK), lambda i: (i, 0, 0)),
        out_shape=jax.ShapeDtypeStruct((g, n_chunks, S5_CK), BF16),
        scratch_shapes=[pltpu.VMEM((S5_CK, S5_CK), BF16)],
        compiler_params=_params(("parallel",)),
    )(ug, rowp, colp, bt, ct)
    return yg.reshape(g, n_chunks, S5_CHUNK, S5_GROUP).transpose(1, 2, 0, 3).reshape(seq, S5_WIDTH)


def _mla_prep_kernel(mla_ref, pos_ref, ga_ref, gkv_ref, gq_ref, gkn_ref, gkr_ref,
                     invf_ref, wq_ref, wkv_ref, q_ref, k_ref, v_ref):
    cq = mla_ref[:, 0:CQ_W].astype(F32)
    lane = lax.broadcasted_iota(jnp.int32, cq.shape, 1)
    ssq = jnp.sum(jnp.where(lane < MLA_Q_RANK, cq * cq, 0.0), axis=-1, keepdims=True)
    hq = (cq * lax.rsqrt(ssq * (1.0 / MLA_Q_RANK) + EPS) * ga_ref[...]).astype(BF16)
    qf = jnp.dot(hq, wq_ref[...], preferred_element_type=F32)
    ckv = mla_ref[:, CKV_OFF:].astype(F32)
    lane_kv = lax.broadcasted_iota(jnp.int32, ckv.shape, 1)
    in_kv = jnp.logical_and(lane_kv >= CKV_LO, lane_kv < CKV_LO + MLA_KV_RANK)
    ssq_kv = jnp.sum(jnp.where(in_kv, ckv * ckv, 0.0), axis=-1, keepdims=True)
    hkv = (ckv * lax.rsqrt(ssq_kv * (1.0 / MLA_KV_RANK) + EPS) * gkv_ref[...]).astype(BF16)
    kvf = jnp.dot(hkv, wkv_ref[...], preferred_element_type=F32)

    l128 = lax.broadcasted_iota(jnp.int32, (cq.shape[0], 128), 1)
    half = MLA_ROPE // 2
    ang = pos_ref[...].astype(F32) * invf_ref[...]
    cosv = jnp.cos(ang)
    sinv = jnp.sin(ang)
    sgn_sin = jnp.where(l128 < half, -sinv, jnp.where(l128 < MLA_ROPE, sinv, 0.0))

    def rope(t):
        sw = jnp.where(l128 < half, pltpu.roll(t, 128 - half, 1), pltpu.roll(t, half, 1))
        return t * cosv + sw * sgn_sin

    kr = jnp.where(l128 < MLA_ROPE, pltpu.roll(mla_ref[:, KR_OFF:].astype(F32), MLA_ROPE, 1), 0.0)
    kr_ssq = jnp.sum(kr * kr, axis=-1, keepdims=True)
    kr_rot = rope(kr * gkr_ref[...])

    gq = gq_ref[...]
    gkn = gkn_ref[...]
    scale = MLA_QK ** -0.5
    inv_qk = 1.0 / MLA_QK
    ones_col = jnp.where(l128 == 0, 1.0, 0.0).astype(BF16)
    for h in range(MLA_HEADS):
        qh = qf[:, MLA_QK_PAD * h:MLA_QK_PAD * (h + 1)]
        rq = lax.rsqrt(jnp.sum(qh * qh, axis=-1, keepdims=True) * inv_qk + EPS)
        qn = qh * rq * gq
        q_ref[h] = (jnp.concatenate([qn[:, :MLA_NOPE], rope(qn[:, MLA_NOPE:])], axis=1) * scale).astype(BF16)
        kn = kvf[:, MLA_QK_PAD * h:MLA_QK_PAD * h + MLA_NOPE]
        rk = lax.rsqrt((jnp.sum(kn * kn, axis=-1, keepdims=True) + kr_ssq) * inv_qk + EPS)
        k_ref[h] = jnp.concatenate([kn * rk * gkn, kr_rot * rk], axis=1).astype(BF16)
        vh = kvf[:, MLA_QK_PAD * h + MLA_NOPE:MLA_QK_PAD * (h + 1)].astype(BF16)
        v_ref[h] = jnp.concatenate([vh, ones_col], axis=1)


def mla_prep(proj, pos_col, ga, gkv, gq, gkn, gkr, invf, wq, wkv, *, tm):
    seq = proj.shape[0]
    h = MLA_HEADS
    full = lambda shape: pl.BlockSpec(shape, lambda i: (0,) * len(shape))
    return pl.pallas_call(
        _mla_prep_kernel,
        grid=(seq // tm,),
        in_specs=[pl.BlockSpec((tm, MLA_W), lambda i: (i, PJ_MLA // MLA_W)),
                  pl.BlockSpec((tm, 1), lambda i: (i, 0)),
                  full((1, CQ_W)), full((1, CKV_W)), full((1, MLA_QK_PAD)), full((1, 128)), full((1, 128)),
                  full((1, 128)), full((CQ_W, h * MLA_QK_PAD)), full((CKV_W, h * MLA_QK_PAD))],
        out_specs=[pl.BlockSpec((h, tm, MLA_QK_PAD), lambda i: (0, i, 0)),
                   pl.BlockSpec((h, tm, MLA_QK_PAD), lambda i: (0, i, 0)),
                   pl.BlockSpec((h, tm, MLA_VP), lambda i: (0, i, 0))],
        out_shape=[jax.ShapeDtypeStruct((h, seq, MLA_QK_PAD), BF16),
                   jax.ShapeDtypeStruct((h, seq, MLA_QK_PAD), BF16),
                   jax.ShapeDtypeStruct((h, seq, MLA_VP), BF16)],
        compiler_params=_params(("parallel",)),
    )(proj, pos_col, ga, gkv, gq, gkn, gkr, invf, wq, wkv)


def _flash_kernel(q_ref, k_ref, v_ref, pq_ref, pk_ref, o_ref, sa, sb, pa, pb, ala, alb, m_ref, acc_ref, *, t):
    nq = q_ref.shape[1] // t
    nt = (((1,), (1,)), ((), ()))

    def blk(j):
        return pl.ds(pl.multiple_of(j * t, t), t)

    def score(qt, j, s_out):
        s_out[...] = lax.dot_general(q_ref[0, blk(qt), :], k_ref[0, blk(j), :], nt,
                                     preferred_element_type=F32)

    score(0, 0, sa)

    def tile(qi, carry):
        _flash_tile(qi, nq, blk, score, v_ref, pq_ref, pk_ref, o_ref, sa, sb, pa, pb, ala, alb, m_ref, acc_ref, t)
        return carry

    lax.fori_loop(0, nq, tile, 0)


def _flash_tile(qi, nq, blk, score, v_ref, pq_ref, pk_ref, o_ref, sa, sb, pa, pb, ala, alb, m_ref, acc_ref, t):
    def stage_a(j, s_out):
        score(qi, j, s_out)

    def stage_b(j, s_in, p_out, al_out, masked):
        s = s_in[...]
        if masked:
            s = jnp.where(pk_ref[:, blk(j)] <= pq_ref[blk(qi), :], s, NEG_INF)
        m_prev = m_ref[...]
        m_new = jnp.maximum(m_prev, jnp.max(s, axis=-1, keepdims=True))
        al_out[...] = jnp.exp(m_prev - m_new)
        p_out[...] = jnp.exp((s - jnp.concatenate([m_new] * (t // 128), axis=1)).astype(BF16))
        m_ref[...] = m_new

    def stage_c(j, p_in, al_in):
        al = al_in[...]
        acc_ref[...] = (jnp.concatenate([al] * (MLA_VP // 128), axis=1) * acc_ref[...]
                        + jnp.dot(p_in[...], v_ref[0, blk(j), :], preferred_element_type=F32))

    m_ref[...] = jnp.full(m_ref.shape, NEG_INF, F32)
    acc_ref[...] = jnp.zeros(acc_ref.shape, F32)
    pb[...] = jnp.zeros(pb.shape, BF16)
    alb[...] = jnp.ones(alb.shape, F32)
    nxt = jnp.minimum(qi + 1, nq - 1)

    def finish():
        acc = acc_ref[...]
        o_ref[blk(qi), :] = (acc[:, :MLA_V] / acc[:, MLA_V:MLA_V + 1]).astype(o_ref.dtype)

    def pair_body(j):
        stage_a(j + 1, sb)
        stage_b(j, sa, pa, ala, False)
        stage_c(jnp.maximum(j - 1, 0), pb, alb)
        stage_a(j + 2, sa)
        stage_b(j + 1, sb, pb, alb, False)
        stage_c(j, pa, ala)

    def quad(i, c):
        pair_body(4 * i)
        pair_body(4 * i + 2)
        return c

    lax.fori_loop(0, qi // 4, quad, 0)

    @pl.when((qi // 2) % 2 == 1)
    def _():
        pair_body(4 * (qi // 4))

    @pl.when(qi % 2 == 0)
    def _():
        stage_b(qi, sa, pa, ala, True)
        stage_c(jnp.maximum(qi - 1, 0), pb, alb)
        score(nxt, 0, sa)
        stage_c(qi, pa, ala)
        finish()

    @pl.when(qi % 2 == 1)
    def _():
        stage_a(qi, sb)
        stage_b(qi - 1, sa, pa, ala, False)
        stage_c(jnp.maximum(qi - 2, 0), pb, alb)
        score(nxt, 0, sa)
        stage_b(qi, sb, pb, alb, True)
        stage_c(qi - 1, pa, ala)
        stage_c(qi, pb, alb)
        finish()


def flash_attention(q, k, v, pos_col, pos_row, *, t):
    h, seq, _ = q.shape
    return pl.pallas_call(
        functools.partial(_flash_kernel, t=t),
        grid=(h,),
        in_specs=[pl.BlockSpec((1, seq, MLA_QK_PAD), lambda hh: (hh, 0, 0)),
                  pl.BlockSpec((1, seq, MLA_QK_PAD), lambda hh: (hh, 0, 0)),
                  pl.BlockSpec((1, seq, MLA_VP), lambda hh: (hh, 0, 0)),
                  pl.BlockSpec((seq, 1), lambda hh: (0, 0)),
                  pl.BlockSpec((1, seq), lambda hh: (0, 0))],
        out_specs=pl.BlockSpec((seq, MLA_V), lambda hh: (0, hh)),
        out_shape=jax.ShapeDtypeStruct((seq, h * MLA_V), BF16),
        scratch_shapes=[pltpu.VMEM((t, t), F32), pltpu.VMEM((t, t), F32),
                        pltpu.VMEM((t, t), BF16), pltpu.VMEM((t, t), BF16),
                        pltpu.VMEM((t, 128), F32), pltpu.VMEM((t, 128), F32),
                        pltpu.VMEM((t, 128), F32), pltpu.VMEM((t, MLA_VP), F32)],
        compiler_params=_params(("parallel",)),
    )(q, k, v, pos_col, pos_row)


def _mem_kv_kernel(mem_ref, g_ref, w_ref, gk_ref, k_ref, v_ref):
    m = _rms(mem_ref[...], g_ref[...]).astype(BF16)
    kv = jnp.dot(m, w_ref[...], preferred_element_type=F32)
    for h in range(MEM_HEADS):
        kh = kv[:, MEM_HEAD_DIM * h:MEM_HEAD_DIM * (h + 1)]
        k_ref[:, MEM_HEAD_DIM * h:MEM_HEAD_DIM * (h + 1)] = _rms(kh, gk_ref[...]).astype(BF16)
    v_ref[...] = kv[:, MEM_WIDTH:].astype(BF16)


def mem_kv(mem, g, w, gk):
    n_mem = mem.shape[0]
    full = lambda shape: pl.BlockSpec(shape, lambda i: (0,) * len(shape))
    return pl.pallas_call(
        _mem_kv_kernel,
        grid=(1,),
        in_specs=[full(mem.shape), full(g.shape), full(w.shape), full(gk.shape)],
        out_specs=[full((n_mem, MEM_WIDTH)), full((n_mem, MEM_WIDTH))],
        out_shape=[jax.ShapeDtypeStruct((n_mem, MEM_WIDTH), BF16)] * 2,
        compiler_params=_params(("arbitrary",)),
    )(mem, g, w, gk)


def _mem_attn_kernel(q_ref, k_ref, v_ref, gq_ref, o_ref):
    scale = MEM_HEAD_DIM ** -0.5
    for h in range(MEM_HEADS):
        sl = slice(MEM_HEAD_DIM * h, MEM_HEAD_DIM * (h + 1))
        qh = (_rms(q_ref[:, sl].astype(F32), gq_ref[...]) * scale).astype(BF16)
        s = lax.dot_general(qh, k_ref[:, sl], (((1,), (1,)), ((), ())), preferred_element_type=F32)
        p = jnp.exp(s - jnp.max(s, axis=-1, keepdims=True))
        o = jnp.dot(p.astype(BF16), v_ref[:, sl], preferred_element_type=F32)
        o_ref[:, sl] = (o / jnp.sum(p, axis=-1, keepdims=True)).astype(o_ref.dtype)


def mem_attention(proj, k, v, gq, *, tm):
    seq = proj.shape[0]
    n_mem = k.shape[0]
    return pl.pallas_call(
        _mem_attn_kernel,
        grid=(seq // tm,),
        in_specs=[pl.BlockSpec((tm, MEM_WIDTH), lambda i: (i, PJ_MQ // MEM_WIDTH)),
                  pl.BlockSpec((n_mem, MEM_WIDTH), lambda i: (0, 0)),
                  pl.BlockSpec((n_mem, MEM_WIDTH), lambda i: (0, 0)),
                  pl.BlockSpec((1, MEM_HEAD_DIM), lambda i: (0, 0))],
        out_specs=pl.BlockSpec((tm, MEM_WIDTH), lambda i: (i, 0)),
        out_shape=jax.ShapeDtypeStruct((seq, MEM_WIDTH), BF16),
        compiler_params=_params(("parallel",)),
    )(proj, k, v, gq)


def _ffn_kernel(x_ref, g_ref, wg_ref, wu_ref, wd_ref, o_ref, h_ref, acc_ref):
    f = pl.program_id(1)

    @pl.when(f == 0)
    def _():
        h_ref[...] = _rms(x_ref[...], g_ref[...]).astype(BF16)
        acc_ref[...] = jnp.zeros_like(acc_ref)

    h = h_ref[...]
    gate = jnp.dot(h, wg_ref[...], preferred_element_type=F32)
    up = jnp.dot(h, wu_ref[...], preferred_element_type=F32)
    a = (jax.nn.silu(gate) * up).astype(BF16)
    acc_ref[...] += jnp.dot(a, wd_ref[...], preferred_element_type=F32)

    @pl.when(f == pl.num_programs(1) - 1)
    def _():
        o_ref[...] = x_ref[...] + acc_ref[...]


def dense_ffn(x, g, w_gate_up, w_down, *, tm, tf):
    m, d = x.shape
    nf = D_FF // tf
    return pl.pallas_call(
        _ffn_kernel,
        grid=(m // tm, nf),
        in_specs=[pl.BlockSpec((tm, d), lambda i, f: (i, 0)),
                  pl.BlockSpec((1, d), lambda i, f: (0, 0)),
                  pl.BlockSpec((d, tf), lambda i, f: (0, f)),
                  pl.BlockSpec((d, tf), lambda i, f: (0, f + nf)),
                  pl.BlockSpec((tf, d), lambda i, f: (f, 0))],
        out_specs=pl.BlockSpec((tm, d), lambda i, f: (i, 0)),
        out_shape=jax.ShapeDtypeStruct((m, d), F32),
        scratch_shapes=[pltpu.VMEM((tm, d), BF16), pltpu.VMEM((tm, d), F32)],
        compiler_params=_params(("parallel", "arbitrary")),
    )(x, g, w_gate_up, w_gate_up, w_down)


def _router_kernel(x_ref, g_ref, w_ref, b_ref, idx_ref, wgt_ref):
    h = _rms(x_ref[...], g_ref[...])
    logits = jnp.dot(h, w_ref[...], precision=HIGHEST, preferred_element_type=F32) + b_ref[...]
    lane = lax.broadcasted_iota(jnp.int32, logits.shape, 1)
    logits = jnp.where(lane < N_EXPERTS, logits, -jnp.inf)
    v1 = jnp.max(logits, axis=-1, keepdims=True)
    i1 = jnp.min(jnp.where(logits == v1, lane, 128), axis=-1, keepdims=True)
    rest = jnp.where(lane == i1, -jnp.inf, logits)
    v2 = jnp.max(rest, axis=-1, keepdims=True)
    i2 = jnp.min(jnp.where(rest == v2, lane, 128), axis=-1, keepdims=True)
    e2 = jnp.exp(v2 - v1)
    w1 = 1.0 / (1.0 + e2)
    w2 = e2 / (1.0 + e2)
    idx_ref[...] = jnp.where(lane == 0, i1, jnp.where(lane == 1, i2, 0))
    wgt_ref[...] = jnp.where(lane == 0, w1, jnp.where(lane == 1, w2, 0.0))


def moe_router(x, g, w_pad, b_pad, *, tm):
    m, d = x.shape
    return pl.pallas_call(
        _router_kernel,
        grid=(m // tm,),
        in_specs=[pl.BlockSpec((tm, d), lambda i: (i, 0)),
                  pl.BlockSpec((1, d), lambda i: (0, 0)),
                  pl.BlockSpec((d, 128), lambda i: (0, 0)),
                  pl.BlockSpec((1, 128), lambda i: (0, 0))],
        out_specs=[pl.BlockSpec((tm, 128), lambda i: (i, 0)),
                   pl.BlockSpec((tm, 128), lambda i: (i, 0))],
        out_shape=[jax.ShapeDtypeStruct((m, 128), jnp.int32),
                   jax.ShapeDtypeStruct((m, 128), F32)],
        compiler_params=_params(("parallel",)),
    )(x, g, w_pad, b_pad)


MOE_ROW_STEPS = 4


def _moe_ffn_kernel(te_ref, tr_ref, src_ref, x_hbm, g_ref, wg_ref, wu_ref, wd_ref, o_ref, xbuf, h_ref, sem):
    i = pl.program_id(0)
    f = pl.program_id(1)
    n_tiles = pl.num_programs(0)
    rows = tr_ref[i]
    tm = h_ref.shape[0]
    per_step = xbuf.shape[0] // pl.num_programs(1)

    def row_copy(idx, r):
        return pltpu.make_async_copy(x_hbm.at[pl.ds(idx, 1), :], xbuf.at[pl.ds(r, 1), :], sem)

    def start_rows(tile, step):
        for k in range(per_step):
            r = step * per_step + k
            row_copy(src_ref[tile * tm + jnp.minimum(r, tm - 1)], r).start()

    def wait_all():
        def wait_row(r, c):
            row_copy(0, r).wait()
            return c

        lax.fori_loop(0, xbuf.shape[0], wait_row, 0, unroll=4)

    is_first = i == 0
    is_last = i == n_tiles - 1
    nxt = jnp.minimum(i + 1, n_tiles - 1)
    prev_rows = tr_ref[jnp.maximum(i - 1, 0)]

    @pl.when(f == 0)
    def _():
        o_ref[...] = jnp.zeros_like(o_ref)

        @pl.when(jnp.logical_and(is_first, rows > 0))
        def _():
            def first(step, c):
                start_rows(0, step)
                return c

            lax.fori_loop(0, pl.num_programs(1), first, 0)

        @pl.when(jnp.where(is_first, rows, prev_rows) > 0)
        def _():
            wait_all()

        @pl.when(rows > 0)
        def _():
            h_ref[...] = _rms(xbuf[0:tm, :], g_ref[...]).astype(BF16)

    def run(n):
        start_rows(nxt, f)
        h = h_ref[0:n, :]
        gate = jnp.dot(h, wg_ref[0].astype(BF16), preferred_element_type=F32)
        up = jnp.dot(h, wu_ref[0].astype(BF16), preferred_element_type=F32)
        a = (jax.nn.silu(gate) * up).astype(BF16)
        o_ref[0:n, :] += jnp.dot(a, wd_ref[0].astype(BF16), preferred_element_type=F32)

    step = tm // MOE_ROW_STEPS
    for k in range(1, MOE_ROW_STEPS + 1):
        @pl.when(jnp.logical_and(rows > (k - 1) * step, rows <= k * step))
        def _(n=k * step):
            run(n)

    @pl.when(jnp.logical_and(jnp.logical_and(is_last, f == pl.num_programs(1) - 1), rows > 0))
    def _():
        wait_all()


def moe_ffn(x, g, src, tile_expert, tile_rows, w_gate_up, w_down, *, tm, tf):
    d = x.shape[1]
    p = src.shape[0]
    nf = D_FF // tf
    last = nf - 1
    per_step = -(-tm // nf)
    while (nf * per_step) % 8:
        per_step += 1
    gather_rows = nf * per_step

    def fsel(i, f, tr):
        return jnp.where(tr[i] > 0, f, last)

    return pl.pallas_call(
        _moe_ffn_kernel,
        grid_spec=pltpu.PrefetchScalarGridSpec(
            num_scalar_prefetch=3,
            grid=(p // tm, nf),
            in_specs=[pl.BlockSpec(memory_space=pl.ANY),
                      pl.BlockSpec((1, d), lambda i, f, te, tr, src: (0, 0)),
                      pl.BlockSpec((1, d, tf), lambda i, f, te, tr, src: (te[i], 0, fsel(i, f, tr))),
                      pl.BlockSpec((1, d, tf), lambda i, f, te, tr, src: (te[i], 0, fsel(i, f, tr) + nf)),
                      pl.BlockSpec((1, tf, d), lambda i, f, te, tr, src: (te[i], fsel(i, f, tr), 0))],
            out_specs=pl.BlockSpec((tm, d), lambda i, f, te, tr, src: (i, 0)),
            scratch_shapes=[pltpu.VMEM((gather_rows, d), F32), pltpu.VMEM((tm, d), BF16),
                            pltpu.SemaphoreType.DMA(())]),
        out_shape=jax.ShapeDtypeStruct((p, d), F32),
        compiler_params=_params(("arbitrary", "arbitrary")),
    )(tile_expert, tile_rows, src, x, g, w_gate_up, w_gate_up, w_down)


def _combine_kernel(p0_ref, p1_ref, x_ref, w_ref, ys_hbm, o_ref, buf, sem, *, tt):
    base = pl.program_id(0) * tt

    def issue(j, c):
        r0 = p0_ref[base + j]
        r1 = p1_ref[base + j]
        pltpu.make_async_copy(ys_hbm.at[pl.ds(r0, 1), :], buf.at[0, pl.ds(j, 1), :], sem.at[0]).start()
        pltpu.make_async_copy(ys_hbm.at[pl.ds(r1, 1), :], buf.at[1, pl.ds(j, 1), :], sem.at[1]).start()
        return c

    lax.fori_loop(0, tt, issue, 0, unroll=8)

    def drain(j, c):
        pltpu.make_async_copy(ys_hbm.at[pl.ds(0, 1), :], buf.at[0, pl.ds(j, 1), :], sem.at[0]).wait()
        pltpu.make_async_copy(ys_hbm.at[pl.ds(0, 1), :], buf.at[1, pl.ds(j, 1), :], sem.at[1]).wait()
        return c

    lax.fori_loop(0, tt, drain, 0, unroll=8)
    w = w_ref[...]
    o_ref[...] = x_ref[...] + (w[:, 0:1] * buf[0] + w[:, 1:2] * buf[1])


def moe_combine(x, wgt, ys, pos0, pos1, *, tt):
    m, d = x.shape
    return pl.pallas_call(
        functools.partial(_combine_kernel, tt=tt),
        grid_spec=pltpu.PrefetchScalarGridSpec(
            num_scalar_prefetch=2,
            grid=(m // tt,),
            in_specs=[pl.BlockSpec((tt, d), lambda i, p0, p1: (i, 0)),
                      pl.BlockSpec((tt, 128), lambda i, p0, p1: (i, 0)),
                      pl.BlockSpec(memory_space=pl.ANY)],
            out_specs=pl.BlockSpec((tt, d), lambda i, p0, p1: (i, 0)),
            scratch_shapes=[pltpu.VMEM((2, tt, d), F32), pltpu.SemaphoreType.DMA((2,))]),
        out_shape=jax.ShapeDtypeStruct((m, d), F32),
        compiler_params=_params(("arbitrary",)),
    )(pos0, pos1, x, wgt, ys)


def moe_layer(x, g, router, router_b, w_gate_up, w_down, expert_base, *, tm, tf):
    seq, d = x.shape
    w_pad = jnp.pad(router, ((0, 0), (0, 128 - N_EXPERTS)))
    b_pad = jnp.pad(router_b, (0, 128 - N_EXPERTS))[None, :]
    idx, wgt = moe_router(x, g, w_pad, b_pad, tm=min(512, seq))

    e_flat = idx[:, :TOP_K].reshape(-1)
    onehot = (e_flat[:, None] == jnp.arange(N_EXPERTS, dtype=jnp.int32)[None, :]).astype(jnp.int32)
    rank = jnp.cumsum(onehot, axis=0) - onehot
    counts = jnp.sum(onehot, axis=0)
    tiles_per = (counts + tm - 1) // tm
    tile_ends = jnp.cumsum(tiles_per)
    tile_starts = tile_ends - tiles_per
    pos = jnp.sum(onehot * (tile_starts[None, :] * tm + rank), axis=1).astype(jnp.int32)
    n_tiles = TOP_K * seq // tm + N_EXPERTS
    p_total = n_tiles * tm
    tok = jnp.arange(TOP_K * seq, dtype=jnp.int32) // TOP_K
    src = jnp.zeros((p_total,), jnp.int32).at[pos].set(tok)
    tile_idx = jnp.arange(n_tiles, dtype=jnp.int32)
    tile_expert = jnp.minimum(jnp.sum((tile_idx[:, None] >= tile_ends[None, :]).astype(jnp.int32), axis=1),
                              N_EXPERTS - 1).astype(jnp.int32)
    valid = tile_idx < tile_ends[-1]
    remaining = counts[tile_expert] - (tile_idx - tile_starts[tile_expert]) * tm
    tile_rows = jnp.where(valid, jnp.minimum(remaining, tm), 0).astype(jnp.int32)
    tile_expert = jnp.where(valid, tile_expert, jnp.max(jnp.where(valid, tile_expert, 0))).astype(jnp.int32)

    ys = moe_ffn(x, g, src, tile_expert + expert_base, tile_rows, w_gate_up, w_down, tm=tm, tf=tf)
    pos2 = pos.reshape(seq, TOP_K)
    return moe_combine(x, wgt, ys, pos2[:, 0], pos2[:, 1], tt=min(256, seq))


def _repack_w_q_b(w):
    w = w.reshape(MLA_Q_RANK, MLA_HEADS, MLA_QK)
    w = jnp.pad(w, ((0, CQ_W - MLA_Q_RANK), (0, 0), (0, MLA_QK_PAD - MLA_QK)))
    return w.reshape(CQ_W, MLA_HEADS * MLA_QK_PAD).astype(BF16)


def mixer_layer(x, mem, pos_col, pos_row, invf, p):
    seq = x.shape[0]
    tm = min(1024, seq)
    row = lambda v: v[None, :]
    proj = norm_matmul(x, row(p['norm_mix']), p['w_in'].astype(BF16), tm=tm, tn=1024)

    y = s5_mix(proj[:, PJ_U:PJ_U + S5_WIDTH], p['s5_a_re'], p['s5_a_im'], p['s5_log_dt'], p['s5_b_re'],
               p['s5_b_im'], p['s5_c_re'], p['s5_c_im'], p['s5_d'])

    ga = row(jnp.pad(p['mla_q_a_norm'], (0, CQ_W - MLA_Q_RANK)))
    kv_pad = (CKV_LO, CKV_W - CKV_LO - MLA_KV_RANK)
    gkv = row(jnp.pad(p['mla_kv_norm'], kv_pad))
    gq = row(jnp.pad(p['mla_q_norm'], (0, MLA_QK_PAD - MLA_QK)))
    gkn = row(p['mla_k_norm'][:MLA_NOPE])
    gkr = row(jnp.pad(p['mla_k_norm'][MLA_NOPE:], (0, 128 - MLA_ROPE)))
    wkv = jnp.pad(p['mla_w_kv_b'], (kv_pad, (0, 0))).astype(BF16)
    q, k, v = mla_prep(proj, pos_col, ga, gkv, gq, gkn, gkr, invf,
                       _repack_w_q_b(p['mla_w_q_b']), wkv, tm=min(256, seq))
    o = flash_attention(q, k, v, pos_col, pos_row, t=min(512, seq))

    km, vm = mem_kv(mem, row(p['mem_norm']), p['mem_w_kv'].astype(BF16), row(p['mem_k_norm']))
    om = mem_attention(proj, km, vm, row(p['mem_q_norm']), tm=min(512, seq))
    merged = merge_branches(y, o, om, p['s5_w_glu'].astype(BF16), p['mla_w_o'].astype(BF16),
                            p['mem_w_o'].astype(BF16), proj, tm=tm, tn=512)

    return resid_matmul(merged, p['w_out'].astype(BF16), x, tm=tm, tn=1024)


_LAYER_KEYS = ('norm_mix', 'w_in', 's5_a_re', 's5_a_im', 's5_log_dt', 's5_b_re', 's5_b_im', 's5_c_re',
               's5_c_im', 's5_d', 's5_w_glu', 'mla_q_a_norm', 'mla_w_q_b', 'mla_kv_norm', 'mla_w_kv_b',
               'mla_q_norm', 'mla_k_norm', 'mla_w_o', 'mem_norm', 'mem_w_kv', 'mem_q_norm', 'mem_k_norm',
               'mem_w_o', 'w_out')


def kernel(x, mem, positions, norm_mix, w_in, s5_a_re, s5_a_im, s5_log_dt, s5_b_re, s5_b_im, s5_c_re, s5_c_im, s5_d, s5_w_glu, mla_q_a_norm, mla_w_q_b, mla_kv_norm, mla_w_kv_b, mla_q_norm, mla_k_norm, mla_w_o, mem_norm, mem_w_kv, mem_q_norm, mem_k_norm, mem_w_o, w_out, norm_ffn, ffn_w_gate_up, ffn_w_down, moe_router, moe_router_b, moe_w_gate_up, moe_w_down):
    stacked = dict(zip(_LAYER_KEYS, (norm_mix, w_in, s5_a_re, s5_a_im, s5_log_dt, s5_b_re, s5_b_im, s5_c_re,
                                     s5_c_im, s5_d, s5_w_glu, mla_q_a_norm, mla_w_q_b, mla_kv_norm,
                                     mla_w_kv_b, mla_q_norm, mla_k_norm, mla_w_o, mem_norm, mem_w_kv,
                                     mem_q_norm, mem_k_norm, mem_w_o, w_out)))
    bsz, seq, d = x.shape
    depth = norm_mix.shape[0]
    half = MLA_ROPE // 2
    inv_freq = ROPE_THETA ** (-jnp.arange(0, MLA_ROPE, 2, dtype=F32) / MLA_ROPE)
    invf = jnp.concatenate([inv_freq, inv_freq, jnp.zeros((128 - 2 * half,), F32)])[None, :]
    moe_gu = moe_w_gate_up.reshape((-1,) + moe_w_gate_up.shape[2:])
    moe_dn = moe_w_down.reshape((-1,) + moe_w_down.shape[2:])

    outs = []
    for b in range(bsz):
        xb = x[b]
        pos_col = positions[b][:, None]
        pos_row = positions[b][None, :]
        for l in range(depth):
            p = {key: val[l] for key, val in stacked.items()}
            xb = mixer_layer(xb, mem[b], pos_col, pos_row, invf, p)
            gf = norm_ffn[l][None, :]
            if l % 2 == 0:
                xb = dense_ffn(xb, gf, ffn_w_gate_up[l // 2].astype(BF16), ffn_w_down[l // 2].astype(BF16),
                               tm=min(512, seq), tf=512)
            else:
                xb = moe_layer(xb, gf, moe_router[l // 2], moe_router_b[l // 2], moe_gu, moe_dn,
                               (l // 2) * N_EXPERTS, tm=min(1024, TOP_K * seq), tf=256)
        outs.append(xb)
    return jnp.stack(outs, axis=0)
```

```python
import functools
import math

import jax
import jax.numpy as jnp
from jax import lax
from jax.experimental import pallas as pl
from jax.experimental.pallas import tpu as pltpu

F32 = jnp.float32
BF16 = jnp.bfloat16
HIGHEST = lax.Precision.HIGHEST

D_MODEL = 2048
N_MEM = 256
S5_WIDTH = 1024
S5_GROUP = 16
S5_GROUPS = S5_WIDTH // S5_GROUP
S5_STATE = 64
S5_CHUNK = 32
S5_CK = S5_CHUNK * S5_GROUP
S5_GROUPS_PER_STEP = 128 // S5_GROUP
MLA_HEADS = 16
MLA_Q_RANK = 448
MLA_KV_RANK = 512
MLA_NOPE = 128
MLA_ROPE = 64
MLA_QK = MLA_NOPE + MLA_ROPE
MLA_V = 128
MLA_QK_PAD = 256
MLA_VP = 256
ROPE_THETA = 10000.0
MEM_HEADS = 4
MEM_HEAD_DIM = 256
MEM_WIDTH = MEM_HEADS * MEM_HEAD_DIM
N_BRANCH = 3
D_FF = 7168
N_EXPERTS = 8
TOP_K = 2
EPS = 1e-6
NEG_INF = -1e30

PJ_U = 0
PJ_MLA = 1024
PJ_MQ = 2048
PJ_GATE = 3072
MLA_W = MLA_Q_RANK + MLA_KV_RANK + MLA_ROPE
CQ_W = 512
CKV_OFF = 384
CKV_W = MLA_W - CKV_OFF
CKV_LO = MLA_Q_RANK - CKV_OFF
KR_OFF = MLA_W - 128

VMEM_LIMIT = 52 * 1024 * 1024


def _params(sem):
    return pltpu.CompilerParams(dimension_semantics=sem, vmem_limit_bytes=VMEM_LIMIT)


def _rms(x, g):
    r = lax.rsqrt(jnp.mean(x * x, axis=-1, keepdims=True) + EPS)
    return x * r * g


def _onehot_dot(a, b):
    if a.dtype == BF16:
        hi = b.astype(BF16)
        lo = (b - hi.astype(F32)).astype(BF16)
        return (jnp.dot(a, hi, preferred_element_type=F32) + jnp.dot(a, lo, preferred_element_type=F32))
    hi = a.astype(BF16)
    lo = (a - hi.astype(F32)).astype(BF16)
    return (jnp.dot(hi, b, preferred_element_type=F32) + jnp.dot(lo, b, preferred_element_type=F32))


def _norm_mm_kernel(x_ref, g_ref, w_ref, o_ref, h_ref):
    @pl.when(pl.program_id(1) == 0)
    def _():
        h_ref[...] = _rms(x_ref[...], g_ref[...]).astype(BF16)

    o_ref[...] = jnp.dot(h_ref[...], w_ref[...], preferred_element_type=F32).astype(o_ref.dtype)


def norm_matmul(x, g, w, *, tm, tn, out_dtype=BF16):
    m, k = x.shape
    n = w.shape[1]
    return pl.pallas_call(
        _norm_mm_kernel,
        grid=(m // tm, n // tn),
        in_specs=[pl.BlockSpec((tm, k), lambda i, j: (i, 0)),
                  pl.BlockSpec((1, k), lambda i, j: (0, 0)),
                  pl.BlockSpec((k, tn), lambda i, j: (0, j))],
        out_specs=pl.BlockSpec((tm, tn), lambda i, j: (i, j)),
        out_shape=jax.ShapeDtypeStruct((m, n), out_dtype),
        scratch_shapes=[pltpu.VMEM((tm, k), BF16)],
        compiler_params=_params(("parallel", "arbitrary")),
    )(x, g, w)


def _merge_kernel(y_ref, o_ref, om_ref, wa_ref, wb_ref, wo_ref, wm_ref, g0_ref, g1_ref, g2_ref, out_ref):
    y = y_ref[...]
    s5 = (jnp.dot(y, wa_ref[...], preferred_element_type=F32)
          * jax.nn.sigmoid(jnp.dot(y, wb_ref[...], preferred_element_type=F32)))
    mla = jnp.dot(o_ref[...], wo_ref[...], preferred_element_type=F32)
    mem = jnp.dot(om_ref[...], wm_ref[...], preferred_element_type=F32)
    gate = lambda r: jax.nn.sigmoid(r[...].astype(F32))
    out_ref[...] = (gate(g0_ref) * s5 + gate(g1_ref) * mla + gate(g2_ref) * mem).astype(out_ref.dtype)


def merge_branches(y, o, om, w_glu, w_o, w_mo, proj, *, tm, tn):
    m = y.shape[0]
    n = w_o.shape[1]
    nb = n // tn
    gb = PJ_GATE // tn
    lhs = lambda a: pl.BlockSpec((tm, a.shape[1]), lambda i, j: (i, 0))
    col = lambda w, off: pl.BlockSpec((w.shape[0], tn), lambda i, j: (0, j + off))
    gate = lambda b: pl.BlockSpec((tm, tn), lambda i, j: (i, gb + b * nb + j))
    return pl.pallas_call(
        _merge_kernel,
        grid=(m // tm, nb),
        in_specs=[lhs(y), lhs(o), lhs(om), col(w_glu, 0), col(w_glu, nb), col(w_o, 0), col(w_mo, 0),
                  gate(0), gate(1), gate(2)],
        out_specs=pl.BlockSpec((tm, tn), lambda i, j: (i, j)),
        out_shape=jax.ShapeDtypeStruct((m, n), BF16),
        compiler_params=_params(("parallel", "parallel")),
    )(y, o, om, w_glu, w_glu, w_o, w_mo, proj, proj, proj)


def _resid_mm_kernel(a_ref, w_ref, x_ref, o_ref):
    o_ref[...] = x_ref[...] + jnp.dot(a_ref[...], w_ref[...], preferred_element_type=F32)


def resid_matmul(a, w, x, *, tm, tn):
    m, k = a.shape
    n = w.shape[1]
    return pl.pallas_call(
        _resid_mm_kernel,
        grid=(m // tm, n // tn),
        in_specs=[pl.BlockSpec((tm, k), lambda i, j: (i, 0)),
                  pl.BlockSpec((k, tn), lambda i, j: (0, j)),
                  pl.BlockSpec((tm, tn), lambda i, j: (i, j))],
        out_specs=pl.BlockSpec((tm, tn), lambda i, j: (i, j)),
        out_shape=jax.ShapeDtypeStruct((m, n), F32),
        compiler_params=_params(("parallel", "parallel")),
    )(a, w, x)


def _s5_kernel(u_ref, perm_ref, rowp_ref, colp_ref, bt_ref, ct_ref, y_ref, toep_ref):
    ng = S5_GROUPS_PER_STEP
    w = ng * 128
    n_tiles = S5_CK // 128
    perm = perm_ref[...]
    parts = [jnp.dot(u_ref[0, :, a * w:(a + 1) * w], perm, preferred_element_type=F32).astype(BF16)
             for a in range(n_tiles)]
    ys = []
    for gi in range(ng):
        u = jnp.concatenate([part[:, gi * 128:(gi + 1) * 128] for part in parts], axis=1)
        ys.append(_s5_group(gi, u, rowp_ref, colp_ref, bt_ref, ct_ref, toep_ref))
    for a in range(n_tiles):
        ycat = jnp.concatenate([y[:, a * 128:(a + 1) * 128] for y in ys], axis=1)
        y_ref[0, :, a * w:(a + 1) * w] = lax.dot_general(
            ycat, perm, (((1,), (1,)), ((), ())), preferred_element_type=F32).astype(y_ref.dtype)


def _s5_group(gi, u, rowp_ref, colp_ref, bt_ref, ct_ref, toep_ref):
    t_chunk = S5_CHUNK
    n_chunks = u.shape[0]
    p2 = 2 * S5_STATE

    rowp = rowp_ref[gi]
    are_r, aim_r = rowp[0:1], rowp[1:2]
    dt_r = jnp.exp(rowp[2:3])
    colp = colp_ref[gi]
    are_c, aim_c = colp[:, 0:1], colp[:, 1:2]
    dt_c = jnp.exp(colp[:, 2:3])
    d_c = colp[0:S5_GROUP, 3:4]

    row_q = lax.broadcasted_iota(jnp.int32, (p2, S5_CK), 0)
    lane_q = lax.broadcasted_iota(jnp.int32, (1, p2), 1)

    tau = lax.broadcasted_iota(jnp.int32, (p2, p2), 1).astype(F32)
    row_pp = lax.broadcasted_iota(jnp.int32, (p2, p2), 0)
    mag = jnp.exp(tau * dt_c * are_c)
    ang = tau * dt_c * aim_c
    pw = mag * jnp.where(row_pp < S5_STATE, jnp.cos(ang), jnp.sin(ang))

    l_t = lax.broadcasted_iota(jnp.int32, (p2, S5_CK), 1) // S5_GROUP
    e_tau = (l_t == row_q).astype(BF16)
    e_tau1 = (l_t + 1 == row_q).astype(BF16)
    l_i = lax.broadcasted_iota(jnp.int32, (S5_GROUP, S5_CK), 1) % S5_GROUP
    e_ch = (l_i == lax.broadcasted_iota(jnp.int32, (S5_GROUP, S5_CK), 0)).astype(BF16)

    def swap_halves(x):
        return jnp.concatenate([x[S5_STATE:], x[:S5_STATE]], axis=0)

    x1 = _onehot_dot(pw, e_tau)
    x1s = _onehot_dot(pw, e_tau1)
    ct = ct_ref[gi]
    cta = _onehot_dot(ct[0], e_ch)
    ctb = _onehot_dot(ct[1], e_ch)
    ctb = jnp.where(row_q < S5_STATE, -ctb, ctb)
    z = cta * x1 + ctb * swap_halves(x1)
    zs = cta * x1s + ctb * swap_halves(x1s)
    wt = jnp.where(row_q < S5_STATE, zs, -zs)

    lam_mag = jnp.exp(dt_r * are_r)
    lam_re = lam_mag * jnp.cos(dt_r * aim_r)
    lam_im = lam_mag * jnp.sin(dt_r * aim_r)
    den = are_r * are_r + aim_r * aim_r
    n_re = lam_re - 1.0
    f_re = (n_re * are_r + lam_im * aim_r) / den
    f_im = (lam_im * are_r - n_re * aim_r) / den
    bt = bt_ref[gi]
    bb_re = f_re * bt[0] - f_im * bt[1]
    bb_im = f_re * bt[1] + f_im * bt[0]
    first = lane_q < S5_STATE

    kt = jnp.dot(jnp.where(first, bb_re, -bb_im), z, precision=HIGHEST, preferred_element_type=F32)
    lane_k = lax.broadcasted_iota(jnp.int32, (S5_GROUP, S5_CK), 1)
    row_k = lax.broadcasted_iota(jnp.int32, (S5_GROUP, S5_CK), 0)
    kt = kt + jnp.where(lane_k == row_k, d_c, 0.0)
    for s in range(t_chunk):
        off = S5_GROUP * s
        blk = kt if s == 0 else jnp.where(lane_k >= off, pltpu.roll(kt, off, 1), 0.0)
        toep_ref[gi, off:off + S5_GROUP, :] = blk.astype(BF16)

    e_s = (t_chunk - 1 - lax.broadcasted_iota(jnp.int32, (t_chunk, p2), 0)).astype(F32)
    pt_mag = jnp.exp(e_s * dt_r * are_r)
    pt_ang = e_s * dt_r * aim_r
    pt_re = pt_mag * jnp.cos(pt_ang)
    pt_im = pt_mag * jnp.sin(pt_ang)
    r_s = lax.broadcasted_iota(jnp.int32, (S5_CK, t_chunk), 0) // S5_GROUP
    e_rows = (r_s == lax.broadcasted_iota(jnp.int32, (S5_CK, t_chunk), 1)).astype(BF16)
    r_j = lax.broadcasted_iota(jnp.int32, (S5_CK, S5_GROUP), 0) % S5_GROUP
    e_rowj = (r_j == lax.broadcasted_iota(jnp.int32, (S5_CK, S5_GROUP), 1)).astype(BF16)
    gt = (_onehot_dot(e_rows, pt_re) * _onehot_dot(e_rowj, jnp.where(first, bb_re, bb_im))
          + _onehot_dot(e_rows, pt_im) * _onehot_dot(e_rowj, jnp.where(first, -bb_im, bb_re)))

    h = jnp.dot(u, gt.astype(BF16), preferred_element_type=F32)

    k_row = lax.broadcasted_iota(jnp.int32, (8, p2), 0)
    m_pow = (t_chunk * (1 << k_row)).astype(F32)
    m_mag = jnp.exp(m_pow * dt_r * are_r)
    m_ang = m_pow * dt_r * aim_r
    m_re = m_mag * jnp.cos(m_ang)
    m_im = m_mag * jnp.sin(m_ang)
    row_c = lax.broadcasted_iota(jnp.int32, (n_chunks, p2), 0)
    k = 0
    while (1 << k) < n_chunks:
        sh = 1 << k
        s_prev = jnp.where(row_c >= sh, pltpu.roll(h, sh, 0), 0.0)
        s_swap = pltpu.roll(s_prev, S5_STATE, 1)
        mb = jnp.where(first, -m_im[k:k + 1], m_im[k:k + 1])
        h = h + m_re[k:k + 1] * s_prev + mb * s_swap
        k += 1
    h_prev = jnp.where(row_c >= 1, pltpu.roll(h, 1, 0), 0.0)

    y = (jnp.dot(u, toep_ref[gi], preferred_element_type=F32)
         + jnp.dot(h_prev.astype(BF16), wt.astype(BF16), preferred_element_type=F32))
    return jax.nn.gelu(y).astype(BF16)


def s5_mix(u, a_re, a_im, log_dt, b_re, b_im, c_re, c_im, d):
    seq = u.shape[0]
    n_chunks = seq // S5_CHUNK
    g = S5_GROUPS
    ng = S5_GROUPS_PER_STEP
    steps = g // ng
    w = ng * S5_CK
    ug = u.reshape(n_chunks, S5_CHUNK, steps, 128).transpose(2, 0, 1, 3).reshape(steps, n_chunks, w)
    r = jnp.arange(ng * 128, dtype=jnp.int32)
    target = ((r // S5_GROUP) % ng) * 128 + (r // 128) * S5_GROUP + r % S5_GROUP
    perm = (r[None, :] == target[:, None]).astype(BF16)

    dup = lambda v: jnp.concatenate([v, v], axis=-1)
    zeros = jnp.zeros((g, 2 * S5_STATE), F32)
    ldt = jnp.broadcast_to(log_dt[:, None], (g, 2 * S5_STATE))
    rowp = jnp.stack([dup(a_re), dup(a_im), ldt] + [zeros] * 5, axis=1)
    d_pad = jnp.pad(d, ((0, 0), (0, 2 * S5_STATE - S5_GROUP)))
    colp = jnp.stack([dup(a_re), dup(a_im), ldt, d_pad] + [zeros] * 4, axis=2)
    bt = jnp.stack([dup(b_re.transpose(0, 2, 1)), dup(b_im.transpose(0, 2, 1))], axis=1)
    ctr, cti = c_re.transpose(0, 2, 1), c_im.transpose(0, 2, 1)
    ct = jnp.stack([jnp.concatenate([ctr, ctr], 1), jnp.concatenate([cti, cti], 1)], axis=1)

    yg = pl.pallas_call(
        _s5_kernel,
        grid=(steps,),
        in_specs=[pl.BlockSpec((1, n_chunks, w), lambda i: (i, 0, 0)),
                  pl.BlockSpec(perm.shape, lambda i: (0, 0)),
                  pl.BlockSpec((ng, 8, 2 * S5_STATE), lambda i: (i, 0, 0)),
                  pl.BlockSpec((ng, 2 * S5_STATE, 8), lambda i: (i, 0, 0)),
                  pl.BlockSpec((ng, 2, S5_GROUP, 2 * S5_STATE), lambda i: (i, 0, 0, 0)),
                  pl.BlockSpec((ng, 2, 2 * S5_STATE, S5_GROUP), lambda i: (i, 0, 0, 0))],
        out_specs=pl.BlockSpec((1, n_chunks, w), lambda i: (i, 0, 0)),
        out_shape=jax.ShapeDtypeStruct((steps, n_chunks, w), BF16),
        scratch_shapes=[pltpu.VMEM((ng, S5_CK, S5_CK), BF16)],
        compiler_params=_params(("parallel",)),
    )(ug, perm, rowp, colp, bt, ct)
    return yg.reshape(steps, n_chunks, S5_CHUNK, 128).transpose(1, 2, 0, 3).reshape(seq, S5_WIDTH)


def _mla_prep_kernel(mla_ref, pos_ref, ga_ref, gkv_ref, gq_ref, gkn_ref, gkr_ref,
                     invf_ref, wq_ref, wkv_ref, q_ref, k_ref, v_ref):
    cq = mla_ref[:, 0:CQ_W].astype(F32)
    lane = lax.broadcasted_iota(jnp.int32, cq.shape, 1)
    ssq = jnp.sum(jnp.where(lane < MLA_Q_RANK, cq * cq, 0.0), axis=-1, keepdims=True)
    hq = (cq * lax.rsqrt(ssq * (1.0 / MLA_Q_RANK) + EPS) * ga_ref[...]).astype(BF16)
    qf = jnp.dot(hq, wq_ref[...], preferred_element_type=F32)
    ckv = mla_ref[:, CKV_OFF:].astype(F32)
    lane_kv = lax.broadcasted_iota(jnp.int32, ckv.shape, 1)
    in_kv = jnp.logical_and(lane_kv >= CKV_LO, lane_kv < CKV_LO + MLA_KV_RANK)
    ssq_kv = jnp.sum(jnp.where(in_kv, ckv * ckv, 0.0), axis=-1, keepdims=True)
    hkv = (ckv * lax.rsqrt(ssq_kv * (1.0 / MLA_KV_RANK) + EPS) * gkv_ref[...]).astype(BF16)
    kvf = jnp.dot(hkv, wkv_ref[...], preferred_element_type=F32)

    l128 = lax.broadcasted_iota(jnp.int32, (cq.shape[0], 128), 1)
    half = MLA_ROPE // 2
    ang = pos_ref[...].astype(F32) * invf_ref[...]
    cosv = jnp.cos(ang)
    sinv = jnp.sin(ang)
    sgn_sin = jnp.where(l128 < half, -sinv, jnp.where(l128 < MLA_ROPE, sinv, 0.0))

    def rope(t):
        sw = jnp.where(l128 < half, pltpu.roll(t, 128 - half, 1), pltpu.roll(t, half, 1))
        return t * cosv + sw * sgn_sin

    kr = jnp.where(l128 < MLA_ROPE, pltpu.roll(mla_ref[:, KR_OFF:].astype(F32), MLA_ROPE, 1), 0.0)
    kr_ssq = jnp.sum(kr * kr, axis=-1, keepdims=True)
    kr_rot = rope(kr * gkr_ref[...])

    gq = gq_ref[...]
    gkn = gkn_ref[...]
    scale = MLA_QK ** -0.5
    inv_qk = 1.0 / MLA_QK
    ones_col = jnp.where(l128 == 0, 1.0, 0.0).astype(BF16)
    for h in range(MLA_HEADS):
        qh = qf[:, MLA_QK_PAD * h:MLA_QK_PAD * (h + 1)]
        rq = lax.rsqrt(jnp.sum(qh * qh, axis=-1, keepdims=True) * inv_qk + EPS)
        qn = qh * rq * gq
        q_ref[h] = (jnp.concatenate([qn[:, :MLA_NOPE], rope(qn[:, MLA_NOPE:])], axis=1) * scale).astype(BF16)
        kn = kvf[:, MLA_QK_PAD * h:MLA_QK_PAD * h + MLA_NOPE]
        rk = lax.rsqrt((jnp.sum(kn * kn, axis=-1, keepdims=True) + kr_ssq) * inv_qk + EPS)
        k_ref[h] = jnp.concatenate([kn * rk * gkn, kr_rot * rk], axis=1).astype(BF16)
        vh = kvf[:, MLA_QK_PAD * h + MLA_NOPE:MLA_QK_PAD * (h + 1)].astype(BF16)
        v_ref[h] = jnp.concatenate([vh, ones_col], axis=1)


def mla_prep(proj, pos_col, ga, gkv, gq, gkn, gkr, invf, wq, wkv, *, tm):
    seq = proj.shape[0]
    h = MLA_HEADS
    full = lambda shape: pl.BlockSpec(shape, lambda i: (0,) * len(shape))
    return pl.pallas_call(
        _mla_prep_kernel,
        grid=(seq // tm,),
        in_specs=[pl.BlockSpec((tm, MLA_W), lambda i: (i, PJ_MLA // MLA_W)),
                  pl.BlockSpec((tm, 1), lambda i: (i, 0)),
                  full((1, CQ_W)), full((1, CKV_W)), full((1, MLA_QK_PAD)), full((1, 128)), full((1, 128)),
                  full((1, 128)), full((CQ_W, h * MLA_QK_PAD)), full((CKV_W, h * MLA_QK_PAD))],
        out_specs=[pl.BlockSpec((h, tm, MLA_QK_PAD), lambda i: (0, i, 0)),
                   pl.BlockSpec((h, tm, MLA_QK_PAD), lambda i: (0, i, 0)),
                   pl.BlockSpec((h, tm, MLA_VP), lambda i: (0, i, 0))],
        out_shape=[jax.ShapeDtypeStruct((h, seq, MLA_QK_PAD), BF16),
                   jax.ShapeDtypeStruct((h, seq, MLA_QK_PAD), BF16),
                   jax.ShapeDtypeStruct((h, seq, MLA_VP), BF16)],
        compiler_params=_params(("parallel",)),
    )(proj, pos_col, ga, gkv, gq, gkn, gkr, invf, wq, wkv)


def _flash_kernel(q_ref, k_ref, v_ref, pq_ref, pk_ref, o_ref, sa, sb, pa, pb, ala, alb, m_ref, acc_ref, *, t):
    nq = q_ref.shape[1] // t
    nt = (((1,), (1,)), ((), ()))

    def blk(j):
        return pl.ds(pl.multiple_of(j * t, t), t)

    def score(qt, j, s_out):
        s_out[...] = lax.dot_general(q_ref[0, blk(qt), :], k_ref[0, blk(j), :], nt,
                                     preferred_element_type=F32)

    score(0, 0, sa)

    def tile(qi, carry):
        _flash_tile(qi, nq, blk, score, v_ref, pq_ref, pk_ref, o_ref, sa, sb, pa, pb, ala, alb, m_ref, acc_ref, t)
        return carry

    lax.fori_loop(0, nq, tile, 0)


def _flash_tile(qi, nq, blk, score, v_ref, pq_ref, pk_ref, o_ref, sa, sb, pa, pb, ala, alb, m_ref, acc_ref, t):
    def stage_a(j, s_out):
        score(qi, j, s_out)

    def stage_b(j, s_in, p_out, al_out, masked):
        s = s_in[...]
        if masked:
            s = jnp.where(pk_ref[:, blk(j)] <= pq_ref[blk(qi), :], s, NEG_INF)
        m_prev = m_ref[...]
        m_new = jnp.maximum(m_prev, jnp.max(s, axis=-1, keepdims=True))
        al_out[...] = jnp.exp(m_prev - m_new)
        p_out[...] = jnp.exp((s - jnp.concatenate([m_new] * (t // 128), axis=1)).astype(BF16))
        m_ref[...] = m_new

    def stage_c(j, p_in, al_in):
        al = al_in[...]
        acc_ref[...] = (jnp.concatenate([al] * (MLA_VP // 128), axis=1) * acc_ref[...]
                        + jnp.dot(p_in[...], v_ref[0, blk(j), :], preferred_element_type=F32))

    m_ref[...] = jnp.full(m_ref.shape, NEG_INF, F32)
    acc_ref[...] = jnp.zeros(acc_ref.shape, F32)
    pb[...] = jnp.zeros(pb.shape, BF16)
    alb[...] = jnp.ones(alb.shape, F32)
    nxt = jnp.minimum(qi + 1, nq - 1)

    def finish():
        acc = acc_ref[...]
        o_ref[blk(qi), :] = (acc[:, :MLA_V] / acc[:, MLA_V:MLA_V + 1]).astype(o_ref.dtype)

    def pair_body(j):
        stage_a(j + 1, sb)
        stage_b(j, sa, pa, ala, False)
        stage_c(jnp.maximum(j - 1, 0), pb, alb)
        stage_a(j + 2, sa)
        stage_b(j + 1, sb, pb, alb, False)
        stage_c(j, pa, ala)

    def quad(i, c):
        pair_body(4 * i)
        pair_body(4 * i + 2)
        return c

    lax.fori_loop(0, qi // 4, quad, 0)

    @pl.when((qi // 2) % 2 == 1)
    def _():
        pair_body(4 * (qi // 4))

    @pl.when(qi % 2 == 0)
    def _():
        stage_b(qi, sa, pa, ala, True)
        stage_c(jnp.maximum(qi - 1, 0), pb, alb)
        score(nxt, 0, sa)
        stage_c(qi, pa, ala)
        finish()

    @pl.when(qi % 2 == 1)
    def _():
        stage_a(qi, sb)
        stage_b(qi - 1, sa, pa, ala, False)
        stage_c(jnp.maximum(qi - 2, 0), pb, alb)
        score(nxt, 0, sa)
        stage_b(qi, sb, pb, alb, True)
        stage_c(qi - 1, pa, ala)
        stage_c(qi, pb, alb)
        finish()


def flash_attention(q, k, v, pos_col, pos_row, *, t):
    h, seq, _ = q.shape
    return pl.pallas_call(
        functools.partial(_flash_kernel, t=t),
        grid=(h,),
        in_specs=[pl.BlockSpec((1, seq, MLA_QK_PAD), lambda hh: (hh, 0, 0)),
                  pl.BlockSpec((1, seq, MLA_QK_PAD), lambda hh: (hh, 0, 0)),
                  pl.BlockSpec((1, seq, MLA_VP), lambda hh: (hh, 0, 0)),
                  pl.BlockSpec((seq, 1), lambda hh: (0, 0)),
                  pl.BlockSpec((1, seq), lambda hh: (0, 0))],
        out_specs=pl.BlockSpec((seq, MLA_V), lambda hh: (0, hh)),
        out_shape=jax.ShapeDtypeStruct((seq, h * MLA_V), BF16),
        scratch_shapes=[pltpu.VMEM((t, t), F32), pltpu.VMEM((t, t), F32),
                        pltpu.VMEM((t, t), BF16), pltpu.VMEM((t, t), BF16),
                        pltpu.VMEM((t, 128), F32), pltpu.VMEM((t, 128), F32),
                        pltpu.VMEM((t, 128), F32), pltpu.VMEM((t, MLA_VP), F32)],
        compiler_params=_params(("parallel",)),
    )(q, k, v, pos_col, pos_row)


def _mem_kv_kernel(mem_ref, g_ref, w_ref, gk_ref, k_ref, v_ref):
    m = _rms(mem_ref[...], g_ref[...]).astype(BF16)
    kv = jnp.dot(m, w_ref[...], preferred_element_type=F32)
    for h in range(MEM_HEADS):
        kh = kv[:, MEM_HEAD_DIM * h:MEM_HEAD_DIM * (h + 1)]
        k_ref[:, MEM_HEAD_DIM * h:MEM_HEAD_DIM * (h + 1)] = _rms(kh, gk_ref[...]).astype(BF16)
    v_ref[...] = kv[:, MEM_WIDTH:].astype(BF16)


def mem_kv(mem, g, w, gk):
    n_mem = mem.shape[0]
    full = lambda shape: pl.BlockSpec(shape, lambda i: (0,) * len(shape))
    return pl.pallas_call(
        _mem_kv_kernel,
        grid=(1,),
        in_specs=[full(mem.shape), full(g.shape), full(w.shape), full(gk.shape)],
        out_specs=[full((n_mem, MEM_WIDTH)), full((n_mem, MEM_WIDTH))],
        out_shape=[jax.ShapeDtypeStruct((n_mem, MEM_WIDTH), BF16)] * 2,
        compiler_params=_params(("arbitrary",)),
    )(mem, g, w, gk)


def _mem_attn_kernel(q_ref, k_ref, v_ref, gq_ref, o_ref):
    scale = MEM_HEAD_DIM ** -0.5
    for h in range(MEM_HEADS):
        sl = slice(MEM_HEAD_DIM * h, MEM_HEAD_DIM * (h + 1))
        qh = (_rms(q_ref[:, sl].astype(F32), gq_ref[...]) * scale).astype(BF16)
        s = lax.dot_general(qh, k_ref[:, sl], (((1,), (1,)), ((), ())), preferred_element_type=F32)
        p = jnp.exp(s - jnp.max(s, axis=-1, keepdims=True))
        o = jnp.dot(p.astype(BF16), v_ref[:, sl], preferred_element_type=F32)
        o_ref[:, sl] = (o / jnp.sum(p, axis=-1, keepdims=True)).astype(o_ref.dtype)


def mem_attention(proj, k, v, gq, *, tm):
    seq = proj.shape[0]
    n_mem = k.shape[0]
    return pl.pallas_call(
        _mem_attn_kernel,
        grid=(seq // tm,),
        in_specs=[pl.BlockSpec((tm, MEM_WIDTH), lambda i: (i, PJ_MQ // MEM_WIDTH)),
                  pl.BlockSpec((n_mem, MEM_WIDTH), lambda i: (0, 0)),
                  pl.BlockSpec((n_mem, MEM_WIDTH), lambda i: (0, 0)),
                  pl.BlockSpec((1, MEM_HEAD_DIM), lambda i: (0, 0))],
        out_specs=pl.BlockSpec((tm, MEM_WIDTH), lambda i: (i, 0)),
        out_shape=jax.ShapeDtypeStruct((seq, MEM_WIDTH), BF16),
        compiler_params=_params(("parallel",)),
    )(proj, k, v, gq)


def _ffn_kernel(x_ref, g_ref, wg_ref, wu_ref, wd_ref, o_ref, h_ref, acc_ref):
    f = pl.program_id(1)

    @pl.when(f == 0)
    def _():
        h_ref[...] = _rms(x_ref[...], g_ref[...]).astype(BF16)
        acc_ref[...] = jnp.zeros_like(acc_ref)

    h = h_ref[...]
    gate = jnp.dot(h, wg_ref[...], preferred_element_type=F32)
    up = jnp.dot(h, wu_ref[...], preferred_element_type=F32)
    a = (jax.nn.silu(gate) * up).astype(BF16)
    acc_ref[...] += jnp.dot(a, wd_ref[...], preferred_element_type=F32)

    @pl.when(f == pl.num_programs(1) - 1)
    def _():
        o_ref[...] = x_ref[...] + acc_ref[...]


def dense_ffn(x, g, w_gate_up, w_down, *, tm, tf):
    m, d = x.shape
    nf = D_FF // tf
    return pl.pallas_call(
        _ffn_kernel,
        grid=(m // tm, nf),
        in_specs=[pl.BlockSpec((tm, d), lambda i, f: (i, 0)),
                  pl.BlockSpec((1, d), lambda i, f: (0, 0)),
                  pl.BlockSpec((d, tf), lambda i, f: (0, f)),
                  pl.BlockSpec((d, tf), lambda i, f: (0, f + nf)),
                  pl.BlockSpec((tf, d), lambda i, f: (f, 0))],
        out_specs=pl.BlockSpec((tm, d), lambda i, f: (i, 0)),
        out_shape=jax.ShapeDtypeStruct((m, d), F32),
        scratch_shapes=[pltpu.VMEM((tm, d), BF16), pltpu.VMEM((tm, d), F32)],
        compiler_params=_params(("parallel", "arbitrary")),
    )(x, g, w_gate_up, w_gate_up, w_down)


def _router_kernel(x_ref, g_ref, w_ref, b_ref, idx_ref, wgt_ref):
    h = _rms(x_ref[...], g_ref[...])
    logits = jnp.dot(h, w_ref[...], precision=HIGHEST, preferred_element_type=F32) + b_ref[...]
    lane = lax.broadcasted_iota(jnp.int32, logits.shape, 1)
    logits = jnp.where(lane < N_EXPERTS, logits, -jnp.inf)
    v1 = jnp.max(logits, axis=-1, keepdims=True)
    i1 = jnp.min(jnp.where(logits == v1, lane, 128), axis=-1, keepdims=True)
    rest = jnp.where(lane == i1, -jnp.inf, logits)
    v2 = jnp.max(rest, axis=-1, keepdims=True)
    i2 = jnp.min(jnp.where(rest == v2, lane, 128), axis=-1, keepdims=True)
    e2 = jnp.exp(v2 - v1)
    w1 = 1.0 / (1.0 + e2)
    w2 = e2 / (1.0 + e2)
    idx_ref[...] = jnp.where(lane == 0, i1, jnp.where(lane == 1, i2, 0))
    wgt_ref[...] = jnp.where(lane == 0, w1, jnp.where(lane == 1, w2, 0.0))


def moe_router(x, g, w_pad, b_pad, *, tm):
    m, d = x.shape
    return pl.pallas_call(
        _router_kernel,
        grid=(m // tm,),
        in_specs=[pl.BlockSpec((tm, d), lambda i: (i, 0)),
                  pl.BlockSpec((1, d), lambda i: (0, 0)),
                  pl.BlockSpec((d, 128), lambda i: (0, 0)),
                  pl.BlockSpec((1, 128), lambda i: (0, 0))],
        out_specs=[pl.BlockSpec((tm, 128), lambda i: (i, 0)),
                   pl.BlockSpec((tm, 128), lambda i: (i, 0))],
        out_shape=[jax.ShapeDtypeStruct((m, 128), jnp.int32),
                   jax.ShapeDtypeStruct((m, 128), F32)],
        compiler_params=_params(("parallel",)),
    )(x, g, w_pad, b_pad)


MOE_ROW_STEPS = 4


def _moe_ffn_kernel(te_ref, tr_ref, src_ref, x_hbm, g_ref, wg_ref, wu_ref, wd_ref, o_ref, xbuf, h_ref, sem):
    i = pl.program_id(0)
    f = pl.program_id(1)
    n_tiles = pl.num_programs(0)
    rows = tr_ref[i]
    tm = h_ref.shape[0]
    per_step = xbuf.shape[0] // pl.num_programs(1)

    def row_copy(idx, r):
        return pltpu.make_async_copy(x_hbm.at[pl.ds(idx, 1), :], xbuf.at[pl.ds(r, 1), :], sem)

    def start_rows(tile, step):
        for k in range(per_step):
            r = step * per_step + k
            row_copy(src_ref[tile * tm + jnp.minimum(r, tm - 1)], r).start()

    def wait_all():
        def wait_row(r, c):
            row_copy(0, r).wait()
            return c

        lax.fori_loop(0, xbuf.shape[0], wait_row, 0, unroll=4)

    is_first = i == 0
    is_last = i == n_tiles - 1
    nxt = jnp.minimum(i + 1, n_tiles - 1)
    prev_rows = tr_ref[jnp.maximum(i - 1, 0)]

    @pl.when(f == 0)
    def _():
        o_ref[...] = jnp.zeros_like(o_ref)

        @pl.when(jnp.logical_and(is_first, rows > 0))
        def _():
            def first(step, c):
                start_rows(0, step)
                return c

            lax.fori_loop(0, pl.num_programs(1), first, 0)

        @pl.when(jnp.where(is_first, rows, prev_rows) > 0)
        def _():
            wait_all()

        @pl.when(rows > 0)
        def _():
            h_ref[...] = _rms(xbuf[0:tm, :], g_ref[...]).astype(BF16)

    def run(n):
        start_rows(nxt, f)
        h = h_ref[0:n, :]
        gate = jnp.dot(h, wg_ref[0].astype(BF16), preferred_element_type=F32)
        up = jnp.dot(h, wu_ref[0].astype(BF16), preferred_element_type=F32)
        a = (jax.nn.silu(gate) * up).astype(BF16)
        o_ref[0:n, :] += jnp.dot(a, wd_ref[0].astype(BF16), preferred_element_type=F32)

    step = tm // MOE_ROW_STEPS
    for k in range(1, MOE_ROW_STEPS + 1):
        @pl.when(jnp.logical_and(rows > (k - 1) * step, rows <= k * step))
        def _(n=k * step):
            run(n)

    @pl.when(jnp.logical_and(jnp.logical_and(is_last, f == pl.num_programs(1) - 1), rows > 0))
    def _():
        wait_all()


def moe_ffn(x, g, src, tile_expert, tile_rows, w_gate_up, w_down, *, tm, tf):
    d = x.shape[1]
    p = src.shape[0]
    nf = D_FF // tf
    last = nf - 1
    per_step = -(-tm // nf)
    while (nf * per_step) % 8:
        per_step += 1
    gather_rows = nf * per_step

    def fsel(i, f, tr):
        return jnp.where(tr[i] > 0, f, last)

    return pl.pallas_call(
        _moe_ffn_kernel,
        grid_spec=pltpu.PrefetchScalarGridSpec(
            num_scalar_prefetch=3,
            grid=(p // tm, nf),
            in_specs=[pl.BlockSpec(memory_space=pl.ANY),
                      pl.BlockSpec((1, d), lambda i, f, te, tr, src: (0, 0)),
                      pl.BlockSpec((1, d, tf), lambda i, f, te, tr, src: (te[i], 0, fsel(i, f, tr))),
                      pl.BlockSpec((1, d, tf), lambda i, f, te, tr, src: (te[i], 0, fsel(i, f, tr) + nf)),
                      pl.BlockSpec((1, tf, d), lambda i, f, te, tr, src: (te[i], fsel(i, f, tr), 0))],
            out_specs=pl.BlockSpec((tm, d), lambda i, f, te, tr, src: (i, 0)),
            scratch_shapes=[pltpu.VMEM((gather_rows, d), F32), pltpu.VMEM((tm, d), BF16),
                            pltpu.SemaphoreType.DMA(())]),
        out_shape=jax.ShapeDtypeStruct((p, d), F32),
        compiler_params=_params(("arbitrary", "arbitrary")),
    )(tile_expert, tile_rows, src, x, g, w_gate_up, w_gate_up, w_down)


def _combine_kernel(p0_ref, p1_ref, x_ref, w_ref, ys_hbm, o_ref, buf, sem, *, tt):
    base = pl.program_id(0) * tt

    def issue(j, c):
        r0 = p0_ref[base + j]
        r1 = p1_ref[base + j]
        pltpu.make_async_copy(ys_hbm.at[pl.ds(r0, 1), :], buf.at[0, pl.ds(j, 1), :], sem.at[0]).start()
        pltpu.make_async_copy(ys_hbm.at[pl.ds(r1, 1), :], buf.at[1, pl.ds(j, 1), :], sem.at[1]).start()
        return c

    lax.fori_loop(0, tt, issue, 0, unroll=8)

    def drain(j, c):
        pltpu.make_async_copy(ys_hbm.at[pl.ds(0, 1), :], buf.at[0, pl.ds(j, 1), :], sem.at[0]).wait()
        pltpu.make_async_copy(ys_hbm.at[pl.ds(0, 1), :], buf.at[1, pl.ds(j, 1), :], sem.at[1]).wait()
        return c

    lax.fori_loop(0, tt, drain, 0, unroll=8)
    w = w_ref[...]
    o_ref[...] = x_ref[...] + (w[:, 0:1] * buf[0] + w[:, 1:2] * buf[1])


def moe_combine(x, wgt, ys, pos0, pos1, *, tt):
    m, d = x.shape
    return pl.pallas_call(
        functools.partial(_combine_kernel, tt=tt),
        grid_spec=pltpu.PrefetchScalarGridSpec(
            num_scalar_prefetch=2,
            grid=(m // tt,),
            in_specs=[pl.BlockSpec((tt, d), lambda i, p0, p1: (i, 0)),
                      pl.BlockSpec((tt, 128), lambda i, p0, p1: (i, 0)),
                      pl.BlockSpec(memory_space=pl.ANY)],
            out_specs=pl.BlockSpec((tt, d), lambda i, p0, p1: (i, 0)),
            scratch_shapes=[pltpu.VMEM((2, tt, d), F32), pltpu.SemaphoreType.DMA((2,))]),
        out_shape=jax.ShapeDtypeStruct((m, d), F32),
        compiler_params=_params(("arbitrary",)),
    )(pos0, pos1, x, wgt, ys)


def moe_layer(x, g, router, router_b, w_gate_up, w_down, expert_base, *, tm, tf):
    seq, d = x.shape
    w_pad = jnp.pad(router, ((0, 0), (0, 128 - N_EXPERTS)))
    b_pad = jnp.pad(router_b, (0, 128 - N_EXPERTS))[None, :]
    idx, wgt = moe_router(x, g, w_pad, b_pad, tm=min(512, seq))

    e_flat = idx[:, :TOP_K].reshape(-1)
    onehot = (e_flat[:, None] == jnp.arange(N_EXPERTS, dtype=jnp.int32)[None, :]).astype(jnp.int32)
    rank = jnp.cumsum(onehot, axis=0) - onehot
    counts = jnp.sum(onehot, axis=0)
    tiles_per = (counts + tm - 1) // tm
    tile_ends = jnp.cumsum(tiles_per)
    tile_starts = tile_ends - tiles_per
    pos = jnp.sum(onehot * (tile_starts[None, :] * tm + rank), axis=1).astype(jnp.int32)
    n_tiles = TOP_K * seq // tm + N_EXPERTS
    p_total = n_tiles * tm
    tok = jnp.arange(TOP_K * seq, dtype=jnp.int32) // TOP_K
    src = jnp.zeros((p_total,), jnp.int32).at[pos].set(tok)
    tile_idx = jnp.arange(n_tiles, dtype=jnp.int32)
    tile_expert = jnp.minimum(jnp.sum((tile_idx[:, None] >= tile_ends[None, :]).astype(jnp.int32), axis=1),
                              N_EXPERTS - 1).astype(jnp.int32)
    valid = tile_idx < tile_ends[-1]
    remaining = counts[tile_expert] - (tile_idx - tile_starts[tile_expert]) * tm
    tile_rows = jnp.where(valid, jnp.minimum(remaining, tm), 0).astype(jnp.int32)
    tile_expert = jnp.where(valid, tile_expert, jnp.max(jnp.where(valid, tile_expert, 0))).astype(jnp.int32)

    ys = moe_ffn(x, g, src, tile_expert + expert_base, tile_rows, w_gate_up, w_down, tm=tm, tf=tf)
    pos2 = pos.reshape(seq, TOP_K)
    return moe_combine(x, wgt, ys, pos2[:, 0], pos2[:, 1], tt=min(256, seq))


def _repack_w_q_b(w):
    w = w.reshape(MLA_Q_RANK, MLA_HEADS, MLA_QK)
    w = jnp.pad(w, ((0, CQ_W - MLA_Q_RANK), (0, 0), (0, MLA_QK_PAD - MLA_QK)))
    return w.reshape(CQ_W, MLA_HEADS * MLA_QK_PAD).astype(BF16)


def mixer_layer(x, mem, pos_col, pos_row, invf, p):
    seq = x.shape[0]
    tm = min(1024, seq)
    row = lambda v: v[None, :]
    proj = norm_matmul(x, row(p['norm_mix']), p['w_in'].astype(BF16), tm=tm, tn=1024)

    y = s5_mix(proj[:, PJ_U:PJ_U + S5_WIDTH], p['s5_a_re'], p['s5_a_im'], p['s5_log_dt'], p['s5_b_re'],
               p['s5_b_im'], p['s5_c_re'], p['s5_c_im'], p['s5_d'])

    ga = row(jnp.pad(p['mla_q_a_norm'], (0, CQ_W - MLA_Q_RANK)))
    kv_pad = (CKV_LO, CKV_W - CKV_LO - MLA_KV_RANK)
    gkv = row(jnp.pad(p['mla_kv_norm'], kv_pad))
    gq = row(jnp.pad(p['mla_q_norm'], (0, MLA_QK_PAD - MLA_QK)))
    gkn = row(p['mla_k_norm'][:MLA_NOPE])
    gkr = row(jnp.pad(p['mla_k_norm'][MLA_NOPE:], (0, 128 - MLA_ROPE)))
    wkv = jnp.pad(p['mla_w_kv_b'], (kv_pad, (0, 0))).astype(BF16)
    q, k, v = mla_prep(proj, pos_col, ga, gkv, gq, gkn, gkr, invf,
                       _repack_w_q_b(p['mla_w_q_b']), wkv, tm=min(256, seq))
    o = flash_attention(q, k, v, pos_col, pos_row, t=min(512, seq))

    km, vm = mem_kv(mem, row(p['mem_norm']), p['mem_w_kv'].astype(BF16), row(p['mem_k_norm']))
    om = mem_attention(proj, km, vm, row(p['mem_q_norm']), tm=min(512, seq))
    merged = merge_branches(y, o, om, p['s5_w_glu'].astype(BF16), p['mla_w_o'].astype(BF16),
                            p['mem_w_o'].astype(BF16), proj, tm=tm, tn=512)

    return resid_matmul(merged, p['w_out'].astype(BF16), x, tm=tm, tn=1024)


_LAYER_KEYS = ('norm_mix', 'w_in', 's5_a_re', 's5_a_im', 's5_log_dt', 's5_b_re', 's5_b_im', 's5_c_re',
               's5_c_im', 's5_d', 's5_w_glu', 'mla_q_a_norm', 'mla_w_q_b', 'mla_kv_norm', 'mla_w_kv_b',
               'mla_q_norm', 'mla_k_norm', 'mla_w_o', 'mem_norm', 'mem_w_kv', 'mem_q_norm', 'mem_k_norm',
               'mem_w_o', 'w_out')


def kernel(x, mem, positions, norm_mix, w_in, s5_a_re, s5_a_im, s5_log_dt, s5_b_re, s5_b_im, s5_c_re, s5_c_im, s5_d, s5_w_glu, mla_q_a_norm, mla_w_q_b, mla_kv_norm, mla_w_kv_b, mla_q_norm, mla_k_norm, mla_w_o, mem_norm, mem_w_kv, mem_q_norm, mem_k_norm, mem_w_o, w_out, norm_ffn, ffn_w_gate_up, ffn_w_down, moe_router, moe_router_b, moe_w_gate_up, moe_w_down):
    stacked = dict(zip(_LAYER_KEYS, (norm_mix, w_in, s5_a_re, s5_a_im, s5_log_dt, s5_b_re, s5_b_im, s5_c_re,
                                     s5_c_im, s5_d, s5_w_glu, mla_q_a_norm, mla_w_q_b, mla_kv_norm,
                                     mla_w_kv_b, mla_q_norm, mla_k_norm, mla_w_o, mem_norm, mem_w_kv,
                                     mem_q_norm, mem_k_norm, mem_w_o, w_out)))
    bsz, seq, d = x.shape
    depth = norm_mix.shape[0]
    half = MLA_ROPE // 2
    inv_freq = ROPE_THETA ** (-jnp.arange(0, MLA_ROPE, 2, dtype=F32) / MLA_ROPE)
    invf = jnp.concatenate([inv_freq, inv_freq, jnp.zeros((128 - 2 * half,), F32)])[None, :]
    moe_gu = moe_w_gate_up.reshape((-1,) + moe_w_gate_up.shape[2:])
    moe_dn = moe_w_down.reshape((-1,) + moe_w_down.shape[2:])

    outs = []
    for b in range(bsz):
        xb = x[b]
        pos_col = positions[b][:, None]
        pos_row = positions[b][None, :]
        for l in range(depth):
            p = {key: val[l] for key, val in stacked.items()}
            xb = mixer_layer(xb, mem[b], pos_col, pos_row, invf, p)
            gf = norm_ffn[l][None, :]
            if l % 2 == 0:
                xb = dense_ffn(xb, gf, ffn_w_gate_up[l // 2].astype(BF16), ffn_w_down[l // 2].astype(BF16),
                               tm=min(512, seq), tf=512)
            else:
                xb = moe_layer(xb, gf, moe_router[l // 2], moe_router_b[l // 2], moe_gu, moe_dn,
                               (l // 2) * N_EXPERTS, tm=min(1024, TOP_K * seq), tf=256)
        outs.append(xb)
    return jnp.stack(outs, axis=0)
```

```python
import functools
import math

import jax
import jax.numpy as jnp
from jax import lax
from jax.experimental import pallas as pl
from jax.experimental.pallas import tpu as pltpu

F32 = jnp.float32
BF16 = jnp.bfloat16
HIGHEST = lax.Precision.HIGHEST

D_MODEL = 2048
N_MEM = 256
S5_WIDTH = 1024
S5_GROUP = 16
S5_GROUPS = S5_WIDTH // S5_GROUP
S5_STATE = 64
S5_CHUNK = 32
S5_CK = S5_CHUNK * S5_GROUP
S5_GROUPS_PER_STEP = 128 // S5_GROUP
MLA_HEADS = 16
MLA_Q_RANK = 448
MLA_KV_RANK = 512
MLA_NOPE = 128
MLA_ROPE = 64
MLA_QK = MLA_NOPE + MLA_ROPE
MLA_V = 128
MLA_QK_PAD = 256
MLA_VP = 256
ROPE_THETA = 10000.0
MEM_HEADS = 4
MEM_HEAD_DIM = 256
MEM_WIDTH = MEM_HEADS * MEM_HEAD_DIM
N_BRANCH = 3
D_FF = 7168
N_EXPERTS = 8
TOP_K = 2
EPS = 1e-6
NEG_INF = -1e30

PJ_U = 0
PJ_MLA = 1024
PJ_MQ = 2048
PJ_GATE = 3072
MLA_W = MLA_Q_RANK + MLA_KV_RANK + MLA_ROPE
CQ_W = 512
CKV_OFF = 384
CKV_W = MLA_W - CKV_OFF
CKV_LO = MLA_Q_RANK - CKV_OFF
KR_OFF = MLA_W - 128

VMEM_LIMIT = 52 * 1024 * 1024


def _params(sem):
    return pltpu.CompilerParams(dimension_semantics=sem, vmem_limit_bytes=VMEM_LIMIT)


def _rms(x, g):
    r = lax.rsqrt(jnp.mean(x * x, axis=-1, keepdims=True) + EPS)
    return x * r * g


def _onehot_dot(a, b):
    if a.dtype == BF16:
        hi = b.astype(BF16)
        lo = (b - hi.astype(F32)).astype(BF16)
        return (jnp.dot(a, hi, preferred_element_type=F32) + jnp.dot(a, lo, preferred_element_type=F32))
    hi = a.astype(BF16)
    lo = (a - hi.astype(F32)).astype(BF16)
    return (jnp.dot(hi, b, preferred_element_type=F32) + jnp.dot(lo, b, preferred_element_type=F32))


def _norm_mm_kernel(x_ref, g_ref, w_ref, o_ref, first_ref, h_ref):
    @pl.when(pl.program_id(1) == 0)
    def _():
        h_ref[...] = _rms(x_ref[...], g_ref[...]).astype(BF16)

    res = jnp.dot(h_ref[...], w_ref[...], preferred_element_type=F32).astype(o_ref.dtype)
    o_ref[...] = res

    @pl.when(pl.program_id(1) == 0)
    def _():
        first_ref[...] = res


def norm_matmul(x, g, w, *, tm, tn, out_dtype=BF16):
    m, k = x.shape
    n = w.shape[1]
    return pl.pallas_call(
        _norm_mm_kernel,
        grid=(m // tm, n // tn),
        in_specs=[pl.BlockSpec((tm, k), lambda i, j: (i, 0)),
                  pl.BlockSpec((1, k), lambda i, j: (0, 0)),
                  pl.BlockSpec((k, tn), lambda i, j: (0, j))],
        out_specs=[pl.BlockSpec((tm, tn), lambda i, j: (i, j)),
                   pl.BlockSpec((tm, tn), lambda i, j: (i, 0))],
        out_shape=[jax.ShapeDtypeStruct((m, n), out_dtype),
                   jax.ShapeDtypeStruct((m, tn), out_dtype)],
        scratch_shapes=[pltpu.VMEM((tm, k), BF16)],
        compiler_params=_params(("parallel", "arbitrary")),
    )(x, g, w)


def _merge_kernel(y_ref, o_ref, om_ref, wa_ref, wb_ref, wo_ref, wm_ref, g0_ref, g1_ref, g2_ref, out_ref):
    y = y_ref[...]
    s5 = (jnp.dot(y, wa_ref[...], preferred_element_type=F32)
          * jax.nn.sigmoid(jnp.dot(y, wb_ref[...], preferred_element_type=F32)))
    mla = jnp.dot(o_ref[...], wo_ref[...], preferred_element_type=F32)
    mem = jnp.dot(om_ref[...], wm_ref[...], preferred_element_type=F32)
    gate = lambda r: jax.nn.sigmoid(r[...].astype(F32))
    out_ref[...] = (gate(g0_ref) * s5 + gate(g1_ref) * mla + gate(g2_ref) * mem).astype(out_ref.dtype)


def merge_branches(y, o, om, w_glu, w_o, w_mo, proj, *, tm, tn):
    m = y.shape[0]
    n = w_o.shape[1]
    nb = n // tn
    gb = PJ_GATE // tn
    lhs = lambda a: pl.BlockSpec((tm, a.shape[1]), lambda i, j: (i, 0))
    col = lambda w, off: pl.BlockSpec((w.shape[0], tn), lambda i, j: (0, j + off))
    gate = lambda b: pl.BlockSpec((tm, tn), lambda i, j: (i, gb + b * nb + j))
    return pl.pallas_call(
        _merge_kernel,
        grid=(m // tm, nb),
        in_specs=[lhs(y), lhs(o), lhs(om), col(w_glu, 0), col(w_glu, nb), col(w_o, 0), col(w_mo, 0),
                  gate(0), gate(1), gate(2)],
        out_specs=pl.BlockSpec((tm, tn), lambda i, j: (i, j)),
        out_shape=jax.ShapeDtypeStruct((m, n), BF16),
        compiler_params=_params(("parallel", "parallel")),
    )(y, o, om, w_glu, w_glu, w_o, w_mo, proj, proj, proj)


def _resid_mm_kernel(a_ref, w_ref, x_ref, o_ref):
    o_ref[...] = x_ref[...] + jnp.dot(a_ref[...], w_ref[...], preferred_element_type=F32)


def resid_matmul(a, w, x, *, tm, tn):
    m, k = a.shape
    n = w.shape[1]
    return pl.pallas_call(
        _resid_mm_kernel,
        grid=(m // tm, n // tn),
        in_specs=[pl.BlockSpec((tm, k), lambda i, j: (i, 0)),
                  pl.BlockSpec((k, tn), lambda i, j: (0, j)),
                  pl.BlockSpec((tm, tn), lambda i, j: (i, j))],
        out_specs=pl.BlockSpec((tm, tn), lambda i, j: (i, j)),
        out_shape=jax.ShapeDtypeStruct((m, n), F32),
        compiler_params=_params(("parallel", "parallel")),
    )(a, w, x)


def _s5_kernel(u_ref, perm_ref, rowp_ref, colp_ref, bt_ref, ct_ref, y_ref, toep_ref):
    ng = S5_GROUPS_PER_STEP
    w = ng * 128
    n_tiles = S5_CK // 128
    perm = perm_ref[...]
    parts = [jnp.dot(u_ref[0, :, a * w:(a + 1) * w], perm, preferred_element_type=F32).astype(BF16)
             for a in range(n_tiles)]
    ys = []
    for gi in range(ng):
        u = jnp.concatenate([part[:, gi * 128:(gi + 1) * 128] for part in parts], axis=1)
        ys.append(_s5_group(gi, u, rowp_ref, colp_ref, bt_ref, ct_ref, toep_ref))
    for a in range(n_tiles):
        ycat = jnp.concatenate([y[:, a * 128:(a + 1) * 128] for y in ys], axis=1)
        y_ref[0, :, a * w:(a + 1) * w] = lax.dot_general(
            ycat, perm, (((1,), (1,)), ((), ())), preferred_element_type=F32).astype(y_ref.dtype)


def _s5_group(gi, u, rowp_ref, colp_ref, bt_ref, ct_ref, toep_ref):
    t_chunk = S5_CHUNK
    n_chunks = u.shape[0]
    p2 = 2 * S5_STATE

    rowp = rowp_ref[gi]
    are_r, aim_r = rowp[0:1], rowp[1:2]
    dt_r = jnp.exp(rowp[2:3])
    colp = colp_ref[gi]
    are_c, aim_c = colp[:, 0:1], colp[:, 1:2]
    dt_c = jnp.exp(colp[:, 2:3])
    d_c = colp[0:S5_GROUP, 3:4]

    row_q = lax.broadcasted_iota(jnp.int32, (p2, S5_CK), 0)
    lane_q = lax.broadcasted_iota(jnp.int32, (1, p2), 1)

    tau = lax.broadcasted_iota(jnp.int32, (p2, p2), 1).astype(F32)
    row_pp = lax.broadcasted_iota(jnp.int32, (p2, p2), 0)
    mag = jnp.exp(tau * dt_c * are_c)
    ang = tau * dt_c * aim_c
    pw = mag * jnp.where(row_pp < S5_STATE, jnp.cos(ang), jnp.sin(ang))

    l_t = lax.broadcasted_iota(jnp.int32, (p2, S5_CK), 1) // S5_GROUP
    e_tau = (l_t == row_q).astype(BF16)
    e_tau1 = (l_t + 1 == row_q).astype(BF16)
    l_i = lax.broadcasted_iota(jnp.int32, (S5_GROUP, S5_CK), 1) % S5_GROUP
    e_ch = (l_i == lax.broadcasted_iota(jnp.int32, (S5_GROUP, S5_CK), 0)).astype(BF16)

    def swap_halves(x):
        return jnp.concatenate([x[S5_STATE:], x[:S5_STATE]], axis=0)

    x1 = _onehot_dot(pw, e_tau)
    x1s = _onehot_dot(pw, e_tau1)
    ct = ct_ref[gi]
    cta = _onehot_dot(ct[0], e_ch)
    ctb = _onehot_dot(ct[1], e_ch)
    ctb = jnp.where(row_q < S5_STATE, -ctb, ctb)
    z = cta * x1 + ctb * swap_halves(x1)
    zs = cta * x1s + ctb * swap_halves(x1s)
    wt = jnp.where(row_q < S5_STATE, zs, -zs)

    lam_mag = jnp.exp(dt_r * are_r)
    lam_re = lam_mag * jnp.cos(dt_r * aim_r)
    lam_im = lam_mag * jnp.sin(dt_r * aim_r)
    den = are_r * are_r + aim_r * aim_r
    n_re = lam_re - 1.0
    f_re = (n_re * are_r + lam_im * aim_r) / den
    f_im = (lam_im * are_r - n_re * aim_r) / den
    bt = bt_ref[gi]
    bb_re = f_re * bt[0] - f_im * bt[1]
    bb_im = f_re * bt[1] + f_im * bt[0]
    first = lane_q < S5_STATE

    kt = jnp.dot(jnp.where(first, bb_re, -bb_im), z, precision=HIGHEST, preferred_element_type=F32)
    lane_k = lax.broadcasted_iota(jnp.int32, (S5_GROUP, S5_CK), 1)
    row_k = lax.broadcasted_iota(jnp.int32, (S5_GROUP, S5_CK), 0)
    kt = kt + jnp.where(lane_k == row_k, d_c, 0.0)
    for s in range(t_chunk):
        off = S5_GROUP * s
        blk = kt if s == 0 else jnp.where(lane_k >= off, pltpu.roll(kt, off, 1), 0.0)
        toep_ref[gi, off:off + S5_GROUP, :] = blk.astype(BF16)

    e_s = (t_chunk - 1 - lax.broadcasted_iota(jnp.int32, (t_chunk, p2), 0)).astype(F32)
    pt_mag = jnp.exp(e_s * dt_r * are_r)
    pt_ang = e_s * dt_r * aim_r
    pt_re = pt_mag * jnp.cos(pt_ang)
    pt_im = pt_mag * jnp.sin(pt_ang)
    r_s = lax.broadcasted_iota(jnp.int32, (S5_CK, t_chunk), 0) // S5_GROUP
    e_rows = (r_s == lax.broadcasted_iota(jnp.int32, (S5_CK, t_chunk), 1)).astype(BF16)
    r_j = lax.broadcasted_iota(jnp.int32, (S5_CK, S5_GROUP), 0) % S5_GROUP
    e_rowj = (r_j == lax.broadcasted_iota(jnp.int32, (S5_CK, S5_GROUP), 1)).astype(BF16)
    gt = (_onehot_dot(e_rows, pt_re) * _onehot_dot(e_rowj, jnp.where(first, bb_re, bb_im))
          + _onehot_dot(e_rows, pt_im) * _onehot_dot(e_rowj, jnp.where(first, -bb_im, bb_re)))

    h = jnp.dot(u, gt.astype(BF16), preferred_element_type=F32)

    k_row = lax.broadcasted_iota(jnp.int32, (8, p2), 0)
    m_pow = (t_chunk * (1 << k_row)).astype(F32)
    m_mag = jnp.exp(m_pow * dt_r * are_r)
    m_ang = m_pow * dt_r * aim_r
    m_re = m_mag * jnp.cos(m_ang)
    m_im = m_mag * jnp.sin(m_ang)
    row_c = lax.broadcasted_iota(jnp.int32, (n_chunks, p2), 0)
    k = 0
    while (1 << k) < n_chunks:
        sh = 1 << k
        s_prev = jnp.where(row_c >= sh, pltpu.roll(h, sh, 0), 0.0)
        s_swap = pltpu.roll(s_prev, S5_STATE, 1)
        mb = jnp.where(first, -m_im[k:k + 1], m_im[k:k + 1])
        h = h + m_re[k:k + 1] * s_prev + mb * s_swap
        k += 1
    h_prev = jnp.where(row_c >= 1, pltpu.roll(h, 1, 0), 0.0)

    y = (jnp.dot(u, toep_ref[gi], preferred_element_type=F32)
         + jnp.dot(h_prev.astype(BF16), wt.astype(BF16), preferred_element_type=F32))
    return jax.nn.gelu(y).astype(BF16)


def s5_mix(u, a_re, a_im, log_dt, b_re, b_im, c_re, c_im, d):
    seq = u.shape[0]
    n_chunks = seq // S5_CHUNK
    g = S5_GROUPS
    ng = S5_GROUPS_PER_STEP
    steps = g // ng
    w = ng * S5_CK
    ug = u.reshape(n_chunks, S5_CHUNK, steps, 128).transpose(2, 0, 1, 3).reshape(steps, n_chunks, w)
    r = jnp.arange(ng * 128, dtype=jnp.int32)
    target = ((r // S5_GROUP) % ng) * 128 + (r // 128) * S5_GROUP + r % S5_GROUP
    perm = (r[None, :] == target[:, None]).astype(BF16)

    dup = lambda v: jnp.concatenate([v, v], axis=-1)
    zeros = jnp.zeros((g, 2 * S5_STATE), F32)
    ldt = jnp.broadcast_to(log_dt[:, None], (g, 2 * S5_STATE))
    rowp = jnp.stack([dup(a_re), dup(a_im), ldt] + [zeros] * 5, axis=1)
    d_pad = jnp.pad(d, ((0, 0), (0, 2 * S5_STATE - S5_GROUP)))
    colp = jnp.stack([dup(a_re), dup(a_im), ldt, d_pad] + [zeros] * 4, axis=2)
    bt = jnp.stack([dup(b_re.transpose(0, 2, 1)), dup(b_im.transpose(0, 2, 1))], axis=1)
    ctr, cti = c_re.transpose(0, 2, 1), c_im.transpose(0, 2, 1)
    ct = jnp.stack([jnp.concatenate([ctr, ctr], 1), jnp.concatenate([cti, cti], 1)], axis=1)

    yg = pl.pallas_call(
        _s5_kernel,
        grid=(steps,),
        in_specs=[pl.BlockSpec((1, n_chunks, w), lambda i: (i, 0, 0)),
                  pl.BlockSpec(perm.shape, lambda i: (0, 0)),
                  pl.BlockSpec((ng, 8, 2 * S5_STATE), lambda i: (i, 0, 0)),
                  pl.BlockSpec((ng, 2 * S5_STATE, 8), lambda i: (i, 0, 0)),
                  pl.BlockSpec((ng, 2, S5_GROUP, 2 * S5_STATE), lambda i: (i, 0, 0, 0)),
                  pl.BlockSpec((ng, 2, 2 * S5_STATE, S5_GROUP), lambda i: (i, 0, 0, 0))],
        out_specs=pl.BlockSpec((1, n_chunks, w), lambda i: (i, 0, 0)),
        out_shape=jax.ShapeDtypeStruct((steps, n_chunks, w), BF16),
        scratch_shapes=[pltpu.VMEM((ng, S5_CK, S5_CK), BF16)],
        compiler_params=_params(("parallel",)),
    )(ug, perm, rowp, colp, bt, ct)
    return yg.reshape(steps, n_chunks, S5_CHUNK, 128).transpose(1, 2, 0, 3).reshape(seq, S5_WIDTH)


def _mla_prep_kernel(mla_ref, pos_ref, ga_ref, gkv_ref, gq_ref, gkn_ref, gkr_ref,
                     invf_ref, wq_ref, wkv_ref, q_ref, k_ref, v_ref):
    cq = mla_ref[:, 0:CQ_W].astype(F32)
    lane = lax.broadcasted_iota(jnp.int32, cq.shape, 1)
    ssq = jnp.sum(jnp.where(lane < MLA_Q_RANK, cq * cq, 0.0), axis=-1, keepdims=True)
    hq = (cq * lax.rsqrt(ssq * (1.0 / MLA_Q_RANK) + EPS) * ga_ref[...]).astype(BF16)
    qf = jnp.dot(hq, wq_ref[...], preferred_element_type=F32)
    ckv = mla_ref[:, CKV_OFF:].astype(F32)
    lane_kv = lax.broadcasted_iota(jnp.int32, ckv.shape, 1)
    in_kv = jnp.logical_and(lane_kv >= CKV_LO, lane_kv < CKV_LO + MLA_KV_RANK)
    ssq_kv = jnp.sum(jnp.where(in_kv, ckv * ckv, 0.0), axis=-1, keepdims=True)
    hkv = (ckv * lax.rsqrt(ssq_kv * (1.0 / MLA_KV_RANK) + EPS) * gkv_ref[...]).astype(BF16)
    kvf = jnp.dot(hkv, wkv_ref[...], preferred_element_type=F32)

    l128 = lax.broadcasted_iota(jnp.int32, (cq.shape[0], 128), 1)
    half = MLA_ROPE // 2
    ang = pos_ref[...].astype(F32) * invf_ref[...]
    cosv = jnp.cos(ang)
    sinv = jnp.sin(ang)
    sgn_sin = jnp.where(l128 < half, -sinv, jnp.where(l128 < MLA_ROPE, sinv, 0.0))

    def rope(t):
        sw = jnp.where(l128 < half, pltpu.roll(t, 128 - half, 1), pltpu.roll(t, half, 1))
        return t * cosv + sw * sgn_sin

    kr = jnp.where(l128 < MLA_ROPE, pltpu.roll(mla_ref[:, KR_OFF:].astype(F32), MLA_ROPE, 1), 0.0)
    kr_ssq = jnp.sum(kr * kr, axis=-1, keepdims=True)
    kr_rot = rope(kr * gkr_ref[...])

    gq = gq_ref[...]
    gkn = gkn_ref[...]
    scale = MLA_QK ** -0.5
    inv_qk = 1.0 / MLA_QK
    ones_col = jnp.where(l128 == 0, 1.0, 0.0).astype(BF16)
    for h in range(MLA_HEADS):
        qh = qf[:, MLA_QK_PAD * h:MLA_QK_PAD * (h + 1)]
        rq = lax.rsqrt(jnp.sum(qh * qh, axis=-1, keepdims=True) * inv_qk + EPS)
        qn = qh * rq * gq
        q_ref[h] = (jnp.concatenate([qn[:, :MLA_NOPE], rope(qn[:, MLA_NOPE:])], axis=1) * scale).astype(BF16)
        kn = kvf[:, MLA_QK_PAD * h:MLA_QK_PAD * h + MLA_NOPE]
        rk = lax.rsqrt((jnp.sum(kn * kn, axis=-1, keepdims=True) + kr_ssq) * inv_qk + EPS)
        k_ref[h] = jnp.concatenate([kn * rk * gkn, kr_rot * rk], axis=1).astype(BF16)
        vh = kvf[:, MLA_QK_PAD * h + MLA_NOPE:MLA_QK_PAD * (h + 1)].astype(BF16)
        v_ref[h] = jnp.concatenate([vh, ones_col], axis=1)


def mla_prep(proj, pos_col, ga, gkv, gq, gkn, gkr, invf, wq, wkv, *, tm):
    seq = proj.shape[0]
    h = MLA_HEADS
    full = lambda shape: pl.BlockSpec(shape, lambda i: (0,) * len(shape))
    return pl.pallas_call(
        _mla_prep_kernel,
        grid=(seq // tm,),
        in_specs=[pl.BlockSpec((tm, MLA_W), lambda i: (i, PJ_MLA // MLA_W)),
                  pl.BlockSpec((tm, 1), lambda i: (i, 0)),
                  full((1, CQ_W)), full((1, CKV_W)), full((1, MLA_QK_PAD)), full((1, 128)), full((1, 128)),
                  full((1, 128)), full((CQ_W, h * MLA_QK_PAD)), full((CKV_W, h * MLA_QK_PAD))],
        out_specs=[pl.BlockSpec((h, tm, MLA_QK_PAD), lambda i: (0, i, 0)),
                   pl.BlockSpec((h, tm, MLA_QK_PAD), lambda i: (0, i, 0)),
                   pl.BlockSpec((h, tm, MLA_VP), lambda i: (0, i, 0))],
        out_shape=[jax.ShapeDtypeStruct((h, seq, MLA_QK_PAD), BF16),
                   jax.ShapeDtypeStruct((h, seq, MLA_QK_PAD), BF16),
                   jax.ShapeDtypeStruct((h, seq, MLA_VP), BF16)],
        compiler_params=_params(("parallel",)),
    )(proj, pos_col, ga, gkv, gq, gkn, gkr, invf, wq, wkv)


def _flash_kernel(q_ref, k_ref, v_ref, pq_ref, pk_ref, o_ref, sa, sb, pa, pb, ala, alb, m_ref, acc_ref, *, t):
    nq = q_ref.shape[1] // t
    nt = (((1,), (1,)), ((), ()))

    def blk(j):
        return pl.ds(pl.multiple_of(j * t, t), t)

    def score(qt, j, s_out):
        s_out[...] = lax.dot_general(q_ref[0, blk(qt), :], k_ref[0, blk(j), :], nt,
                                     preferred_element_type=F32)

    score(0, 0, sa)

    def tile(qi, carry):
        _flash_tile(qi, nq, blk, score, v_ref, pq_ref, pk_ref, o_ref, sa, sb, pa, pb, ala, alb, m_ref, acc_ref, t)
        return carry

    lax.fori_loop(0, nq, tile, 0)


def _flash_tile(qi, nq, blk, score, v_ref, pq_ref, pk_ref, o_ref, sa, sb, pa, pb, ala, alb, m_ref, acc_ref, t):
    def stage_a(j, s_out):
        score(qi, j, s_out)

    def stage_b(j, s_in, p_out, al_out, masked):
        s = s_in[...]
        if masked:
            s = jnp.where(pk_ref[:, blk(j)] <= pq_ref[blk(qi), :], s, NEG_INF)
        m_prev = m_ref[...]
        m_new = jnp.maximum(m_prev, jnp.max(s, axis=-1, keepdims=True))
        al_out[...] = jnp.exp(m_prev - m_new)
        p_out[...] = jnp.exp((s - jnp.concatenate([m_new] * (t // 128), axis=1)).astype(BF16))
        m_ref[...] = m_new

    def stage_c(j, p_in, al_in):
        al = al_in[...]
        acc_ref[...] = (jnp.concatenate([al] * (MLA_VP // 128), axis=1) * acc_ref[...]
                        + jnp.dot(p_in[...], v_ref[0, blk(j), :], preferred_element_type=F32))

    m_ref[...] = jnp.full(m_ref.shape, NEG_INF, F32)
    acc_ref[...] = jnp.zeros(acc_ref.shape, F32)
    pb[...] = jnp.zeros(pb.shape, BF16)
    alb[...] = jnp.ones(alb.shape, F32)
    nxt = jnp.minimum(qi + 1, nq - 1)

    def finish():
        acc = acc_ref[...]
        o_ref[blk(qi), :] = (acc[:, :MLA_V] / acc[:, MLA_V:MLA_V + 1]).astype(o_ref.dtype)

    def pair_body(j):
        stage_a(j + 1, sb)
        stage_b(j, sa, pa, ala, False)
        stage_c(jnp.maximum(j - 1, 0), pb, alb)
        stage_a(j + 2, sa)
        stage_b(j + 1, sb, pb, alb, False)
        stage_c(j, pa, ala)

    def quad(i, c):
        pair_body(4 * i)
        pair_body(4 * i + 2)
        return c

    lax.fori_loop(0, qi // 4, quad, 0)

    @pl.when((qi // 2) % 2 == 1)
    def _():
        pair_body(4 * (qi // 4))

    @pl.when(qi % 2 == 0)
    def _():
        stage_b(qi, sa, pa, ala, True)
        stage_c(jnp.maximum(qi - 1, 0), pb, alb)
        score(nxt, 0, sa)
        stage_c(qi, pa, ala)
        finish()

    @pl.when(qi % 2 == 1)
    def _():
        stage_a(qi, sb)
        stage_b(qi - 1, sa, pa, ala, False)
        stage_c(jnp.maximum(qi - 2, 0), pb, alb)
        score(nxt, 0, sa)
        stage_b(qi, sb, pb, alb, True)
        stage_c(qi - 1, pa, ala)
        stage_c(qi, pb, alb)
        finish()


def flash_attention(q, k, v, pos_col, pos_row, *, t):
    h, seq, _ = q.shape
    return pl.pallas_call(
        functools.partial(_flash_kernel, t=t),
        grid=(h,),
        in_specs=[pl.BlockSpec((1, seq, MLA_QK_PAD), lambda hh: (hh, 0, 0)),
                  pl.BlockSpec((1, seq, MLA_QK_PAD), lambda hh: (hh, 0, 0)),
                  pl.BlockSpec((1, seq, MLA_VP), lambda hh: (hh, 0, 0)),
                  pl.BlockSpec((seq, 1), lambda hh: (0, 0)),
                  pl.BlockSpec((1, seq), lambda hh: (0, 0))],
        out_specs=pl.BlockSpec((seq, MLA_V), lambda hh: (0, hh)),
        out_shape=jax.ShapeDtypeStruct((seq, h * MLA_V), BF16),
        scratch_shapes=[pltpu.VMEM((t, t), F32), pltpu.VMEM((t, t), F32),
                        pltpu.VMEM((t, t), BF16), pltpu.VMEM((t, t), BF16),
                        pltpu.VMEM((t, 128), F32), pltpu.VMEM((t, 128), F32),
                        pltpu.VMEM((t, 128), F32), pltpu.VMEM((t, MLA_VP), F32)],
        compiler_params=_params(("parallel",)),
    )(q, k, v, pos_col, pos_row)


def _mem_kv_kernel(mem_ref, g_ref, w_ref, gk_ref, k_ref, v_ref):
    m = _rms(mem_ref[...], g_ref[...]).astype(BF16)
    kv = jnp.dot(m, w_ref[...], preferred_element_type=F32)
    for h in range(MEM_HEADS):
        kh = kv[:, MEM_HEAD_DIM * h:MEM_HEAD_DIM * (h + 1)]
        k_ref[:, MEM_HEAD_DIM * h:MEM_HEAD_DIM * (h + 1)] = _rms(kh, gk_ref[...]).astype(BF16)
    v_ref[...] = kv[:, MEM_WIDTH:].astype(BF16)


def mem_kv(mem, g, w, gk):
    n_mem = mem.shape[0]
    full = lambda shape: pl.BlockSpec(shape, lambda i: (0,) * len(shape))
    return pl.pallas_call(
        _mem_kv_kernel,
        grid=(1,),
        in_specs=[full(mem.shape), full(g.shape), full(w.shape), full(gk.shape)],
        out_specs=[full((n_mem, MEM_WIDTH)), full((n_mem, MEM_WIDTH))],
        out_shape=[jax.ShapeDtypeStruct((n_mem, MEM_WIDTH), BF16)] * 2,
        compiler_params=_params(("arbitrary",)),
    )(mem, g, w, gk)


def _mem_attn_kernel(q_ref, k_ref, v_ref, gq_ref, o_ref):
    scale = MEM_HEAD_DIM ** -0.5
    for h in range(MEM_HEADS):
        sl = slice(MEM_HEAD_DIM * h, MEM_HEAD_DIM * (h + 1))
        qh = (_rms(q_ref[:, sl].astype(F32), gq_ref[...]) * scale).astype(BF16)
        s = lax.dot_general(qh, k_ref[:, sl], (((1,), (1,)), ((), ())), preferred_element_type=F32)
        p = jnp.exp(s - jnp.max(s, axis=-1, keepdims=True))
        o = jnp.dot(p.astype(BF16), v_ref[:, sl], preferred_element_type=F32)
        o_ref[:, sl] = (o / jnp.sum(p, axis=-1, keepdims=True)).astype(o_ref.dtype)


def mem_attention(proj, k, v, gq, *, tm):
    seq = proj.shape[0]
    n_mem = k.shape[0]
    return pl.pallas_call(
        _mem_attn_kernel,
        grid=(seq // tm,),
        in_specs=[pl.BlockSpec((tm, MEM_WIDTH), lambda i: (i, PJ_MQ // MEM_WIDTH)),
                  pl.BlockSpec((n_mem, MEM_WIDTH), lambda i: (0, 0)),
                  pl.BlockSpec((n_mem, MEM_WIDTH), lambda i: (0, 0)),
                  pl.BlockSpec((1, MEM_HEAD_DIM), lambda i: (0, 0))],
        out_specs=pl.BlockSpec((tm, MEM_WIDTH), lambda i: (i, 0)),
        out_shape=jax.ShapeDtypeStruct((seq, MEM_WIDTH), BF16),
        compiler_params=_params(("parallel",)),
    )(proj, k, v, gq)


def _ffn_kernel(x_ref, g_ref, wg_ref, wu_ref, wd_ref, o_ref, h_ref):
    f = pl.program_id(1)

    @pl.when(f == 0)
    def _():
        x = x_ref[...]
        h_ref[...] = _rms(x, g_ref[...]).astype(BF16)
        o_ref[...] = x

    h = h_ref[...]
    gate = jnp.dot(h, wg_ref[...], preferred_element_type=F32)
    up = jnp.dot(h, wu_ref[...], preferred_element_type=F32)
    a = (jax.nn.silu(gate) * up).astype(BF16)
    o_ref[...] += jnp.dot(a, wd_ref[...], preferred_element_type=F32)


def dense_ffn(x, g, w_gate_up, w_down, *, tm, tf):
    m, d = x.shape
    nf = D_FF // tf
    return pl.pallas_call(
        _ffn_kernel,
        grid=(m // tm, nf),
        in_specs=[pl.BlockSpec((tm, d), lambda i, f: (i, 0)),
                  pl.BlockSpec((1, d), lambda i, f: (0, 0)),
                  pl.BlockSpec((d, tf), lambda i, f: (0, f)),
                  pl.BlockSpec((d, tf), lambda i, f: (0, f + nf)),
                  pl.BlockSpec((tf, d), lambda i, f: (f, 0))],
        out_specs=pl.BlockSpec((tm, d), lambda i, f: (i, 0)),
        out_shape=jax.ShapeDtypeStruct((m, d), F32),
        scratch_shapes=[pltpu.VMEM((tm, d), BF16)],
        compiler_params=_params(("parallel", "arbitrary")),
    )(x, g, w_gate_up, w_gate_up, w_down)


def _router_kernel(x_ref, g_ref, w_ref, b_ref, idx_ref, wgt_ref):
    h = _rms(x_ref[...], g_ref[...])
    logits = jnp.dot(h, w_ref[...], precision=HIGHEST, preferred_element_type=F32) + b_ref[...]
    lane = lax.broadcasted_iota(jnp.int32, logits.shape, 1)
    logits = jnp.where(lane < N_EXPERTS, logits, -jnp.inf)
    v1 = jnp.max(logits, axis=-1, keepdims=True)
    i1 = jnp.min(jnp.where(logits == v1, lane, 128), axis=-1, keepdims=True)
    rest = jnp.where(lane == i1, -jnp.inf, logits)
    v2 = jnp.max(rest, axis=-1, keepdims=True)
    i2 = jnp.min(jnp.where(rest == v2, lane, 128), axis=-1, keepdims=True)
    e2 = jnp.exp(v2 - v1)
    w1 = 1.0 / (1.0 + e2)
    w2 = e2 / (1.0 + e2)
    idx_ref[...] = jnp.where(lane == 0, i1, jnp.where(lane == 1, i2, 0))
    wgt_ref[...] = jnp.where(lane == 0, w1, jnp.where(lane == 1, w2, 0.0))


def moe_router(x, g, w_pad, b_pad, *, tm):
    m, d = x.shape
    return pl.pallas_call(
        _router_kernel,
        grid=(m // tm,),
        in_specs=[pl.BlockSpec((tm, d), lambda i: (i, 0)),
                  pl.BlockSpec((1, d), lambda i: (0, 0)),
                  pl.BlockSpec((d, 128), lambda i: (0, 0)),
                  pl.BlockSpec((1, 128), lambda i: (0, 0))],
        out_specs=[pl.BlockSpec((tm, 128), lambda i: (i, 0)),
                   pl.BlockSpec((tm, 128), lambda i: (i, 0))],
        out_shape=[jax.ShapeDtypeStruct((m, 128), jnp.int32),
                   jax.ShapeDtypeStruct((m, 128), F32)],
        compiler_params=_params(("parallel",)),
    )(x, g, w_pad, b_pad)


MOE_ROW_STEPS = 4


def _moe_ffn_kernel(te_ref, tr_ref, src_ref, x_hbm, g_ref, wg_ref, wu_ref, wd_ref, o_ref, xbuf, h_ref, sem):
    i = pl.program_id(0)
    f = pl.program_id(1)
    n_tiles = pl.num_programs(0)
    rows = tr_ref[i]
    tm = h_ref.shape[0]
    per_step = xbuf.shape[0] // pl.num_programs(1)

    def row_copy(idx, r):
        return pltpu.make_async_copy(x_hbm.at[pl.ds(idx, 1), :], xbuf.at[pl.ds(r, 1), :], sem)

    def start_rows(tile, step):
        for k in range(per_step):
            r = step * per_step + k
            row_copy(src_ref[tile * tm + jnp.minimum(r, tm - 1)], r).start()

    def wait_all():
        def wait_row(r, c):
            row_copy(0, r).wait()
            return c

        lax.fori_loop(0, xbuf.shape[0], wait_row, 0, unroll=4)

    is_first = i == 0
    is_last = i == n_tiles - 1
    nxt = jnp.minimum(i + 1, n_tiles - 1)
    prev_rows = tr_ref[jnp.maximum(i - 1, 0)]

    @pl.when(f == 0)
    def _():
        o_ref[...] = jnp.zeros_like(o_ref)

        @pl.when(jnp.logical_and(is_first, rows > 0))
        def _():
            def first(step, c):
                start_rows(0, step)
                return c

            lax.fori_loop(0, pl.num_programs(1), first, 0)

        @pl.when(jnp.where(is_first, rows, prev_rows) > 0)
        def _():
            wait_all()

        @pl.when(rows > 0)
        def _():
            h_ref[...] = _rms(xbuf[0:tm, :], g_ref[...]).astype(BF16)

    def run(n):
        start_rows(nxt, f)
        h = h_ref[0:n, :]
        gate = jnp.dot(h, wg_ref[0].astype(BF16), preferred_element_type=F32)
        up = jnp.dot(h, wu_ref[0].astype(BF16), preferred_element_type=F32)
        a = (jax.nn.silu(gate) * up).astype(BF16)
        o_ref[0:n, :] += jnp.dot(a, wd_ref[0].astype(BF16), preferred_element_type=F32)

    step = tm // MOE_ROW_STEPS
    for k in range(1, MOE_ROW_STEPS + 1):
        @pl.when(jnp.logical_and(rows > (k - 1) * step, rows <= k * step))
        def _(n=k * step):
            run(n)

    @pl.when(jnp.logical_and(jnp.logical_and(is_last, f == pl.num_programs(1) - 1), rows > 0))
    def _():
        wait_all()


def moe_ffn(x, g, src, tile_expert, tile_rows, w_gate_up, w_down, *, tm, tf):
    d = x.shape[1]
    p = src.shape[0]
    nf = D_FF // tf
    last = nf - 1
    per_step = -(-tm // nf)
    while (nf * per_step) % 8:
        per_step += 1
    gather_rows = nf * per_step

    def fsel(i, f, tr):
        return jnp.where(tr[i] > 0, f, last)

    return pl.pallas_call(
        _moe_ffn_kernel,
        grid_spec=pltpu.PrefetchScalarGridSpec(
            num_scalar_prefetch=3,
            grid=(p // tm, nf),
            in_specs=[pl.BlockSpec(memory_space=pl.ANY),
                      pl.BlockSpec((1, d), lambda i, f, te, tr, src: (0, 0)),
                      pl.BlockSpec((1, d, tf), lambda i, f, te, tr, src: (te[i], 0, fsel(i, f, tr))),
                      pl.BlockSpec((1, d, tf), lambda i, f, te, tr, src: (te[i], 0, fsel(i, f, tr) + nf)),
                      pl.BlockSpec((1, tf, d), lambda i, f, te, tr, src: (te[i], fsel(i, f, tr), 0))],
            out_specs=pl.BlockSpec((tm, d), lambda i, f, te, tr, src: (i, 0)),
            scratch_shapes=[pltpu.VMEM((gather_rows, d), F32), pltpu.VMEM((tm, d), BF16),
                            pltpu.SemaphoreType.DMA(())]),
        out_shape=jax.ShapeDtypeStruct((p, d), F32),
        compiler_params=_params(("arbitrary", "arbitrary")),
    )(tile_expert, tile_rows, src, x, g, w_gate_up, w_gate_up, w_down)


def _combine_kernel(p0_ref, p1_ref, x_ref, w_ref, ys_hbm, o_ref, buf, sem, *, tt):
    base = pl.program_id(0) * tt

    def issue(j, c):
        r0 = p0_ref[base + j]
        r1 = p1_ref[base + j]
        pltpu.make_async_copy(ys_hbm.at[pl.ds(r0, 1), :], buf.at[0, pl.ds(j, 1), :], sem.at[0]).start()
        pltpu.make_async_copy(ys_hbm.at[pl.ds(r1, 1), :], buf.at[1, pl.ds(j, 1), :], sem.at[1]).start()
        return c

    lax.fori_loop(0, tt, issue, 0, unroll=8)

    def drain(j, c):
        pltpu.make_async_copy(ys_hbm.at[pl.ds(0, 1), :], buf.at[0, pl.ds(j, 1), :], sem.at[0]).wait()
        pltpu.make_async_copy(ys_hbm.at[pl.ds(0, 1), :], buf.at[1, pl.ds(j, 1), :], sem.at[1]).wait()
        return c

    lax.fori_loop(0, tt, drain, 0, unroll=8)
    w = w_ref[...]
    o_ref[...] = x_ref[...] + (w[:, 0:1] * buf[0] + w[:, 1:2] * buf[1])


def moe_combine(x, wgt, ys, pos0, pos1, *, tt):
    m, d = x.shape
    return pl.pallas_call(
        functools.partial(_combine_kernel, tt=tt),
        grid_spec=pltpu.PrefetchScalarGridSpec(
            num_scalar_prefetch=2,
            grid=(m // tt,),
            in_specs=[pl.BlockSpec((tt, d), lambda i, p0, p1: (i, 0)),
                      pl.BlockSpec((tt, 128), lambda i, p0, p1: (i, 0)),
                      pl.BlockSpec(memory_space=pl.ANY)],
            out_specs=pl.BlockSpec((tt, d), lambda i, p0, p1: (i, 0)),
            scratch_shapes=[pltpu.VMEM((2, tt, d), F32), pltpu.SemaphoreType.DMA((2,))]),
        out_shape=jax.ShapeDtypeStruct((m, d), F32),
        compiler_params=_params(("arbitrary",)),
    )(pos0, pos1, x, wgt, ys)


def moe_layer(x, g, router, router_b, w_gate_up, w_down, expert_base, *, tm, tf):
    seq, d = x.shape
    w_pad = jnp.pad(router, ((0, 0), (0, 128 - N_EXPERTS)))
    b_pad = jnp.pad(router_b, (0, 128 - N_EXPERTS))[None, :]
    idx, wgt = moe_router(x, g, w_pad, b_pad, tm=min(512, seq))

    e_flat = idx[:, :TOP_K].reshape(-1)
    onehot = (e_flat[:, None] == jnp.arange(N_EXPERTS, dtype=jnp.int32)[None, :]).astype(jnp.int32)
    rank = jnp.cumsum(onehot, axis=0) - onehot
    counts = jnp.sum(onehot, axis=0)
    tiles_per = (counts + tm - 1) // tm
    tile_ends = jnp.cumsum(tiles_per)
    tile_starts = tile_ends - tiles_per
    pos = jnp.sum(onehot * (tile_starts[None, :] * tm + rank), axis=1).astype(jnp.int32)
    n_tiles = TOP_K * seq // tm + N_EXPERTS
    p_total = n_tiles * tm
    tok = jnp.arange(TOP_K * seq, dtype=jnp.int32) // TOP_K
    src = jnp.zeros((p_total,), jnp.int32).at[pos].set(tok)
    tile_idx = jnp.arange(n_tiles, dtype=jnp.int32)
    tile_expert = jnp.minimum(jnp.sum((tile_idx[:, None] >= tile_ends[None, :]).astype(jnp.int32), axis=1),
                              N_EXPERTS - 1).astype(jnp.int32)
    valid = tile_idx < tile_ends[-1]
    remaining = counts[tile_expert] - (tile_idx - tile_starts[tile_expert]) * tm
    tile_rows = jnp.where(valid, jnp.minimum(remaining, tm), 0).astype(jnp.int32)
    tile_expert = jnp.where(valid, tile_expert, jnp.max(jnp.where(valid, tile_expert, 0))).astype(jnp.int32)

    ys = moe_ffn(x, g, src, tile_expert + expert_base, tile_rows, w_gate_up, w_down, tm=tm, tf=tf)
    pos2 = pos.reshape(seq, TOP_K)
    return moe_combine(x, wgt, ys, pos2[:, 0], pos2[:, 1], tt=min(256, seq))


def _repack_w_q_b(w):
    w = w.reshape(MLA_Q_RANK, MLA_HEADS, MLA_QK)
    w = jnp.pad(w, ((0, CQ_W - MLA_Q_RANK), (0, 0), (0, MLA_QK_PAD - MLA_QK)))
    return w.reshape(CQ_W, MLA_HEADS * MLA_QK_PAD).astype(BF16)


def mixer_layer(x, mem, pos_col, pos_row, invf, p):
    seq = x.shape[0]
    tm = min(1024, seq)
    row = lambda v: v[None, :]
    proj, u = norm_matmul(x, row(p['norm_mix']), p['w_in'].astype(BF16), tm=tm, tn=S5_WIDTH)

    y = s5_mix(u, p['s5_a_re'], p['s5_a_im'], p['s5_log_dt'], p['s5_b_re'],
               p['s5_b_im'], p['s5_c_re'], p['s5_c_im'], p['s5_d'])

    ga = row(jnp.pad(p['mla_q_a_norm'], (0, CQ_W - MLA_Q_RANK)))
    kv_pad = (CKV_LO, CKV_W - CKV_LO - MLA_KV_RANK)
    gkv = row(jnp.pad(p['mla_kv_norm'], kv_pad))
    gq = row(jnp.pad(p['mla_q_norm'], (0, MLA_QK_PAD - MLA_QK)))
    gkn = row(p['mla_k_norm'][:MLA_NOPE])
    gkr = row(jnp.pad(p['mla_k_norm'][MLA_NOPE:], (0, 128 - MLA_ROPE)))
    wkv = jnp.pad(p['mla_w_kv_b'], (kv_pad, (0, 0))).astype(BF16)
    q, k, v = mla_prep(proj, pos_col, ga, gkv, gq, gkn, gkr, invf,
                       _repack_w_q_b(p['mla_w_q_b']), wkv, tm=min(256, seq))
    o = flash_attention(q, k, v, pos_col, pos_row, t=min(512, seq))

    km, vm = mem_kv(mem, row(p['mem_norm']), p['mem_w_kv'].astype(BF16), row(p['mem_k_norm']))
    om = mem_attention(proj, km, vm, row(p['mem_q_norm']), tm=min(512, seq))
    merged = merge_branches(y, o, om, p['s5_w_glu'].astype(BF16), p['mla_w_o'].astype(BF16),
                            p['mem_w_o'].astype(BF16), proj, tm=tm, tn=512)

    return resid_matmul(merged, p['w_out'].astype(BF16), x, tm=tm, tn=1024)


_LAYER_KEYS = ('norm_mix', 'w_in', 's5_a_re', 's5_a_im', 's5_log_dt', 's5_b_re', 's5_b_im', 's5_c_re',
               's5_c_im', 's5_d', 's5_w_glu', 'mla_q_a_norm', 'mla_w_q_b', 'mla_kv_norm', 'mla_w_kv_b',
               'mla_q_norm', 'mla_k_norm', 'mla_w_o', 'mem_norm', 'mem_w_kv', 'mem_q_norm', 'mem_k_norm',
               'mem_w_o', 'w_out')


def kernel(x, mem, positions, norm_mix, w_in, s5_a_re, s5_a_im, s5_log_dt, s5_b_re, s5_b_im, s5_c_re, s5_c_im, s5_d, s5_w_glu, mla_q_a_norm, mla_w_q_b, mla_kv_norm, mla_w_kv_b, mla_q_norm, mla_k_norm, mla_w_o, mem_norm, mem_w_kv, mem_q_norm, mem_k_norm, mem_w_o, w_out, norm_ffn, ffn_w_gate_up, ffn_w_down, moe_router, moe_router_b, moe_w_gate_up, moe_w_down):
    stacked = dict(zip(_LAYER_KEYS, (norm_mix, w_in, s5_a_re, s5_a_im, s5_log_dt, s5_b_re, s5_b_im, s5_c_re,
                                     s5_c_im, s5_d, s5_w_glu, mla_q_a_norm, mla_w_q_b, mla_kv_norm,
                                     mla_w_kv_b, mla_q_norm, mla_k_norm, mla_w_o, mem_norm, mem_w_kv,
                                     mem_q_norm, mem_k_norm, mem_w_o, w_out)))
    bsz, seq, d = x.shape
    depth = norm_mix.shape[0]
    half = MLA_ROPE // 2
    inv_freq = ROPE_THETA ** (-jnp.arange(0, MLA_ROPE, 2, dtype=F32) / MLA_ROPE)
    invf = jnp.concatenate([inv_freq, inv_freq, jnp.zeros((128 - 2 * half,), F32)])[None, :]
    moe_gu = moe_w_gate_up.reshape((-1,) + moe_w_gate_up.shape[2:])
    moe_dn = moe_w_down.reshape((-1,) + moe_w_down.shape[2:])

    outs = []
    for b in range(bsz):
        xb = x[b]
        pos_col = positions[b][:, None]
        pos_row = positions[b][None, :]
        for l in range(depth):
            p = {key: val[l] for key, val in stacked.items()}
            xb = mixer_layer(xb, mem[b], pos_col, pos_row, invf, p)
            gf = norm_ffn[l][None, :]
            if l % 2 == 0:
                xb = dense_ffn(xb, gf, ffn_w_gate_up[l // 2].astype(BF16), ffn_w_down[l // 2].astype(BF16),
                               tm=min(512, seq), tf=512)
            else:
                xb = moe_layer(xb, gf, moe_router[l // 2], moe_router_b[l // 2], moe_gu, moe_dn,
                               (l // 2) * N_EXPERTS, tm=min(1024, TOP_K * seq), tf=256)
        outs.append(xb)
    return jnp.stack(outs, axis=0)
```

```python
import functools
import math

import jax
import jax.numpy as jnp
from jax import lax
from jax.experimental import pallas as pl
from jax.experimental.pallas import tpu as pltpu

F32 = jnp.float32
BF16 = jnp.bfloat16
HIGHEST = lax.Precision.HIGHEST

D_MODEL = 2048
N_MEM = 256
S5_WIDTH = 1024
S5_GROUP = 16
S5_GROUPS = S5_WIDTH // S5_GROUP
S5_STATE = 64
S5_CHUNK = 32
S5_CK = S5_CHUNK * S5_GROUP
S5_GROUPS_PER_STEP = 128 // S5_GROUP
MLA_HEADS = 16
MLA_Q_RANK = 448
MLA_KV_RANK = 512
MLA_NOPE = 128
MLA_ROPE = 64
MLA_QK = MLA_NOPE + MLA_ROPE
MLA_V = 128
MLA_QK_PAD = 256
MLA_VP = 256
ROPE_THETA = 10000.0
MEM_HEADS = 4
MEM_HEAD_DIM = 256
MEM_WIDTH = MEM_HEADS * MEM_HEAD_DIM
N_BRANCH = 3
D_FF = 7168
N_EXPERTS = 8
TOP_K = 2
EPS = 1e-6
NEG_INF = -1e30

PJ_U = 0
PJ_MLA = 1024
PJ_MQ = 2048
PJ_GATE = 3072
MLA_W = MLA_Q_RANK + MLA_KV_RANK + MLA_ROPE
CQ_W = 512
CKV_OFF = 384
CKV_W = MLA_W - CKV_OFF
CKV_LO = MLA_Q_RANK - CKV_OFF
KR_OFF = MLA_W - 128

VMEM_LIMIT = 52 * 1024 * 1024


def _params(sem):
    return pltpu.CompilerParams(dimension_semantics=sem, vmem_limit_bytes=VMEM_LIMIT)


def _rms(x, g):
    r = lax.rsqrt(jnp.mean(x * x, axis=-1, keepdims=True) + EPS)
    return x * r * g


def _onehot_dot(a, b):
    if a.dtype == BF16:
        hi = b.astype(BF16)
        lo = (b - hi.astype(F32)).astype(BF16)
        return (jnp.dot(a, hi, preferred_element_type=F32) + jnp.dot(a, lo, preferred_element_type=F32))
    hi = a.astype(BF16)
    lo = (a - hi.astype(F32)).astype(BF16)
    return (jnp.dot(hi, b, preferred_element_type=F32) + jnp.dot(lo, b, preferred_element_type=F32))


def _norm_mm_kernel(x_ref, g_ref, w_ref, o_ref, first_ref, h_ref):
    @pl.when(pl.program_id(1) == 0)
    def _():
        h_ref[...] = _rms(x_ref[...], g_ref[...]).astype(BF16)

    res = jnp.dot(h_ref[...], w_ref[...], preferred_element_type=F32).astype(o_ref.dtype)
    o_ref[...] = res

    @pl.when(pl.program_id(1) == 0)
    def _():
        first_ref[...] = res


def norm_matmul(x, g, w, *, tm, tn, out_dtype=BF16):
    m, k = x.shape
    n = w.shape[1]
    return pl.pallas_call(
        _norm_mm_kernel,
        grid=(m // tm, n // tn),
        in_specs=[pl.BlockSpec((tm, k), lambda i, j: (i, 0)),
                  pl.BlockSpec((1, k), lambda i, j: (0, 0)),
                  pl.BlockSpec((k, tn), lambda i, j: (0, j))],
        out_specs=[pl.BlockSpec((tm, tn), lambda i, j: (i, j)),
                   pl.BlockSpec((tm, tn), lambda i, j: (i, 0))],
        out_shape=[jax.ShapeDtypeStruct((m, n), out_dtype),
                   jax.ShapeDtypeStruct((m, tn), out_dtype)],
        scratch_shapes=[pltpu.VMEM((tm, k), BF16)],
        compiler_params=_params(("parallel", "arbitrary")),
    )(x, g, w)


def _merge_kernel(y_ref, o_ref, om_ref, wa_ref, wb_ref, wo_ref, wm_ref, g0_ref, g1_ref, g2_ref, out_ref):
    y = y_ref[...]
    s5 = (jnp.dot(y, wa_ref[...], preferred_element_type=F32)
          * jax.nn.sigmoid(jnp.dot(y, wb_ref[...], preferred_element_type=F32)))
    mla = jnp.dot(o_ref[...], wo_ref[...], preferred_element_type=F32)
    mem = jnp.dot(om_ref[...], wm_ref[...], preferred_element_type=F32)
    gate = lambda r: jax.nn.sigmoid(r[...].astype(F32))
    out_ref[...] = (gate(g0_ref) * s5 + gate(g1_ref) * mla + gate(g2_ref) * mem).astype(out_ref.dtype)


def merge_branches(y, o, om, w_glu, w_o, w_mo, proj, *, tm, tn):
    m = y.shape[0]
    n = w_o.shape[1]
    nb = n // tn
    gb = PJ_GATE // tn
    lhs = lambda a: pl.BlockSpec((tm, a.shape[1]), lambda i, j: (i, 0))
    col = lambda w, off: pl.BlockSpec((w.shape[0], tn), lambda i, j: (0, j + off))
    gate = lambda b: pl.BlockSpec((tm, tn), lambda i, j: (i, gb + b * nb + j))
    return pl.pallas_call(
        _merge_kernel,
        grid=(m // tm, nb),
        in_specs=[lhs(y), lhs(o), lhs(om), col(w_glu, 0), col(w_glu, nb), col(w_o, 0), col(w_mo, 0),
                  gate(0), gate(1), gate(2)],
        out_specs=pl.BlockSpec((tm, tn), lambda i, j: (i, j)),
        out_shape=jax.ShapeDtypeStruct((m, n), BF16),
        compiler_params=_params(("parallel", "parallel")),
    )(y, o, om, w_glu, w_glu, w_o, w_mo, proj, proj, proj)


def _resid_mm_kernel(a_ref, w_ref, x_ref, o_ref):
    o_ref[...] = x_ref[...] + jnp.dot(a_ref[...], w_ref[...], preferred_element_type=F32)


def resid_matmul(a, w, x, *, tm, tn):
    m, k = a.shape
    n = w.shape[1]
    return pl.pallas_call(
        _resid_mm_kernel,
        grid=(m // tm, n // tn),
        in_specs=[pl.BlockSpec((tm, k), lambda i, j: (i, 0)),
                  pl.BlockSpec((k, tn), lambda i, j: (0, j)),
                  pl.BlockSpec((tm, tn), lambda i, j: (i, j))],
        out_specs=pl.BlockSpec((tm, tn), lambda i, j: (i, j)),
        out_shape=jax.ShapeDtypeStruct((m, n), F32),
        compiler_params=_params(("parallel", "parallel")),
    )(a, w, x)


def _s5_kernel(u_ref, perm_ref, rowp_ref, colp_ref, bt_ref, ct_ref, y_ref, toep_ref):
    ng = S5_GROUPS_PER_STEP
    w = ng * 128
    n_tiles = S5_CK // 128
    perm = perm_ref[...]
    parts = [jnp.dot(u_ref[0, :, a * w:(a + 1) * w], perm, preferred_element_type=F32).astype(BF16)
             for a in range(n_tiles)]
    ys = []
    for gi in range(ng):
        u = jnp.concatenate([part[:, gi * 128:(gi + 1) * 128] for part in parts], axis=1)
        ys.append(_s5_group(gi, u, rowp_ref, colp_ref, bt_ref, ct_ref, toep_ref))
    for a in range(n_tiles):
        ycat = jnp.concatenate([y[:, a * 128:(a + 1) * 128] for y in ys], axis=1)
        y_ref[0, :, a * w:(a + 1) * w] = lax.dot_general(
            ycat, perm, (((1,), (1,)), ((), ())), preferred_element_type=F32).astype(y_ref.dtype)


def _s5_group(gi, u, rowp_ref, colp_ref, bt_ref, ct_ref, toep_ref):
    t_chunk = S5_CHUNK
    n_chunks = u.shape[0]
    p2 = 2 * S5_STATE

    rowp = rowp_ref[gi]
    are_r, aim_r = rowp[0:1], rowp[1:2]
    dt_r = jnp.exp(rowp[2:3])
    colp = colp_ref[gi]
    are_c, aim_c = colp[:, 0:1], colp[:, 1:2]
    dt_c = jnp.exp(colp[:, 2:3])
    d_c = colp[0:S5_GROUP, 3:4]

    row_q = lax.broadcasted_iota(jnp.int32, (p2, S5_CK), 0)
    lane_q = lax.broadcasted_iota(jnp.int32, (1, p2), 1)

    tau = lax.broadcasted_iota(jnp.int32, (p2, p2), 1).astype(F32)
    row_pp = lax.broadcasted_iota(jnp.int32, (p2, p2), 0)
    mag = jnp.exp(tau * dt_c * are_c)
    ang = tau * dt_c * aim_c
    pw = mag * jnp.where(row_pp < S5_STATE, jnp.cos(ang), jnp.sin(ang))

    l_t = lax.broadcasted_iota(jnp.int32, (p2, S5_CK), 1) // S5_GROUP
    e_tau = (l_t == row_q).astype(BF16)
    e_tau1 = (l_t + 1 == row_q).astype(BF16)
    l_i = lax.broadcasted_iota(jnp.int32, (S5_GROUP, S5_CK), 1) % S5_GROUP
    e_ch = (l_i == lax.broadcasted_iota(jnp.int32, (S5_GROUP, S5_CK), 0)).astype(BF16)

    def swap_halves(x):
        return jnp.concatenate([x[S5_STATE:], x[:S5_STATE]], axis=0)

    x1 = _onehot_dot(pw, e_tau)
    x1s = _onehot_dot(pw, e_tau1)
    ct = ct_ref[gi]
    cta = _onehot_dot(ct[0], e_ch)
    ctb = _onehot_dot(ct[1], e_ch)
    ctb = jnp.where(row_q < S5_STATE, -ctb, ctb)
    z = cta * x1 + ctb * swap_halves(x1)
    zs = cta * x1s + ctb * swap_halves(x1s)
    wt = jnp.where(row_q < S5_STATE, zs, -zs)

    lam_mag = jnp.exp(dt_r * are_r)
    lam_re = lam_mag * jnp.cos(dt_r * aim_r)
    lam_im = lam_mag * jnp.sin(dt_r * aim_r)
    den = are_r * are_r + aim_r * aim_r
    n_re = lam_re - 1.0
    f_re = (n_re * are_r + lam_im * aim_r) / den
    f_im = (lam_im * are_r - n_re * aim_r) / den
    bt = bt_ref[gi]
    bb_re = f_re * bt[0] - f_im * bt[1]
    bb_im = f_re * bt[1] + f_im * bt[0]
    first = lane_q < S5_STATE

    kt = jnp.dot(jnp.where(first, bb_re, -bb_im), z, precision=HIGHEST, preferred_element_type=F32)
    lane_k = lax.broadcasted_iota(jnp.int32, (S5_GROUP, S5_CK), 1)
    row_k = lax.broadcasted_iota(jnp.int32, (S5_GROUP, S5_CK), 0)
    kt = kt + jnp.where(lane_k == row_k, d_c, 0.0)
    for s in range(t_chunk):
        off = S5_GROUP * s
        blk = kt if s == 0 else jnp.where(lane_k >= off, pltpu.roll(kt, off, 1), 0.0)
        toep_ref[gi, off:off + S5_GROUP, :] = blk.astype(BF16)

    e_s = (t_chunk - 1 - lax.broadcasted_iota(jnp.int32, (t_chunk, p2), 0)).astype(F32)
    pt_mag = jnp.exp(e_s * dt_r * are_r)
    pt_ang = e_s * dt_r * aim_r
    pt_re = pt_mag * jnp.cos(pt_ang)
    pt_im = pt_mag * jnp.sin(pt_ang)
    r_s = lax.broadcasted_iota(jnp.int32, (S5_CK, t_chunk), 0) // S5_GROUP
    e_rows = (r_s == lax.broadcasted_iota(jnp.int32, (S5_CK, t_chunk), 1)).astype(BF16)
    r_j = lax.broadcasted_iota(jnp.int32, (S5_CK, S5_GROUP), 0) % S5_GROUP
    e_rowj = (r_j == lax.broadcasted_iota(jnp.int32, (S5_CK, S5_GROUP), 1)).astype(BF16)
    gt = (_onehot_dot(e_rows, pt_re) * _onehot_dot(e_rowj, jnp.where(first, bb_re, bb_im))
          + _onehot_dot(e_rows, pt_im) * _onehot_dot(e_rowj, jnp.where(first, -bb_im, bb_re)))

    h = jnp.dot(u, gt.astype(BF16), preferred_element_type=F32)

    k_row = lax.broadcasted_iota(jnp.int32, (8, p2), 0)
    m_pow = (t_chunk * (1 << k_row)).astype(F32)
    m_mag = jnp.exp(m_pow * dt_r * are_r)
    m_ang = m_pow * dt_r * aim_r
    m_re = m_mag * jnp.cos(m_ang)
    m_im = m_mag * jnp.sin(m_ang)
    row_c = lax.broadcasted_iota(jnp.int32, (n_chunks, p2), 0)
    k = 0
    while (1 << k) < n_chunks:
        sh = 1 << k
        s_prev = jnp.where(row_c >= sh, pltpu.roll(h, sh, 0), 0.0)
        s_swap = pltpu.roll(s_prev, S5_STATE, 1)
        mb = jnp.where(first, -m_im[k:k + 1], m_im[k:k + 1])
        h = h + m_re[k:k + 1] * s_prev + mb * s_swap
        k += 1
    h_prev = jnp.where(row_c >= 1, pltpu.roll(h, 1, 0), 0.0)

    y = (jnp.dot(u, toep_ref[gi], preferred_element_type=F32)
         + jnp.dot(h_prev.astype(BF16), wt.astype(BF16), preferred_element_type=F32))
    return jax.nn.gelu(y).astype(BF16)


def s5_mix(u, a_re, a_im, log_dt, b_re, b_im, c_re, c_im, d):
    seq = u.shape[0]
    n_chunks = seq // S5_CHUNK
    g = S5_GROUPS
    ng = S5_GROUPS_PER_STEP
    steps = g // ng
    w = ng * S5_CK
    ug = u.reshape(n_chunks, S5_CHUNK, steps, 128).transpose(2, 0, 1, 3).reshape(steps, n_chunks, w)
    r = jnp.arange(ng * 128, dtype=jnp.int32)
    target = ((r // S5_GROUP) % ng) * 128 + (r // 128) * S5_GROUP + r % S5_GROUP
    perm = (r[None, :] == target[:, None]).astype(BF16)

    dup = lambda v: jnp.concatenate([v, v], axis=-1)
    zeros = jnp.zeros((g, 2 * S5_STATE), F32)
    ldt = jnp.broadcast_to(log_dt[:, None], (g, 2 * S5_STATE))
    rowp = jnp.stack([dup(a_re), dup(a_im), ldt] + [zeros] * 5, axis=1)
    d_pad = jnp.pad(d, ((0, 0), (0, 2 * S5_STATE - S5_GROUP)))
    colp = jnp.stack([dup(a_re), dup(a_im), ldt, d_pad] + [zeros] * 4, axis=2)
    bt = jnp.stack([dup(b_re.transpose(0, 2, 1)), dup(b_im.transpose(0, 2, 1))], axis=1)
    ctr, cti = c_re.transpose(0, 2, 1), c_im.transpose(0, 2, 1)
    ct = jnp.stack([jnp.concatenate([ctr, ctr], 1), jnp.concatenate([cti, cti], 1)], axis=1)

    yg = pl.pallas_call(
        _s5_kernel,
        grid=(steps,),
        in_specs=[pl.BlockSpec((1, n_chunks, w), lambda i: (i, 0, 0)),
                  pl.BlockSpec(perm.shape, lambda i: (0, 0)),
                  pl.BlockSpec((ng, 8, 2 * S5_STATE), lambda i: (i, 0, 0)),
                  pl.BlockSpec((ng, 2 * S5_STATE, 8), lambda i: (i, 0, 0)),
                  pl.BlockSpec((ng, 2, S5_GROUP, 2 * S5_STATE), lambda i: (i, 0, 0, 0)),
                  pl.BlockSpec((ng, 2, 2 * S5_STATE, S5_GROUP), lambda i: (i, 0, 0, 0))],
        out_specs=pl.BlockSpec((1, n_chunks, w), lambda i: (i, 0, 0)),
        out_shape=jax.ShapeDtypeStruct((steps, n_chunks, w), BF16),
        scratch_shapes=[pltpu.VMEM((ng, S5_CK, S5_CK), BF16)],
        compiler_params=_params(("parallel",)),
    )(ug, perm, rowp, colp, bt, ct)
    return yg.reshape(steps, n_chunks, S5_CHUNK, 128).transpose(1, 2, 0, 3).reshape(seq, S5_WIDTH)


def _mla_prep_kernel(mla_ref, pos_ref, ga_ref, gkv_ref, gq_ref, gkn_ref, gkr_ref,
                     invf_ref, wq_ref, wkv_ref, q_ref, k_ref, v_ref):
    cq = mla_ref[:, 0:CQ_W].astype(F32)
    lane = lax.broadcasted_iota(jnp.int32, cq.shape, 1)
    ssq = jnp.sum(jnp.where(lane < MLA_Q_RANK, cq * cq, 0.0), axis=-1, keepdims=True)
    hq = (cq * lax.rsqrt(ssq * (1.0 / MLA_Q_RANK) + EPS) * ga_ref[...]).astype(BF16)
    qf = jnp.dot(hq, wq_ref[...], preferred_element_type=F32)
    ckv = mla_ref[:, CKV_OFF:].astype(F32)
    lane_kv = lax.broadcasted_iota(jnp.int32, ckv.shape, 1)
    in_kv = jnp.logical_and(lane_kv >= CKV_LO, lane_kv < CKV_LO + MLA_KV_RANK)
    ssq_kv = jnp.sum(jnp.where(in_kv, ckv * ckv, 0.0), axis=-1, keepdims=True)
    hkv = (ckv * lax.rsqrt(ssq_kv * (1.0 / MLA_KV_RANK) + EPS) * gkv_ref[...]).astype(BF16)
    kvf = jnp.dot(hkv, wkv_ref[...], preferred_element_type=F32)

    l128 = lax.broadcasted_iota(jnp.int32, (cq.shape[0], 128), 1)
    half = MLA_ROPE // 2
    ang = pos_ref[...].astype(F32) * invf_ref[...]
    cosv = jnp.cos(ang)
    sinv = jnp.sin(ang)
    sgn_sin = jnp.where(l128 < half, -sinv, jnp.where(l128 < MLA_ROPE, sinv, 0.0))

    def rope(t):
        sw = jnp.where(l128 < half, pltpu.roll(t, 128 - half, 1), pltpu.roll(t, half, 1))
        return t * cosv + sw * sgn_sin

    kr = jnp.where(l128 < MLA_ROPE, pltpu.roll(mla_ref[:, KR_OFF:].astype(F32), MLA_ROPE, 1), 0.0)
    kr_ssq = jnp.sum(kr * kr, axis=-1, keepdims=True)
    kr_rot = rope(kr * gkr_ref[...])

    gq = gq_ref[...]
    gkn = gkn_ref[...]
    scale = MLA_QK ** -0.5
    inv_qk = 1.0 / MLA_QK
    ones_col = jnp.where(l128 == 0, 1.0, 0.0).astype(BF16)
    for h in range(MLA_HEADS):
        qh = qf[:, MLA_QK_PAD * h:MLA_QK_PAD * (h + 1)]
        rq = lax.rsqrt(jnp.sum(qh * qh, axis=-1, keepdims=True) * inv_qk + EPS)
        qn = qh * rq * gq
        q_ref[h] = (jnp.concatenate([qn[:, :MLA_NOPE], rope(qn[:, MLA_NOPE:])], axis=1) * scale).astype(BF16)
        kn = kvf[:, MLA_QK_PAD * h:MLA_QK_PAD * h + MLA_NOPE]
        rk = lax.rsqrt((jnp.sum(kn * kn, axis=-1, keepdims=True) + kr_ssq) * inv_qk + EPS)
        k_ref[h] = jnp.concatenate([kn * rk * gkn, kr_rot * rk], axis=1).astype(BF16)
        vh = kvf[:, MLA_QK_PAD * h + MLA_NOPE:MLA_QK_PAD * (h + 1)].astype(BF16)
        v_ref[h] = jnp.concatenate([vh, ones_col], axis=1)


def mla_prep(proj, pos_col, ga, gkv, gq, gkn, gkr, invf, wq, wkv, *, tm):
    seq = proj.shape[0]
    h = MLA_HEADS
    full = lambda shape: pl.BlockSpec(shape, lambda i: (0,) * len(shape))
    return pl.pallas_call(
        _mla_prep_kernel,
        grid=(seq // tm,),
        in_specs=[pl.BlockSpec((tm, MLA_W), lambda i: (i, PJ_MLA // MLA_W)),
                  pl.BlockSpec((tm, 1), lambda i: (i, 0)),
                  full((1, CQ_W)), full((1, CKV_W)), full((1, MLA_QK_PAD)), full((1, 128)), full((1, 128)),
                  full((1, 128)), full((CQ_W, h * MLA_QK_PAD)), full((CKV_W, h * MLA_QK_PAD))],
        out_specs=[pl.BlockSpec((h, tm, MLA_QK_PAD), lambda i: (0, i, 0)),
                   pl.BlockSpec((h, tm, MLA_QK_PAD), lambda i: (0, i, 0)),
                   pl.BlockSpec((h, tm, MLA_VP), lambda i: (0, i, 0))],
        out_shape=[jax.ShapeDtypeStruct((h, seq, MLA_QK_PAD), BF16),
                   jax.ShapeDtypeStruct((h, seq, MLA_QK_PAD), BF16),
                   jax.ShapeDtypeStruct((h, seq, MLA_VP), BF16)],
        compiler_params=_params(("parallel",)),
    )(proj, pos_col, ga, gkv, gq, gkn, gkr, invf, wq, wkv)


def _flash_kernel(q_ref, k_ref, v_ref, pq_ref, pk_ref, o_ref, sa, sb, pa, pb, ala, alb, m_ref, acc_ref, *, t):
    nq = q_ref.shape[1] // t
    nt = (((1,), (1,)), ((), ()))

    def blk(j):
        return pl.ds(pl.multiple_of(j * t, t), t)

    def score(qt, j, s_out):
        s_out[...] = lax.dot_general(q_ref[0, blk(qt), :], k_ref[0, blk(j), :], nt,
                                     preferred_element_type=F32)

    score(0, 0, sa)

    def tile(qi, carry):
        _flash_tile(qi, nq, blk, score, v_ref, pq_ref, pk_ref, o_ref, sa, sb, pa, pb, ala, alb, m_ref, acc_ref, t)
        return carry

    lax.fori_loop(0, nq, tile, 0)


def _flash_tile(qi, nq, blk, score, v_ref, pq_ref, pk_ref, o_ref, sa, sb, pa, pb, ala, alb, m_ref, acc_ref, t):
    def stage_a(j, s_out):
        score(qi, j, s_out)

    def stage_b(j, s_in, p_out, al_out, masked):
        s = s_in[...]
        if masked:
            s = jnp.where(pk_ref[:, blk(j)] <= pq_ref[blk(qi), :], s, NEG_INF)
        m_prev = m_ref[...]
        m_new = jnp.maximum(m_prev, jnp.max(s, axis=-1, keepdims=True))
        al_out[...] = jnp.exp(m_prev - m_new)
        p_out[...] = jnp.exp((s - jnp.concatenate([m_new] * (t // 128), axis=1)).astype(BF16))
        m_ref[...] = m_new

    def stage_c(j, p_in, al_in):
        al = al_in[...]
        acc_ref[...] = (jnp.concatenate([al] * (MLA_VP // 128), axis=1) * acc_ref[...]
                        + jnp.dot(p_in[...], v_ref[0, blk(j), :], preferred_element_type=F32))

    m_ref[...] = jnp.full(m_ref.shape, NEG_INF, F32)
    acc_ref[...] = jnp.zeros(acc_ref.shape, F32)
    pb[...] = jnp.zeros(pb.shape, BF16)
    alb[...] = jnp.ones(alb.shape, F32)
    nxt = jnp.minimum(qi + 1, nq - 1)

    def finish():
        acc = acc_ref[...]
        o_ref[blk(qi), :] = (acc[:, :MLA_V] / acc[:, MLA_V:MLA_V + 1]).astype(o_ref.dtype)

    def pair_body(j):
        stage_a(j + 1, sb)
        stage_b(j, sa, pa, ala, False)
        stage_c(jnp.maximum(j - 1, 0), pb, alb)
        stage_a(j + 2, sa)
        stage_b(j + 1, sb, pb, alb, False)
        stage_c(j, pa, ala)

    def quad(i, c):
        pair_body(4 * i)
        pair_body(4 * i + 2)
        return c

    lax.fori_loop(0, qi // 4, quad, 0)

    @pl.when((qi // 2) % 2 == 1)
    def _():
        pair_body(4 * (qi // 4))

    @pl.when(qi % 2 == 0)
    def _():
        stage_b(qi, sa, pa, ala, True)
        stage_c(jnp.maximum(qi - 1, 0), pb, alb)
        score(nxt, 0, sa)
        stage_c(qi, pa, ala)
        finish()

    @pl.when(qi % 2 == 1)
    def _():
        stage_a(qi, sb)
        stage_b(qi - 1, sa, pa, ala, False)
        stage_c(jnp.maximum(qi - 2, 0), pb, alb)
        score(nxt, 0, sa)
        stage_b(qi, sb, pb, alb, True)
        stage_c(qi - 1, pa, ala)
        stage_c(qi, pb, alb)
        finish()


def flash_attention(q, k, v, pos_col, pos_row, *, t):
    h, seq, _ = q.shape
    return pl.pallas_call(
        functools.partial(_flash_kernel, t=t),
        grid=(h,),
        in_specs=[pl.BlockSpec((1, seq, MLA_QK_PAD), lambda hh: (hh, 0, 0)),
                  pl.BlockSpec((1, seq, MLA_QK_PAD), lambda hh: (hh, 0, 0)),
                  pl.BlockSpec((1, seq, MLA_VP), lambda hh: (hh, 0, 0)),
                  pl.BlockSpec((seq, 1), lambda hh: (0, 0)),
                  pl.BlockSpec((1, seq), lambda hh: (0, 0))],
        out_specs=pl.BlockSpec((seq, MLA_V), lambda hh: (0, hh)),
        out_shape=jax.ShapeDtypeStruct((seq, h * MLA_V), BF16),
        scratch_shapes=[pltpu.VMEM((t, t), F32), pltpu.VMEM((t, t), F32),
                        pltpu.VMEM((t, t), BF16), pltpu.VMEM((t, t), BF16),
                        pltpu.VMEM((t, 128), F32), pltpu.VMEM((t, 128), F32),
                        pltpu.VMEM((t, 128), F32), pltpu.VMEM((t, MLA_VP), F32)],
        compiler_params=_params(("parallel",)),
    )(q, k, v, pos_col, pos_row)


def _mem_kv_kernel(mem_ref, g_ref, w_ref, gk_ref, k_ref, v_ref):
    m = _rms(mem_ref[...], g_ref[...]).astype(BF16)
    kv = jnp.dot(m, w_ref[...], preferred_element_type=F32)
    for h in range(MEM_HEADS):
        kh = kv[:, MEM_HEAD_DIM * h:MEM_HEAD_DIM * (h + 1)]
        k_ref[:, MEM_HEAD_DIM * h:MEM_HEAD_DIM * (h + 1)] = _rms(kh, gk_ref[...]).astype(BF16)
    v_ref[...] = kv[:, MEM_WIDTH:].astype(BF16)


def mem_kv(mem, g, w, gk):
    n_mem = mem.shape[0]
    full = lambda shape: pl.BlockSpec(shape, lambda i: (0,) * len(shape))
    return pl.pallas_call(
        _mem_kv_kernel,
        grid=(1,),
        in_specs=[full(mem.shape), full(g.shape), full(w.shape), full(gk.shape)],
        out_specs=[full((n_mem, MEM_WIDTH)), full((n_mem, MEM_WIDTH))],
        out_shape=[jax.ShapeDtypeStruct((n_mem, MEM_WIDTH), BF16)] * 2,
        compiler_params=_params(("arbitrary",)),
    )(mem, g, w, gk)


def _mem_attn_kernel(q_ref, k_ref, v_ref, gq_ref, o_ref):
    scale = MEM_HEAD_DIM ** -0.5
    for h in range(MEM_HEADS):
        sl = slice(MEM_HEAD_DIM * h, MEM_HEAD_DIM * (h + 1))
        qh = (_rms(q_ref[:, sl].astype(F32), gq_ref[...]) * scale).astype(BF16)
        s = lax.dot_general(qh, k_ref[:, sl], (((1,), (1,)), ((), ())), preferred_element_type=F32)
        p = jnp.exp(s - jnp.max(s, axis=-1, keepdims=True))
        o = jnp.dot(p.astype(BF16), v_ref[:, sl], preferred_element_type=F32)
        o_ref[:, sl] = (o / jnp.sum(p, axis=-1, keepdims=True)).astype(o_ref.dtype)


def mem_attention(proj, k, v, gq, *, tm):
    seq = proj.shape[0]
    n_mem = k.shape[0]
    return pl.pallas_call(
        _mem_attn_kernel,
        grid=(seq // tm,),
        in_specs=[pl.BlockSpec((tm, MEM_WIDTH), lambda i: (i, PJ_MQ // MEM_WIDTH)),
                  pl.BlockSpec((n_mem, MEM_WIDTH), lambda i: (0, 0)),
                  pl.BlockSpec((n_mem, MEM_WIDTH), lambda i: (0, 0)),
                  pl.BlockSpec((1, MEM_HEAD_DIM), lambda i: (0, 0))],
        out_specs=pl.BlockSpec((tm, MEM_WIDTH), lambda i: (i, 0)),
        out_shape=jax.ShapeDtypeStruct((seq, MEM_WIDTH), BF16),
        compiler_params=_params(("parallel",)),
    )(proj, k, v, gq)


def _ffn_kernel(x_ref, g_ref, wg_ref, wu_ref, wd_ref, o_ref, h_ref):
    f = pl.program_id(1)

    @pl.when(f == 0)
    def _():
        x = x_ref[...]
        h_ref[...] = _rms(x, g_ref[...]).astype(BF16)
        o_ref[...] = x

    h = h_ref[...]
    gate = jnp.dot(h, wg_ref[...], preferred_element_type=F32)
    up = jnp.dot(h, wu_ref[...], preferred_element_type=F32)
    a = (jax.nn.silu(gate) * up).astype(BF16)
    o_ref[...] += jnp.dot(a, wd_ref[...], preferred_element_type=F32)


def dense_ffn(x, g, w_gate_up, w_down, *, tm, tf):
    m, d = x.shape
    nf = D_FF // tf
    return pl.pallas_call(
        _ffn_kernel,
        grid=(m // tm, nf),
        in_specs=[pl.BlockSpec((tm, d), lambda i, f: (i, 0)),
                  pl.BlockSpec((1, d), lambda i, f: (0, 0)),
                  pl.BlockSpec((d, tf), lambda i, f: (0, f)),
                  pl.BlockSpec((d, tf), lambda i, f: (0, f + nf)),
                  pl.BlockSpec((tf, d), lambda i, f: (f, 0))],
        out_specs=pl.BlockSpec((tm, d), lambda i, f: (i, 0)),
        out_shape=jax.ShapeDtypeStruct((m, d), F32),
        scratch_shapes=[pltpu.VMEM((tm, d), BF16)],
        compiler_params=_params(("parallel", "arbitrary")),
    )(x, g, w_gate_up, w_gate_up, w_down)


def _router_kernel(x_ref, g_ref, w_ref, b_ref, idx_ref, wgt_ref):
    h = _rms(x_ref[...], g_ref[...])
    logits = jnp.dot(h, w_ref[...], precision=HIGHEST, preferred_element_type=F32) + b_ref[...]
    lane = lax.broadcasted_iota(jnp.int32, logits.shape, 1)
    logits = jnp.where(lane < N_EXPERTS, logits, -jnp.inf)
    v1 = jnp.max(logits, axis=-1, keepdims=True)
    i1 = jnp.min(jnp.where(logits == v1, lane, 128), axis=-1, keepdims=True)
    rest = jnp.where(lane == i1, -jnp.inf, logits)
    v2 = jnp.max(rest, axis=-1, keepdims=True)
    i2 = jnp.min(jnp.where(rest == v2, lane, 128), axis=-1, keepdims=True)
    e2 = jnp.exp(v2 - v1)
    w1 = 1.0 / (1.0 + e2)
    w2 = e2 / (1.0 + e2)
    idx_ref[...] = jnp.where(lane == 0, i1, jnp.where(lane == 1, i2, 0))
    wgt_ref[...] = jnp.where(lane == 0, w1, jnp.where(lane == 1, w2, 0.0))


def moe_router(x, g, w_pad, b_pad, *, tm):
    m, d = x.shape
    return pl.pallas_call(
        _router_kernel,
        grid=(m // tm,),
        in_specs=[pl.BlockSpec((tm, d), lambda i: (i, 0)),
                  pl.BlockSpec((1, d), lambda i: (0, 0)),
                  pl.BlockSpec((d, 128), lambda i: (0, 0)),
                  pl.BlockSpec((1, 128), lambda i: (0, 0))],
        out_specs=[pl.BlockSpec((tm, 128), lambda i: (i, 0)),
                   pl.BlockSpec((tm, 128), lambda i: (i, 0))],
        out_shape=[jax.ShapeDtypeStruct((m, 128), jnp.int32),
                   jax.ShapeDtypeStruct((m, 128), F32)],
        compiler_params=_params(("parallel",)),
    )(x, g, w_pad, b_pad)


MOE_ROW_STEPS = 8


def _moe_ffn_kernel(te_ref, tr_ref, src_ref, x_hbm, g_ref, wg_ref, wu_ref, wd_ref, o_ref, xbuf, h_ref, sem):
    i = pl.program_id(0)
    f = pl.program_id(1)
    n_tiles = pl.num_programs(0)
    rows = tr_ref[i]
    tm = h_ref.shape[0]
    per_step = xbuf.shape[0] // pl.num_programs(1)

    def row_copy(idx, r):
        return pltpu.make_async_copy(x_hbm.at[pl.ds(idx, 1), :], xbuf.at[pl.ds(r, 1), :], sem)

    def start_rows(tile, step):
        for k in range(per_step):
            r = step * per_step + k
            row_copy(src_ref[tile * tm + jnp.minimum(r, tm - 1)], r).start()

    def wait_all():
        def wait_row(r, c):
            row_copy(0, r).wait()
            return c

        lax.fori_loop(0, xbuf.shape[0], wait_row, 0, unroll=4)

    is_first = i == 0
    is_last = i == n_tiles - 1
    nxt = jnp.minimum(i + 1, n_tiles - 1)
    prev_rows = tr_ref[jnp.maximum(i - 1, 0)]

    @pl.when(f == 0)
    def _():
        o_ref[...] = jnp.zeros_like(o_ref)

        @pl.when(jnp.logical_and(is_first, rows > 0))
        def _():
            def first(step, c):
                start_rows(0, step)
                return c

            lax.fori_loop(0, pl.num_programs(1), first, 0)

        @pl.when(jnp.where(is_first, rows, prev_rows) > 0)
        def _():
            wait_all()

        @pl.when(rows > 0)
        def _():
            h_ref[...] = _rms(xbuf[0:tm, :], g_ref[...]).astype(BF16)

    def run(n):
        start_rows(nxt, f)
        h = h_ref[0:n, :]
        gate = jnp.dot(h, wg_ref[0].astype(BF16), preferred_element_type=F32)
        up = jnp.dot(h, wu_ref[0].astype(BF16), preferred_element_type=F32)
        a = (jax.nn.silu(gate) * up).astype(BF16)
        o_ref[0:n, :] += jnp.dot(a, wd_ref[0].astype(BF16), preferred_element_type=F32)

    step = tm // MOE_ROW_STEPS
    for k in range(1, MOE_ROW_STEPS + 1):
        @pl.when(jnp.logical_and(rows > (k - 1) * step, rows <= k * step))
        def _(n=k * step):
            run(n)

    @pl.when(jnp.logical_and(jnp.logical_and(is_last, f == pl.num_programs(1) - 1), rows > 0))
    def _():
        wait_all()


def moe_ffn(x, g, src, tile_expert, tile_rows, w_gate_up, w_down, *, tm, tf):
    d = x.shape[1]
    p = src.shape[0]
    nf = D_FF // tf
    last = nf - 1
    per_step = -(-tm // nf)
    while (nf * per_step) % 8:
        per_step += 1
    gather_rows = nf * per_step

    def fsel(i, f, tr):
        return jnp.where(tr[i] > 0, f, last)

    return pl.pallas_call(
        _moe_ffn_kernel,
        grid_spec=pltpu.PrefetchScalarGridSpec(
            num_scalar_prefetch=3,
            grid=(p // tm, nf),
            in_specs=[pl.BlockSpec(memory_space=pl.ANY),
                      pl.BlockSpec((1, d), lambda i, f, te, tr, src: (0, 0)),
                      pl.BlockSpec((1, d, tf), lambda i, f, te, tr, src: (te[i], 0, fsel(i, f, tr))),
                      pl.BlockSpec((1, d, tf), lambda i, f, te, tr, src: (te[i], 0, fsel(i, f, tr) + nf)),
                      pl.BlockSpec((1, tf, d), lambda i, f, te, tr, src: (te[i], fsel(i, f, tr), 0))],
            out_specs=pl.BlockSpec((tm, d), lambda i, f, te, tr, src: (i, 0)),
            scratch_shapes=[pltpu.VMEM((gather_rows, d), F32), pltpu.VMEM((tm, d), BF16),
                            pltpu.SemaphoreType.DMA(())]),
        out_shape=jax.ShapeDtypeStruct((p, d), F32),
        compiler_params=_params(("arbitrary", "arbitrary")),
    )(tile_expert, tile_rows, src, x, g, w_gate_up, w_gate_up, w_down)


def _combine_kernel(p0_ref, p1_ref, x_ref, w_ref, ys_hbm, o_ref, buf, sem, *, tt):
    base = pl.program_id(0) * tt

    def issue(j, c):
        r0 = p0_ref[base + j]
        r1 = p1_ref[base + j]
        pltpu.make_async_copy(ys_hbm.at[pl.ds(r0, 1), :], buf.at[0, pl.ds(j, 1), :], sem.at[0]).start()
        pltpu.make_async_copy(ys_hbm.at[pl.ds(r1, 1), :], buf.at[1, pl.ds(j, 1), :], sem.at[1]).start()
        return c

    lax.fori_loop(0, tt, issue, 0, unroll=8)

    def drain(j, c):
        pltpu.make_async_copy(ys_hbm.at[pl.ds(0, 1), :], buf.at[0, pl.ds(j, 1), :], sem.at[0]).wait()
        pltpu.make_async_copy(ys_hbm.at[pl.ds(0, 1), :], buf.at[1, pl.ds(j, 1), :], sem.at[1]).wait()
        return c

    lax.fori_loop(0, tt, drain, 0, unroll=8)
    w = w_ref[...]
    o_ref[...] = x_ref[...] + (w[:, 0:1] * buf[0] + w[:, 1:2] * buf[1])


def moe_combine(x, wgt, ys, pos0, pos1, *, tt):
    m, d = x.shape
    return pl.pallas_call(
        functools.partial(_combine_kernel, tt=tt),
        grid_spec=pltpu.PrefetchScalarGridSpec(
            num_scalar_prefetch=2,
            grid=(m // tt,),
            in_specs=[pl.BlockSpec((tt, d), lambda i, p0, p1: (i, 0)),
                      pl.BlockSpec((tt, 128), lambda i, p0, p1: (i, 0)),
                      pl.BlockSpec(memory_space=pl.ANY)],
            out_specs=pl.BlockSpec((tt, d), lambda i, p0, p1: (i, 0)),
            scratch_shapes=[pltpu.VMEM((2, tt, d), F32), pltpu.SemaphoreType.DMA((2,))]),
        out_shape=jax.ShapeDtypeStruct((m, d), F32),
        compiler_params=_params(("arbitrary",)),
    )(pos0, pos1, x, wgt, ys)


def moe_layer(x, g, router, router_b, w_gate_up, w_down, expert_base, *, tm, tf):
    seq, d = x.shape
    w_pad = jnp.pad(router, ((0, 0), (0, 128 - N_EXPERTS)))
    b_pad = jnp.pad(router_b, (0, 128 - N_EXPERTS))[None, :]
    idx, wgt = moe_router(x, g, w_pad, b_pad, tm=min(512, seq))

    e_flat = idx[:, :TOP_K].reshape(-1)
    onehot = (e_flat[:, None] == jnp.arange(N_EXPERTS, dtype=jnp.int32)[None, :]).astype(jnp.int32)
    rank = jnp.cumsum(onehot, axis=0) - onehot
    counts = jnp.sum(onehot, axis=0)
    tiles_per = (counts + tm - 1) // tm
    tile_ends = jnp.cumsum(tiles_per)
    tile_starts = tile_ends - tiles_per
    gran = tm // MOE_ROW_STEPS
    per_tile = (((counts + jnp.maximum(tiles_per, 1) - 1) // jnp.maximum(tiles_per, 1) + gran - 1) // gran) * gran
    per_tile = jnp.maximum(per_tile, gran)
    row = (tile_starts * tm)[None, :] + (rank // per_tile[None, :]) * tm + rank % per_tile[None, :]
    pos = jnp.sum(onehot * row, axis=1).astype(jnp.int32)
    n_tiles = TOP_K * seq // tm + N_EXPERTS
    p_total = n_tiles * tm
    tok = jnp.arange(TOP_K * seq, dtype=jnp.int32) // TOP_K
    src = jnp.zeros((p_total,), jnp.int32).at[pos].set(tok)
    tile_idx = jnp.arange(n_tiles, dtype=jnp.int32)
    tile_expert = jnp.minimum(jnp.sum((tile_idx[:, None] >= tile_ends[None, :]).astype(jnp.int32), axis=1),
                              N_EXPERTS - 1).astype(jnp.int32)
    valid = tile_idx < tile_ends[-1]
    remaining = counts[tile_expert] - (tile_idx - tile_starts[tile_expert]) * per_tile[tile_expert]
    tile_rows = jnp.where(valid, jnp.minimum(remaining, per_tile[tile_expert]), 0).astype(jnp.int32)
    tile_expert = jnp.where(valid, tile_expert, jnp.max(jnp.where(valid, tile_expert, 0))).astype(jnp.int32)

    ys = moe_ffn(x, g, src, tile_expert + expert_base, tile_rows, w_gate_up, w_down, tm=tm, tf=tf)
    pos2 = pos.reshape(seq, TOP_K)
    return moe_combine(x, wgt, ys, pos2[:, 0], pos2[:, 1], tt=min(256, seq))


def _repack_w_q_b(w):
    w = w.reshape(MLA_Q_RANK, MLA_HEADS, MLA_QK)
    w = jnp.pad(w, ((0, CQ_W - MLA_Q_RANK), (0, 0), (0, MLA_QK_PAD - MLA_QK)))
    return w.reshape(CQ_W, MLA_HEADS * MLA_QK_PAD).astype(BF16)


def mixer_layer(x, mem, pos_col, pos_row, invf, p):
    seq = x.shape[0]
    tm = min(1024, seq)
    row = lambda v: v[None, :]
    proj, u = norm_matmul(x, row(p['norm_mix']), p['w_in'].astype(BF16), tm=tm, tn=S5_WIDTH)

    y = s5_mix(u, p['s5_a_re'], p['s5_a_im'], p['s5_log_dt'], p['s5_b_re'],
               p['s5_b_im'], p['s5_c_re'], p['s5_c_im'], p['s5_d'])

    ga = row(jnp.pad(p['mla_q_a_norm'], (0, CQ_W - MLA_Q_RANK)))
    kv_pad = (CKV_LO, CKV_W - CKV_LO - MLA_KV_RANK)
    gkv = row(jnp.pad(p['mla_kv_norm'], kv_pad))
    gq = row(jnp.pad(p['mla_q_norm'], (0, MLA_QK_PAD - MLA_QK)))
    gkn = row(p['mla_k_norm'][:MLA_NOPE])
    gkr = row(jnp.pad(p['mla_k_norm'][MLA_NOPE:], (0, 128 - MLA_ROPE)))
    wkv = jnp.pad(p['mla_w_kv_b'], (kv_pad, (0, 0))).astype(BF16)
    q, k, v = mla_prep(proj, pos_col, ga, gkv, gq, gkn, gkr, invf,
                       _repack_w_q_b(p['mla_w_q_b']), wkv, tm=min(256, seq))
    o = flash_attention(q, k, v, pos_col, pos_row, t=min(512, seq))

    km, vm = mem_kv(mem, row(p['mem_norm']), p['mem_w_kv'].astype(BF16), row(p['mem_k_norm']))
    om = mem_attention(proj, km, vm, row(p['mem_q_norm']), tm=min(512, seq))
    merged = merge_branches(y, o, om, p['s5_w_glu'].astype(BF16), p['mla_w_o'].astype(BF16),
                            p['mem_w_o'].astype(BF16), proj, tm=tm, tn=512)

    return resid_matmul(merged, p['w_out'].astype(BF16), x, tm=tm, tn=1024)


_LAYER_KEYS = ('norm_mix', 'w_in', 's5_a_re', 's5_a_im', 's5_log_dt', 's5_b_re', 's5_b_im', 's5_c_re',
               's5_c_im', 's5_d', 's5_w_glu', 'mla_q_a_norm', 'mla_w_q_b', 'mla_kv_norm', 'mla_w_kv_b',
               'mla_q_norm', 'mla_k_norm', 'mla_w_o', 'mem_norm', 'mem_w_kv', 'mem_q_norm', 'mem_k_norm',
               'mem_w_o', 'w_out')


def kernel(x, mem, positions, norm_mix, w_in, s5_a_re, s5_a_im, s5_log_dt, s5_b_re, s5_b_im, s5_c_re, s5_c_im, s5_d, s5_w_glu, mla_q_a_norm, mla_w_q_b, mla_kv_norm, mla_w_kv_b, mla_q_norm, mla_k_norm, mla_w_o, mem_norm, mem_w_kv, mem_q_norm, mem_k_norm, mem_w_o, w_out, norm_ffn, ffn_w_gate_up, ffn_w_down, moe_router, moe_router_b, moe_w_gate_up, moe_w_down):
    stacked = dict(zip(_LAYER_KEYS, (norm_mix, w_in, s5_a_re, s5_a_im, s5_log_dt, s5_b_re, s5_b_im, s5_c_re,
                                     s5_c_im, s5_d, s5_w_glu, mla_q_a_norm, mla_w_q_b, mla_kv_norm,
                                     mla_w_kv_b, mla_q_norm, mla_k_norm, mla_w_o, mem_norm, mem_w_kv,
                                     mem_q_norm, mem_k_norm, mem_w_o, w_out)))
    bsz, seq, d = x.shape
    depth = norm_mix.shape[0]
    half = MLA_ROPE // 2
    inv_freq = ROPE_THETA ** (-jnp.arange(0, MLA_ROPE, 2, dtype=F32) / MLA_ROPE)
    invf = jnp.concatenate([inv_freq, inv_freq, jnp.zeros((128 - 2 * half,), F32)])[None, :]
    moe_gu = moe_w_gate_up.reshape((-1,) + moe_w_gate_up.shape[2:])
    moe_dn = moe_w_down.reshape((-1,) + moe_w_down.shape[2:])

    outs = []
    for b in range(bsz):
        xb = x[b]
        pos_col = positions[b][:, None]
        pos_row = positions[b][None, :]
        for l in range(depth):
            p = {key: val[l] for key, val in stacked.items()}
            xb = mixer_layer(xb, mem[b], pos_col, pos_row, invf, p)
            gf = norm_ffn[l][None, :]
            if l % 2 == 0:
                xb = dense_ffn(xb, gf, ffn_w_gate_up[l // 2].astype(BF16), ffn_w_down[l // 2].astype(BF16),
                               tm=min(512, seq), tf=512)
            else:
                xb = moe_layer(xb, gf, moe_router[l // 2], moe_router_b[l // 2], moe_gu, moe_dn,
                               (l // 2) * N_EXPERTS, tm=min(1024, TOP_K * seq), tf=256)
        outs.append(xb)
    return jnp.stack(outs, axis=0)
```

```python
import functools
import math

import jax
import jax.numpy as jnp
from jax import lax
from jax.experimental import pallas as pl
from jax.experimental.pallas import tpu as pltpu

F32 = jnp.float32
BF16 = jnp.bfloat16
HIGHEST = lax.Precision.HIGHEST

D_MODEL = 2048
N_MEM = 256
S5_WIDTH = 1024
S5_GROUP = 16
S5_GROUPS = S5_WIDTH // S5_GROUP
S5_STATE = 64
S5_CHUNK = 32
S5_CK = S5_CHUNK * S5_GROUP
S5_GROUPS_PER_STEP = 128 // S5_GROUP
MLA_HEADS = 16
MLA_Q_RANK = 448
MLA_KV_RANK = 512
MLA_NOPE = 128
MLA_ROPE = 64
MLA_QK = MLA_NOPE + MLA_ROPE
MLA_V = 128
MLA_QK_PAD = 256
MLA_VP = 256
ROPE_THETA = 10000.0
MEM_HEADS = 4
MEM_HEAD_DIM = 256
MEM_WIDTH = MEM_HEADS * MEM_HEAD_DIM
N_BRANCH = 3
D_FF = 7168
N_EXPERTS = 8
TOP_K = 2
EPS = 1e-6
NEG_INF = -1e30

PJ_U = 0
PJ_MLA = 1024
PJ_MQ = 2048
PJ_GATE = 3072
MLA_W = MLA_Q_RANK + MLA_KV_RANK + MLA_ROPE
CQ_W = 512
CKV_OFF = 384
CKV_W = MLA_W - CKV_OFF
CKV_LO = MLA_Q_RANK - CKV_OFF
KR_OFF = MLA_W - 128

VMEM_LIMIT = 52 * 1024 * 1024


def _params(sem):
    return pltpu.CompilerParams(dimension_semantics=sem, vmem_limit_bytes=VMEM_LIMIT)


def _rms(x, g):
    r = lax.rsqrt(jnp.mean(x * x, axis=-1, keepdims=True) + EPS)
    return x * r * g


def _onehot_dot(a, b):
    if a.dtype == BF16:
        hi = b.astype(BF16)
        lo = (b - hi.astype(F32)).astype(BF16)
        return (jnp.dot(a, hi, preferred_element_type=F32) + jnp.dot(a, lo, preferred_element_type=F32))
    hi = a.astype(BF16)
    lo = (a - hi.astype(F32)).astype(BF16)
    return (jnp.dot(hi, b, preferred_element_type=F32) + jnp.dot(lo, b, preferred_element_type=F32))


def _norm_mm_kernel(x_ref, g_ref, w_ref, o_ref, first_ref, h_ref):
    @pl.when(pl.program_id(1) == 0)
    def _():
        h_ref[...] = _rms(x_ref[...], g_ref[...]).astype(BF16)

    res = jnp.dot(h_ref[...], w_ref[...], preferred_element_type=F32).astype(o_ref.dtype)
    o_ref[...] = res

    @pl.when(pl.program_id(1) == 0)
    def _():
        first_ref[...] = res


def norm_matmul(x, g, w, *, tm, tn, out_dtype=BF16):
    m, k = x.shape
    n = w.shape[1]
    return pl.pallas_call(
        _norm_mm_kernel,
        grid=(m // tm, n // tn),
        in_specs=[pl.BlockSpec((tm, k), lambda i, j: (i, 0)),
                  pl.BlockSpec((1, k), lambda i, j: (0, 0)),
                  pl.BlockSpec((k, tn), lambda i, j: (0, j))],
        out_specs=[pl.BlockSpec((tm, tn), lambda i, j: (i, j)),
                   pl.BlockSpec((tm, tn), lambda i, j: (i, 0))],
        out_shape=[jax.ShapeDtypeStruct((m, n), out_dtype),
                   jax.ShapeDtypeStruct((m, tn), out_dtype)],
        scratch_shapes=[pltpu.VMEM((tm, k), BF16)],
        compiler_params=_params(("parallel", "arbitrary")),
    )(x, g, w)


def _merge_kernel(y_ref, o_ref, om_ref, wa_ref, wb_ref, wo_ref, wm_ref, g0_ref, g1_ref, g2_ref, out_ref):
    y = y_ref[...]
    s5 = (jnp.dot(y, wa_ref[...], preferred_element_type=F32)
          * jax.nn.sigmoid(jnp.dot(y, wb_ref[...], preferred_element_type=F32)))
    mla = jnp.dot(o_ref[...], wo_ref[...], preferred_element_type=F32)
    mem = jnp.dot(om_ref[...], wm_ref[...], preferred_element_type=F32)
    gate = lambda r: jax.nn.sigmoid(r[...].astype(F32))
    out_ref[...] = (gate(g0_ref) * s5 + gate(g1_ref) * mla + gate(g2_ref) * mem).astype(out_ref.dtype)


def merge_branches(y, o, om, w_glu, w_o, w_mo, proj, *, tm, tn):
    m = y.shape[0]
    n = w_o.shape[1]
    nb = n // tn
    gb = PJ_GATE // tn
    lhs = lambda a: pl.BlockSpec((tm, a.shape[1]), lambda i, j: (i, 0))
    col = lambda w, off: pl.BlockSpec((w.shape[0], tn), lambda i, j: (0, j + off))
    gate = lambda b: pl.BlockSpec((tm, tn), lambda i, j: (i, gb + b * nb + j))
    return pl.pallas_call(
        _merge_kernel,
        grid=(m // tm, nb),
        in_specs=[lhs(y), lhs(o), lhs(om), col(w_glu, 0), col(w_glu, nb), col(w_o, 0), col(w_mo, 0),
                  gate(0), gate(1), gate(2)],
        out_specs=pl.BlockSpec((tm, tn), lambda i, j: (i, j)),
        out_shape=jax.ShapeDtypeStruct((m, n), BF16),
        compiler_params=_params(("parallel", "parallel")),
    )(y, o, om, w_glu, w_glu, w_o, w_mo, proj, proj, proj)


def _resid_mm_kernel(a_ref, w_ref, x_ref, o_ref):
    o_ref[...] = x_ref[...] + jnp.dot(a_ref[...], w_ref[...], preferred_element_type=F32)


def resid_matmul(a, w, x, *, tm, tn):
    m, k = a.shape
    n = w.shape[1]
    return pl.pallas_call(
        _resid_mm_kernel,
        grid=(m // tm, n // tn),
        in_specs=[pl.BlockSpec((tm, k), lambda i, j: (i, 0)),
                  pl.BlockSpec((k, tn), lambda i, j: (0, j)),
                  pl.BlockSpec((tm, tn), lambda i, j: (i, j))],
        out_specs=pl.BlockSpec((tm, tn), lambda i, j: (i, j)),
        out_shape=jax.ShapeDtypeStruct((m, n), F32),
        compiler_params=_params(("parallel", "parallel")),
    )(a, w, x)


def _s5_kernel(u_ref, perm_ref, rowp_ref, colp_ref, bt_ref, ct_ref, y_ref, toep_ref):
    ng = S5_GROUPS_PER_STEP
    w = ng * 128
    n_tiles = S5_CK // 128
    perm = perm_ref[...]
    parts = [jnp.dot(u_ref[0, :, a * w:(a + 1) * w], perm, preferred_element_type=F32).astype(BF16)
             for a in range(n_tiles)]
    ys = []
    for gi in range(ng):
        u = jnp.concatenate([part[:, gi * 128:(gi + 1) * 128] for part in parts], axis=1)
        ys.append(_s5_group(gi, u, rowp_ref, colp_ref, bt_ref, ct_ref, toep_ref))
    for a in range(n_tiles):
        ycat = jnp.concatenate([y[:, a * 128:(a + 1) * 128] for y in ys], axis=1)
        y_ref[0, :, a * w:(a + 1) * w] = lax.dot_general(
            ycat, perm, (((1,), (1,)), ((), ())), preferred_element_type=F32).astype(y_ref.dtype)


def _s5_group(gi, u, rowp_ref, colp_ref, bt_ref, ct_ref, toep_ref):
    t_chunk = S5_CHUNK
    n_chunks = u.shape[0]
    p2 = 2 * S5_STATE

    rowp = rowp_ref[gi]
    are_r, aim_r = rowp[0:1], rowp[1:2]
    dt_r = jnp.exp(rowp[2:3])
    colp = colp_ref[gi]
    are_c, aim_c = colp[:, 0:1], colp[:, 1:2]
    dt_c = jnp.exp(colp[:, 2:3])
    d_c = colp[0:S5_GROUP, 3:4]

    row_q = lax.broadcasted_iota(jnp.int32, (p2, S5_CK), 0)
    lane_q = lax.broadcasted_iota(jnp.int32, (1, p2), 1)

    tau = lax.broadcasted_iota(jnp.int32, (p2, p2), 1).astype(F32)
    row_pp = lax.broadcasted_iota(jnp.int32, (p2, p2), 0)
    mag = jnp.exp(tau * dt_c * are_c)
    ang = tau * dt_c * aim_c
    pw = mag * jnp.where(row_pp < S5_STATE, jnp.cos(ang), jnp.sin(ang))

    l_t = lax.broadcasted_iota(jnp.int32, (p2, S5_CK), 1) // S5_GROUP
    e_tau = (l_t == row_q).astype(BF16)
    e_tau1 = (l_t + 1 == row_q).astype(BF16)
    l_i = lax.broadcasted_iota(jnp.int32, (S5_GROUP, S5_CK), 1) % S5_GROUP
    e_ch = (l_i == lax.broadcasted_iota(jnp.int32, (S5_GROUP, S5_CK), 0)).astype(BF16)

    def swap_halves(x):
        return jnp.concatenate([x[S5_STATE:], x[:S5_STATE]], axis=0)

    x1 = _onehot_dot(pw, e_tau)
    x1s = _onehot_dot(pw, e_tau1)
    ct = ct_ref[gi]
    cta = _onehot_dot(ct[0], e_ch)
    ctb = _onehot_dot(ct[1], e_ch)
    ctb = jnp.where(row_q < S5_STATE, -ctb, ctb)
    z = cta * x1 + ctb * swap_halves(x1)
    zs = cta * x1s + ctb * swap_halves(x1s)
    wt = jnp.where(row_q < S5_STATE, zs, -zs)

    lam_mag = jnp.exp(dt_r * are_r)
    lam_re = lam_mag * jnp.cos(dt_r * aim_r)
    lam_im = lam_mag * jnp.sin(dt_r * aim_r)
    den = are_r * are_r + aim_r * aim_r
    n_re = lam_re - 1.0
    f_re = (n_re * are_r + lam_im * aim_r) / den
    f_im = (lam_im * are_r - n_re * aim_r) / den
    bt = bt_ref[gi]
    bb_re = f_re * bt[0] - f_im * bt[1]
    bb_im = f_re * bt[1] + f_im * bt[0]
    first = lane_q < S5_STATE

    kt = jnp.dot(jnp.where(first, bb_re, -bb_im), z, precision=HIGHEST, preferred_element_type=F32)
    lane_k = lax.broadcasted_iota(jnp.int32, (S5_GROUP, S5_CK), 1)
    row_k = lax.broadcasted_iota(jnp.int32, (S5_GROUP, S5_CK), 0)
    kt = kt + jnp.where(lane_k == row_k, d_c, 0.0)
    for s in range(t_chunk):
        off = S5_GROUP * s
        blk = kt if s == 0 else jnp.where(lane_k >= off, pltpu.roll(kt, off, 1), 0.0)
        toep_ref[gi, off:off + S5_GROUP, :] = blk.astype(BF16)

    e_s = (t_chunk - 1 - lax.broadcasted_iota(jnp.int32, (t_chunk, p2), 0)).astype(F32)
    pt_mag = jnp.exp(e_s * dt_r * are_r)
    pt_ang = e_s * dt_r * aim_r
    pt_re = pt_mag * jnp.cos(pt_ang)
    pt_im = pt_mag * jnp.sin(pt_ang)
    r_s = lax.broadcasted_iota(jnp.int32, (S5_CK, t_chunk), 0) // S5_GROUP
    e_rows = (r_s == lax.broadcasted_iota(jnp.int32, (S5_CK, t_chunk), 1)).astype(BF16)
    r_j = lax.broadcasted_iota(jnp.int32, (S5_CK, S5_GROUP), 0) % S5_GROUP
    e_rowj = (r_j == lax.broadcasted_iota(jnp.int32, (S5_CK, S5_GROUP), 1)).astype(BF16)
    gt = (_onehot_dot(e_rows, pt_re) * _onehot_dot(e_rowj, jnp.where(first, bb_re, bb_im))
          + _onehot_dot(e_rows, pt_im) * _onehot_dot(e_rowj, jnp.where(first, -bb_im, bb_re)))

    h = jnp.dot(u, gt.astype(BF16), preferred_element_type=F32)

    k_row = lax.broadcasted_iota(jnp.int32, (8, p2), 0)
    m_pow = (t_chunk * (1 << k_row)).astype(F32)
    m_mag = jnp.exp(m_pow * dt_r * are_r)
    m_ang = m_pow * dt_r * aim_r
    m_re = m_mag * jnp.cos(m_ang)
    m_im = m_mag * jnp.sin(m_ang)
    row_c = lax.broadcasted_iota(jnp.int32, (n_chunks, p2), 0)
    k = 0
    while (1 << k) < n_chunks:
        sh = 1 << k
        s_prev = jnp.where(row_c >= sh, pltpu.roll(h, sh, 0), 0.0)
        s_swap = pltpu.roll(s_prev, S5_STATE, 1)
        mb = jnp.where(first, -m_im[k:k + 1], m_im[k:k + 1])
        h = h + m_re[k:k + 1] * s_prev + mb * s_swap
        k += 1
    h_prev = jnp.where(row_c >= 1, pltpu.roll(h, 1, 0), 0.0)

    y = (jnp.dot(u, toep_ref[gi], preferred_element_type=F32)
         + jnp.dot(h_prev.astype(BF16), wt.astype(BF16), preferred_element_type=F32))
    return jax.nn.gelu(y).astype(BF16)


def s5_mix(u, a_re, a_im, log_dt, b_re, b_im, c_re, c_im, d):
    seq = u.shape[0]
    n_chunks = seq // S5_CHUNK
    g = S5_GROUPS
    ng = S5_GROUPS_PER_STEP
    steps = g // ng
    w = ng * S5_CK
    ug = u.reshape(n_chunks, S5_CHUNK, steps, 128).transpose(2, 0, 1, 3).reshape(steps, n_chunks, w)
    r = jnp.arange(ng * 128, dtype=jnp.int32)
    target = ((r // S5_GROUP) % ng) * 128 + (r // 128) * S5_GROUP + r % S5_GROUP
    perm = (r[None, :] == target[:, None]).astype(BF16)

    dup = lambda v: jnp.concatenate([v, v], axis=-1)
    zeros = jnp.zeros((g, 2 * S5_STATE), F32)
    ldt = jnp.broadcast_to(log_dt[:, None], (g, 2 * S5_STATE))
    rowp = jnp.stack([dup(a_re), dup(a_im), ldt] + [zeros] * 5, axis=1)
    d_pad = jnp.pad(d, ((0, 0), (0, 2 * S5_STATE - S5_GROUP)))
    colp = jnp.stack([dup(a_re), dup(a_im), ldt, d_pad] + [zeros] * 4, axis=2)
    bt = jnp.stack([dup(b_re.transpose(0, 2, 1)), dup(b_im.transpose(0, 2, 1))], axis=1)
    ctr, cti = c_re.transpose(0, 2, 1), c_im.transpose(0, 2, 1)
    ct = jnp.stack([jnp.concatenate([ctr, ctr], 1), jnp.concatenate([cti, cti], 1)], axis=1)

    yg = pl.pallas_call(
        _s5_kernel,
        grid=(steps,),
        in_specs=[pl.BlockSpec((1, n_chunks, w), lambda i: (i, 0, 0)),
                  pl.BlockSpec(perm.shape, lambda i: (0, 0)),
                  pl.BlockSpec((ng, 8, 2 * S5_STATE), lambda i: (i, 0, 0)),
                  pl.BlockSpec((ng, 2 * S5_STATE, 8), lambda i: (i, 0, 0)),
                  pl.BlockSpec((ng, 2, S5_GROUP, 2 * S5_STATE), lambda i: (i, 0, 0, 0)),
                  pl.BlockSpec((ng, 2, 2 * S5_STATE, S5_GROUP), lambda i: (i, 0, 0, 0))],
        out_specs=pl.BlockSpec((1, n_chunks, w), lambda i: (i, 0, 0)),
        out_shape=jax.ShapeDtypeStruct((steps, n_chunks, w), BF16),
        scratch_shapes=[pltpu.VMEM((ng, S5_CK, S5_CK), BF16)],
        compiler_params=_params(("parallel",)),
    )(ug, perm, rowp, colp, bt, ct)
    return yg.reshape(steps, n_chunks, S5_CHUNK, 128).transpose(1, 2, 0, 3).reshape(seq, S5_WIDTH)


def _mla_prep_kernel(mla_ref, pos_ref, ga_ref, gkv_ref, gq_ref, gkn_ref, gkr_ref,
                     invf_ref, wq_ref, wkv_ref, q_ref, k_ref, v_ref):
    cq = mla_ref[:, 0:CQ_W].astype(F32)
    lane = lax.broadcasted_iota(jnp.int32, cq.shape, 1)
    ssq = jnp.sum(jnp.where(lane < MLA_Q_RANK, cq * cq, 0.0), axis=-1, keepdims=True)
    hq = (cq * lax.rsqrt(ssq * (1.0 / MLA_Q_RANK) + EPS) * ga_ref[...]).astype(BF16)
    qf = jnp.dot(hq, wq_ref[...], preferred_element_type=F32)
    ckv = mla_ref[:, CKV_OFF:].astype(F32)
    lane_kv = lax.broadcasted_iota(jnp.int32, ckv.shape, 1)
    in_kv = jnp.logical_and(lane_kv >= CKV_LO, lane_kv < CKV_LO + MLA_KV_RANK)
    ssq_kv = jnp.sum(jnp.where(in_kv, ckv * ckv, 0.0), axis=-1, keepdims=True)
    hkv = (ckv * lax.rsqrt(ssq_kv * (1.0 / MLA_KV_RANK) + EPS) * gkv_ref[...]).astype(BF16)
    kvf = jnp.dot(hkv, wkv_ref[...], preferred_element_type=F32)

    l128 = lax.broadcasted_iota(jnp.int32, (cq.shape[0], 128), 1)
    half = MLA_ROPE // 2
    ang = pos_ref[...].astype(F32) * invf_ref[...]
    cosv = jnp.cos(ang)
    sinv = jnp.sin(ang)
    sgn_sin = jnp.where(l128 < half, -sinv, jnp.where(l128 < MLA_ROPE, sinv, 0.0))

    def rope(t):
        sw = jnp.where(l128 < half, pltpu.roll(t, 128 - half, 1), pltpu.roll(t, half, 1))
        return t * cosv + sw * sgn_sin

    kr = jnp.where(l128 < MLA_ROPE, pltpu.roll(mla_ref[:, KR_OFF:].astype(F32), MLA_ROPE, 1), 0.0)
    kr_ssq = jnp.sum(kr * kr, axis=-1, keepdims=True)
    kr_rot = rope(kr * gkr_ref[...])

    gq = gq_ref[...]
    gkn = gkn_ref[...]
    scale = MLA_QK ** -0.5
    inv_qk = 1.0 / MLA_QK
    ones_col = jnp.where(l128 == 0, 1.0, 0.0).astype(BF16)
    for h in range(MLA_HEADS):
        qh = qf[:, MLA_QK_PAD * h:MLA_QK_PAD * (h + 1)]
        rq = lax.rsqrt(jnp.sum(qh * qh, axis=-1, keepdims=True) * inv_qk + EPS)
        qn = qh * rq * gq
        q_ref[h] = (jnp.concatenate([qn[:, :MLA_NOPE], rope(qn[:, MLA_NOPE:])], axis=1) * scale).astype(BF16)
        kn = kvf[:, MLA_QK_PAD * h:MLA_QK_PAD * h + MLA_NOPE]
        rk = lax.rsqrt((jnp.sum(kn * kn, axis=-1, keepdims=True) + kr_ssq) * inv_qk + EPS)
        k_ref[h] = jnp.concatenate([kn * rk * gkn, kr_rot * rk], axis=1).astype(BF16)
        vh = kvf[:, MLA_QK_PAD * h + MLA_NOPE:MLA_QK_PAD * (h + 1)].astype(BF16)
        v_ref[h] = jnp.concatenate([vh, ones_col], axis=1)


def mla_prep(proj, pos_col, ga, gkv, gq, gkn, gkr, invf, wq, wkv, *, tm):
    seq = proj.shape[0]
    h = MLA_HEADS
    full = lambda shape: pl.BlockSpec(shape, lambda i: (0,) * len(shape))
    return pl.pallas_call(
        _mla_prep_kernel,
        grid=(seq // tm,),
        in_specs=[pl.BlockSpec((tm, MLA_W), lambda i: (i, PJ_MLA // MLA_W)),
                  pl.BlockSpec((tm, 1), lambda i: (i, 0)),
                  full((1, CQ_W)), full((1, CKV_W)), full((1, MLA_QK_PAD)), full((1, 128)), full((1, 128)),
                  full((1, 128)), full((CQ_W, h * MLA_QK_PAD)), full((CKV_W, h * MLA_QK_PAD))],
        out_specs=[pl.BlockSpec((h, tm, MLA_QK_PAD), lambda i: (0, i, 0)),
                   pl.BlockSpec((h, tm, MLA_QK_PAD), lambda i: (0, i, 0)),
                   pl.BlockSpec((h, tm, MLA_VP), lambda i: (0, i, 0))],
        out_shape=[jax.ShapeDtypeStruct((h, seq, MLA_QK_PAD), BF16),
                   jax.ShapeDtypeStruct((h, seq, MLA_QK_PAD), BF16),
                   jax.ShapeDtypeStruct((h, seq, MLA_VP), BF16)],
        compiler_params=_params(("parallel",)),
    )(proj, pos_col, ga, gkv, gq, gkn, gkr, invf, wq, wkv)


def _flash_kernel(q_ref, k_ref, v_ref, pq_ref, pk_ref, o_ref, sa, sb, pa, pb, ala, alb, m_ref, acc_ref, *, t):
    nq = q_ref.shape[1] // t
    nt = (((1,), (1,)), ((), ()))

    def blk(j):
        return pl.ds(pl.multiple_of(j * t, t), t)

    def score(qt, j, s_out):
        s_out[...] = lax.dot_general(q_ref[0, blk(qt), :], k_ref[0, blk(j), :], nt,
                                     preferred_element_type=F32)

    score(0, 0, sa)

    def tile(qi, carry):
        _flash_tile(qi, nq, blk, score, v_ref, pq_ref, pk_ref, o_ref, sa, sb, pa, pb, ala, alb, m_ref, acc_ref, t)
        return carry

    lax.fori_loop(0, nq, tile, 0)


def _flash_tile(qi, nq, blk, score, v_ref, pq_ref, pk_ref, o_ref, sa, sb, pa, pb, ala, alb, m_ref, acc_ref, t):
    def stage_a(j, s_out):
        score(qi, j, s_out)

    def stage_b(j, s_in, p_out, al_out, masked):
        s = s_in[...]
        if masked:
            s = jnp.where(pk_ref[:, blk(j)] <= pq_ref[blk(qi), :], s, NEG_INF)
        m_prev = m_ref[...]
        m_new = jnp.maximum(m_prev, jnp.max(s, axis=-1, keepdims=True))
        al_out[...] = jnp.exp(m_prev - m_new)
        p_out[...] = jnp.exp((s - jnp.concatenate([m_new] * (t // 128), axis=1)).astype(BF16))
        m_ref[...] = m_new

    def stage_c(j, p_in, al_in):
        al = al_in[...]
        acc_ref[...] = (jnp.concatenate([al] * (MLA_VP // 128), axis=1) * acc_ref[...]
                        + jnp.dot(p_in[...], v_ref[0, blk(j), :], preferred_element_type=F32))

    m_ref[...] = jnp.full(m_ref.shape, NEG_INF, F32)
    acc_ref[...] = jnp.zeros(acc_ref.shape, F32)
    pb[...] = jnp.zeros(pb.shape, BF16)
    alb[...] = jnp.ones(alb.shape, F32)
    nxt = jnp.minimum(qi + 1, nq - 1)

    def finish():
        acc = acc_ref[...]
        o_ref[blk(qi), :] = (acc[:, :MLA_V] / acc[:, MLA_V:MLA_V + 1]).astype(o_ref.dtype)

    def pair_body(j):
        stage_a(j + 1, sb)
        stage_b(j, sa, pa, ala, False)
        stage_c(jnp.maximum(j - 1, 0), pb, alb)
        stage_a(j + 2, sa)
        stage_b(j + 1, sb, pb, alb, False)
        stage_c(j, pa, ala)

    def quad(i, c):
        pair_body(4 * i)
        pair_body(4 * i + 2)
        return c

    lax.fori_loop(0, qi // 4, quad, 0)

    @pl.when((qi // 2) % 2 == 1)
    def _():
        pair_body(4 * (qi // 4))

    @pl.when(qi % 2 == 0)
    def _():
        stage_b(qi, sa, pa, ala, True)
        stage_c(jnp.maximum(qi - 1, 0), pb, alb)
        score(nxt, 0, sa)
        stage_c(qi, pa, ala)
        finish()

    @pl.when(qi % 2 == 1)
    def _():
        stage_a(qi, sb)
        stage_b(qi - 1, sa, pa, ala, False)
        stage_c(jnp.maximum(qi - 2, 0), pb, alb)
        score(nxt, 0, sa)
        stage_b(qi, sb, pb, alb, True)
        stage_c(qi - 1, pa, ala)
        stage_c(qi, pb, alb)
        finish()


def flash_attention(q, k, v, pos_col, pos_row, *, t):
    h, seq, _ = q.shape
    return pl.pallas_call(
        functools.partial(_flash_kernel, t=t),
        grid=(h,),
        in_specs=[pl.BlockSpec((1, seq, MLA_QK_PAD), lambda hh: (hh, 0, 0)),
                  pl.BlockSpec((1, seq, MLA_QK_PAD), lambda hh: (hh, 0, 0)),
                  pl.BlockSpec((1, seq, MLA_VP), lambda hh: (hh, 0, 0)),
                  pl.BlockSpec((seq, 1), lambda hh: (0, 0)),
                  pl.BlockSpec((1, seq), lambda hh: (0, 0))],
        out_specs=pl.BlockSpec((seq, MLA_V), lambda hh: (0, hh)),
        out_shape=jax.ShapeDtypeStruct((seq, h * MLA_V), BF16),
        scratch_shapes=[pltpu.VMEM((t, t), F32), pltpu.VMEM((t, t), F32),
                        pltpu.VMEM((t, t), BF16), pltpu.VMEM((t, t), BF16),
                        pltpu.VMEM((t, 128), F32), pltpu.VMEM((t, 128), F32),
                        pltpu.VMEM((t, 128), F32), pltpu.VMEM((t, MLA_VP), F32)],
        compiler_params=_params(("parallel",)),
    )(q, k, v, pos_col, pos_row)


def _mem_kv_kernel(mem_ref, g_ref, w_ref, gk_ref, k_ref, v_ref):
    m = _rms(mem_ref[...], g_ref[...]).astype(BF16)
    kv = jnp.dot(m, w_ref[...], preferred_element_type=F32)
    for h in range(MEM_HEADS):
        kh = kv[:, MEM_HEAD_DIM * h:MEM_HEAD_DIM * (h + 1)]
        k_ref[:, MEM_HEAD_DIM * h:MEM_HEAD_DIM * (h + 1)] = _rms(kh, gk_ref[...]).astype(BF16)
    v_ref[...] = kv[:, MEM_WIDTH:].astype(BF16)


def mem_kv(mem, g, w, gk):
    n_mem = mem.shape[0]
    full = lambda shape: pl.BlockSpec(shape, lambda i: (0,) * len(shape))
    return pl.pallas_call(
        _mem_kv_kernel,
        grid=(1,),
        in_specs=[full(mem.shape), full(g.shape), full(w.shape), full(gk.shape)],
        out_specs=[full((n_mem, MEM_WIDTH)), full((n_mem, MEM_WIDTH))],
        out_shape=[jax.ShapeDtypeStruct((n_mem, MEM_WIDTH), BF16)] * 2,
        compiler_params=_params(("arbitrary",)),
    )(mem, g, w, gk)


def _mem_attn_kernel(q_ref, k_ref, v_ref, gq_ref, o_ref):
    scale = MEM_HEAD_DIM ** -0.5
    for h in range(MEM_HEADS):
        sl = slice(MEM_HEAD_DIM * h, MEM_HEAD_DIM * (h + 1))
        qh = (_rms(q_ref[:, sl].astype(F32), gq_ref[...]) * scale).astype(BF16)
        s = lax.dot_general(qh, k_ref[:, sl], (((1,), (1,)), ((), ())), preferred_element_type=F32)
        p = jnp.exp(s - jnp.max(s, axis=-1, keepdims=True))
        o = jnp.dot(p.astype(BF16), v_ref[:, sl], preferred_element_type=F32)
        o_ref[:, sl] = (o / jnp.sum(p, axis=-1, keepdims=True)).astype(o_ref.dtype)


def mem_attention(proj, k, v, gq, *, tm):
    seq = proj.shape[0]
    n_mem = k.shape[0]
    return pl.pallas_call(
        _mem_attn_kernel,
        grid=(seq // tm,),
        in_specs=[pl.BlockSpec((tm, MEM_WIDTH), lambda i: (i, PJ_MQ // MEM_WIDTH)),
                  pl.BlockSpec((n_mem, MEM_WIDTH), lambda i: (0, 0)),
                  pl.BlockSpec((n_mem, MEM_WIDTH), lambda i: (0, 0)),
                  pl.BlockSpec((1, MEM_HEAD_DIM), lambda i: (0, 0))],
        out_specs=pl.BlockSpec((tm, MEM_WIDTH), lambda i: (i, 0)),
        out_shape=jax.ShapeDtypeStruct((seq, MEM_WIDTH), BF16),
        compiler_params=_params(("parallel",)),
    )(proj, k, v, gq)


def _ffn_kernel(x_ref, g_ref, wg_ref, wu_ref, wd_ref, o_ref, h_ref):
    f = pl.program_id(1)

    @pl.when(f == 0)
    def _():
        x = x_ref[...]
        h_ref[...] = _rms(x, g_ref[...]).astype(BF16)
        o_ref[...] = x

    h = h_ref[...]
    gate = jnp.dot(h, wg_ref[...], preferred_element_type=F32)
    up = jnp.dot(h, wu_ref[...], preferred_element_type=F32)
    a = (jax.nn.silu(gate) * up).astype(BF16)
    o_ref[...] += jnp.dot(a, wd_ref[...], preferred_element_type=F32)


def dense_ffn(x, g, w_gate_up, w_down, *, tm, tf):
    m, d = x.shape
    nf = D_FF // tf
    return pl.pallas_call(
        _ffn_kernel,
        grid=(m // tm, nf),
        in_specs=[pl.BlockSpec((tm, d), lambda i, f: (i, 0)),
                  pl.BlockSpec((1, d), lambda i, f: (0, 0)),
                  pl.BlockSpec((d, tf), lambda i, f: (0, f)),
                  pl.BlockSpec((d, tf), lambda i, f: (0, f + nf)),
                  pl.BlockSpec((tf, d), lambda i, f: (f, 0))],
        out_specs=pl.BlockSpec((tm, d), lambda i, f: (i, 0)),
        out_shape=jax.ShapeDtypeStruct((m, d), F32),
        scratch_shapes=[pltpu.VMEM((tm, d), BF16)],
        compiler_params=_params(("parallel", "arbitrary")),
    )(x, g, w_gate_up, w_gate_up, w_down)


def _router_kernel(x_ref, g_ref, w_ref, b_ref, idx_ref, wgt_ref):
    h = _rms(x_ref[...], g_ref[...])
    logits = jnp.dot(h, w_ref[...], precision=HIGHEST, preferred_element_type=F32) + b_ref[...]
    lane = lax.broadcasted_iota(jnp.int32, logits.shape, 1)
    logits = jnp.where(lane < N_EXPERTS, logits, -jnp.inf)
    v1 = jnp.max(logits, axis=-1, keepdims=True)
    i1 = jnp.min(jnp.where(logits == v1, lane, 128), axis=-1, keepdims=True)
    rest = jnp.where(lane == i1, -jnp.inf, logits)
    v2 = jnp.max(rest, axis=-1, keepdims=True)
    i2 = jnp.min(jnp.where(rest == v2, lane, 128), axis=-1, keepdims=True)
    e2 = jnp.exp(v2 - v1)
    w1 = 1.0 / (1.0 + e2)
    w2 = e2 / (1.0 + e2)
    idx_ref[...] = jnp.where(lane == 0, i1, jnp.where(lane == 1, i2, 0))
    wgt_ref[...] = jnp.where(lane == 0, w1, jnp.where(lane == 1, w2, 0.0))


def moe_router(x, g, w_pad, b_pad, *, tm):
    m, d = x.shape
    return pl.pallas_call(
        _router_kernel,
        grid=(m // tm,),
        in_specs=[pl.BlockSpec((tm, d), lambda i: (i, 0)),
                  pl.BlockSpec((1, d), lambda i: (0, 0)),
                  pl.BlockSpec((d, 128), lambda i: (0, 0)),
                  pl.BlockSpec((1, 128), lambda i: (0, 0))],
        out_specs=[pl.BlockSpec((tm, 128), lambda i: (i, 0)),
                   pl.BlockSpec((tm, 128), lambda i: (i, 0))],
        out_shape=[jax.ShapeDtypeStruct((m, 128), jnp.int32),
                   jax.ShapeDtypeStruct((m, 128), F32)],
        compiler_params=_params(("parallel",)),
    )(x, g, w_pad, b_pad)


MOE_ROW_STEPS = 6


def _moe_ffn_kernel(te_ref, tr_ref, src_ref, x_hbm, g_ref, wg_ref, wu_ref, wd_ref, o_ref, xbuf, h_ref, sem):
    i = pl.program_id(0)
    f = pl.program_id(1)
    n_tiles = pl.num_programs(0)
    rows = tr_ref[i]
    tm = h_ref.shape[0]
    per_step = xbuf.shape[0] // pl.num_programs(1)

    def row_copy(idx, r):
        return pltpu.make_async_copy(x_hbm.at[pl.ds(idx, 1), :], xbuf.at[pl.ds(r, 1), :], sem)

    def start_rows(tile, step):
        for k in range(per_step):
            r = step * per_step + k
            row_copy(src_ref[tile * tm + jnp.minimum(r, tm - 1)], r).start()

    def wait_all():
        def wait_row(r, c):
            row_copy(0, r).wait()
            return c

        lax.fori_loop(0, xbuf.shape[0], wait_row, 0, unroll=4)

    is_first = i == 0
    is_last = i == n_tiles - 1
    nxt = jnp.minimum(i + 1, n_tiles - 1)
    prev_rows = tr_ref[jnp.maximum(i - 1, 0)]

    @pl.when(f == 0)
    def _():
        o_ref[...] = jnp.zeros_like(o_ref)

        @pl.when(jnp.logical_and(is_first, rows > 0))
        def _():
            def first(step, c):
                start_rows(0, step)
                return c

            lax.fori_loop(0, pl.num_programs(1), first, 0)

        @pl.when(jnp.where(is_first, rows, prev_rows) > 0)
        def _():
            wait_all()

        @pl.when(rows > 0)
        def _():
            h_ref[...] = _rms(xbuf[0:tm, :], g_ref[...]).astype(BF16)

    def run(n):
        start_rows(nxt, f)
        h = h_ref[0:n, :]
        gate = jnp.dot(h, wg_ref[0].astype(BF16), preferred_element_type=F32)
        up = jnp.dot(h, wu_ref[0].astype(BF16), preferred_element_type=F32)
        a = (jax.nn.silu(gate) * up).astype(BF16)
        o_ref[0:n, :] += jnp.dot(a, wd_ref[0].astype(BF16), preferred_element_type=F32)

    step = tm // MOE_ROW_STEPS
    for k in range(1, MOE_ROW_STEPS + 1):
        @pl.when(jnp.logical_and(rows > (k - 1) * step, rows <= k * step))
        def _(n=k * step):
            run(n)

    @pl.when(jnp.logical_and(jnp.logical_and(is_last, f == pl.num_programs(1) - 1), rows > 0))
    def _():
        wait_all()


def moe_ffn(x, g, src, tile_expert, tile_rows, w_gate_up, w_down, *, tm, tf):
    d = x.shape[1]
    p = src.shape[0]
    nf = D_FF // tf
    last = nf - 1
    per_step = -(-tm // nf)
    while (nf * per_step) % 8:
        per_step += 1
    gather_rows = nf * per_step

    def fsel(i, f, tr):
        return jnp.where(tr[i] > 0, f, last)

    return pl.pallas_call(
        _moe_ffn_kernel,
        grid_spec=pltpu.PrefetchScalarGridSpec(
            num_scalar_prefetch=3,
            grid=(p // tm, nf),
            in_specs=[pl.BlockSpec(memory_space=pl.ANY),
                      pl.BlockSpec((1, d), lambda i, f, te, tr, src: (0, 0)),
                      pl.BlockSpec((1, d, tf), lambda i, f, te, tr, src: (te[i], 0, fsel(i, f, tr))),
                      pl.BlockSpec((1, d, tf), lambda i, f, te, tr, src: (te[i], 0, fsel(i, f, tr) + nf)),
                      pl.BlockSpec((1, tf, d), lambda i, f, te, tr, src: (te[i], fsel(i, f, tr), 0))],
            out_specs=pl.BlockSpec((tm, d), lambda i, f, te, tr, src: (i, 0)),
            scratch_shapes=[pltpu.VMEM((gather_rows, d), F32), pltpu.VMEM((tm, d), BF16),
                            pltpu.SemaphoreType.DMA(())]),
        out_shape=jax.ShapeDtypeStruct((p, d), F32),
        compiler_params=_params(("arbitrary", "arbitrary")),
    )(tile_expert, tile_rows, src, x, g, w_gate_up, w_gate_up, w_down)


def _combine_kernel(p0_ref, p1_ref, x_ref, w_ref, ys_hbm, o_ref, buf, sem, *, tt):
    base = pl.program_id(0) * tt

    def issue(j, c):
        r0 = p0_ref[base + j]
        r1 = p1_ref[base + j]
        pltpu.make_async_copy(ys_hbm.at[pl.ds(r0, 1), :], buf.at[0, pl.ds(j, 1), :], sem.at[0]).start()
        pltpu.make_async_copy(ys_hbm.at[pl.ds(r1, 1), :], buf.at[1, pl.ds(j, 1), :], sem.at[1]).start()
        return c

    lax.fori_loop(0, tt, issue, 0, unroll=8)

    def drain(j, c):
        pltpu.make_async_copy(ys_hbm.at[pl.ds(0, 1), :], buf.at[0, pl.ds(j, 1), :], sem.at[0]).wait()
        pltpu.make_async_copy(ys_hbm.at[pl.ds(0, 1), :], buf.at[1, pl.ds(j, 1), :], sem.at[1]).wait()
        return c

    lax.fori_loop(0, tt, drain, 0, unroll=8)
    w = w_ref[...]
    o_ref[...] = x_ref[...] + (w[:, 0:1] * buf[0] + w[:, 1:2] * buf[1])


def moe_combine(x, wgt, ys, pos0, pos1, *, tt):
    m, d = x.shape
    return pl.pallas_call(
        functools.partial(_combine_kernel, tt=tt),
        grid_spec=pltpu.PrefetchScalarGridSpec(
            num_scalar_prefetch=2,
            grid=(m // tt,),
            in_specs=[pl.BlockSpec((tt, d), lambda i, p0, p1: (i, 0)),
                      pl.BlockSpec((tt, 128), lambda i, p0, p1: (i, 0)),
                      pl.BlockSpec(memory_space=pl.ANY)],
            out_specs=pl.BlockSpec((tt, d), lambda i, p0, p1: (i, 0)),
            scratch_shapes=[pltpu.VMEM((2, tt, d), F32), pltpu.SemaphoreType.DMA((2,))]),
        out_shape=jax.ShapeDtypeStruct((m, d), F32),
        compiler_params=_params(("arbitrary",)),
    )(pos0, pos1, x, wgt, ys)


def moe_layer(x, g, router, router_b, w_gate_up, w_down, expert_base, *, tm, tf):
    seq, d = x.shape
    w_pad = jnp.pad(router, ((0, 0), (0, 128 - N_EXPERTS)))
    b_pad = jnp.pad(router_b, (0, 128 - N_EXPERTS))[None, :]
    idx, wgt = moe_router(x, g, w_pad, b_pad, tm=min(512, seq))

    e_flat = idx[:, :TOP_K].reshape(-1)
    onehot = (e_flat[:, None] == jnp.arange(N_EXPERTS, dtype=jnp.int32)[None, :]).astype(jnp.int32)
    rank = jnp.cumsum(onehot, axis=0) - onehot
    counts = jnp.sum(onehot, axis=0)
    tiles_per = (counts + tm - 1) // tm
    tile_ends = jnp.cumsum(tiles_per)
    tile_starts = tile_ends - tiles_per
    gran = tm // MOE_ROW_STEPS
    per_tile = (((counts + jnp.maximum(tiles_per, 1) - 1) // jnp.maximum(tiles_per, 1) + gran - 1) // gran) * gran
    per_tile = jnp.maximum(per_tile, gran)
    row = (tile_starts * tm)[None, :] + (rank // per_tile[None, :]) * tm + rank % per_tile[None, :]
    pos = jnp.sum(onehot * row, axis=1).astype(jnp.int32)
    n_tiles = -(-TOP_K * seq // tm) + N_EXPERTS
    p_total = n_tiles * tm
    tok = jnp.arange(TOP_K * seq, dtype=jnp.int32) // TOP_K
    src = jnp.zeros((p_total,), jnp.int32).at[pos].set(tok)
    tile_idx = jnp.arange(n_tiles, dtype=jnp.int32)
    tile_expert = jnp.minimum(jnp.sum((tile_idx[:, None] >= tile_ends[None, :]).astype(jnp.int32), axis=1),
                              N_EXPERTS - 1).astype(jnp.int32)
    valid = tile_idx < tile_ends[-1]
    remaining = counts[tile_expert] - (tile_idx - tile_starts[tile_expert]) * per_tile[tile_expert]
    tile_rows = jnp.where(valid, jnp.minimum(remaining, per_tile[tile_expert]), 0).astype(jnp.int32)
    tile_expert = jnp.where(valid, tile_expert, jnp.max(jnp.where(valid, tile_expert, 0))).astype(jnp.int32)

    ys = moe_ffn(x, g, src, tile_expert + expert_base, tile_rows, w_gate_up, w_down, tm=tm, tf=tf)
    pos2 = pos.reshape(seq, TOP_K)
    return moe_combine(x, wgt, ys, pos2[:, 0], pos2[:, 1], tt=min(256, seq))


def _repack_w_q_b(w):
    w = w.reshape(MLA_Q_RANK, MLA_HEADS, MLA_QK)
    w = jnp.pad(w, ((0, CQ_W - MLA_Q_RANK), (0, 0), (0, MLA_QK_PAD - MLA_QK)))
    return w.reshape(CQ_W, MLA_HEADS * MLA_QK_PAD).astype(BF16)


def mixer_layer(x, mem, pos_col, pos_row, invf, p):
    seq = x.shape[0]
    tm = min(1024, seq)
    row = lambda v: v[None, :]
    proj, u = norm_matmul(x, row(p['norm_mix']), p['w_in'].astype(BF16), tm=tm, tn=S5_WIDTH)

    y = s5_mix(u, p['s5_a_re'], p['s5_a_im'], p['s5_log_dt'], p['s5_b_re'],
               p['s5_b_im'], p['s5_c_re'], p['s5_c_im'], p['s5_d'])

    ga = row(jnp.pad(p['mla_q_a_norm'], (0, CQ_W - MLA_Q_RANK)))
    kv_pad = (CKV_LO, CKV_W - CKV_LO - MLA_KV_RANK)
    gkv = row(jnp.pad(p['mla_kv_norm'], kv_pad))
    gq = row(jnp.pad(p['mla_q_norm'], (0, MLA_QK_PAD - MLA_QK)))
    gkn = row(p['mla_k_norm'][:MLA_NOPE])
    gkr = row(jnp.pad(p['mla_k_norm'][MLA_NOPE:], (0, 128 - MLA_ROPE)))
    wkv = jnp.pad(p['mla_w_kv_b'], (kv_pad, (0, 0))).astype(BF16)
    q, k, v = mla_prep(proj, pos_col, ga, gkv, gq, gkn, gkr, invf,
                       _repack_w_q_b(p['mla_w_q_b']), wkv, tm=min(256, seq))
    o = flash_attention(q, k, v, pos_col, pos_row, t=min(512, seq))

    km, vm = mem_kv(mem, row(p['mem_norm']), p['mem_w_kv'].astype(BF16), row(p['mem_k_norm']))
    om = mem_attention(proj, km, vm, row(p['mem_q_norm']), tm=min(512, seq))
    merged = merge_branches(y, o, om, p['s5_w_glu'].astype(BF16), p['mla_w_o'].astype(BF16),
                            p['mem_w_o'].astype(BF16), proj, tm=tm, tn=512)

    return resid_matmul(merged, p['w_out'].astype(BF16), x, tm=tm, tn=1024)


_LAYER_KEYS = ('norm_mix', 'w_in', 's5_a_re', 's5_a_im', 's5_log_dt', 's5_b_re', 's5_b_im', 's5_c_re',
               's5_c_im', 's5_d', 's5_w_glu', 'mla_q_a_norm', 'mla_w_q_b', 'mla_kv_norm', 'mla_w_kv_b',
               'mla_q_norm', 'mla_k_norm', 'mla_w_o', 'mem_norm', 'mem_w_kv', 'mem_q_norm', 'mem_k_norm',
               'mem_w_o', 'w_out')


def kernel(x, mem, positions, norm_mix, w_in, s5_a_re, s5_a_im, s5_log_dt, s5_b_re, s5_b_im, s5_c_re, s5_c_im, s5_d, s5_w_glu, mla_q_a_norm, mla_w_q_b, mla_kv_norm, mla_w_kv_b, mla_q_norm, mla_k_norm, mla_w_o, mem_norm, mem_w_kv, mem_q_norm, mem_k_norm, mem_w_o, w_out, norm_ffn, ffn_w_gate_up, ffn_w_down, moe_router, moe_router_b, moe_w_gate_up, moe_w_down):
    stacked = dict(zip(_LAYER_KEYS, (norm_mix, w_in, s5_a_re, s5_a_im, s5_log_dt, s5_b_re, s5_b_im, s5_c_re,
                                     s5_c_im, s5_d, s5_w_glu, mla_q_a_norm, mla_w_q_b, mla_kv_norm,
                                     mla_w_kv_b, mla_q_norm, mla_k_norm, mla_w_o, mem_norm, mem_w_kv,
                                     mem_q_norm, mem_k_norm, mem_w_o, w_out)))
    bsz, seq, d = x.shape
    depth = norm_mix.shape[0]
    half = MLA_ROPE // 2
    inv_freq = ROPE_THETA ** (-jnp.arange(0, MLA_ROPE, 2, dtype=F32) / MLA_ROPE)
    invf = jnp.concatenate([inv_freq, inv_freq, jnp.zeros((128 - 2 * half,), F32)])[None, :]
    moe_gu = moe_w_gate_up.reshape((-1,) + moe_w_gate_up.shape[2:])
    moe_dn = moe_w_down.reshape((-1,) + moe_w_down.shape[2:])

    outs = []
    for b in range(bsz):
        xb = x[b]
        pos_col = positions[b][:, None]
        pos_row = positions[b][None, :]
        for l in range(depth):
            p = {key: val[l] for key, val in stacked.items()}
            xb = mixer_layer(xb, mem[b], pos_col, pos_row, invf, p)
            gf = norm_ffn[l][None, :]
            if l % 2 == 0:
                xb = dense_ffn(xb, gf, ffn_w_gate_up[l // 2].astype(BF16), ffn_w_down[l // 2].astype(BF16),
                               tm=min(512, seq), tf=512)
            else:
                xb = moe_layer(xb, gf, moe_router[l // 2], moe_router_b[l // 2], moe_gu, moe_dn,
                               (l // 2) * N_EXPERTS, tm=min(768, TOP_K * seq), tf=512)
        outs.append(xb)
    return jnp.stack(outs, axis=0)
```
